```python
import math
import jax, jax.numpy as jnp
from jax import lax
import numpy as np

D_MODEL = 2048
BATCH = 2
SEQ = 4096
DEPTH = 4
DEC_BATCH = 8
DEC_SEQ = 4
PAST_LEN = 16384
PAGE_SIZE = 128

N_A_LAYERS = DEPTH // 2
N_B_LAYERS = DEPTH - N_A_LAYERS
SSM_GROUP = 16
N_GROUPS = D_MODEL // SSM_GROUP
SSM_STATE = 64
SCAN_CHUNK = 128
D_FF = 4 * D_MODEL
N_HEADS = 16
HEAD_DIM = D_MODEL // N_HEADS
N_KV = 4
GROUP_Q = N_HEADS // N_KV
ROT_DIM = HEAD_DIM // 4
ROPE_THETA = 500000.0
N_BRANCH = 3
CMP_BLOCK = 32
CMP_STRIDE = 16
CMP_HIDDEN = 2 * HEAD_DIM
SLC_BLOCK = 64
N_SELECT = 16
WINDOW = 512
Q_BLOCK = 32
FORCE_SCORE = 1.0e4
EPS = 1e-6
Q_WIDTH = N_HEADS * HEAD_DIM
KV_WIDTH = N_BRANCH * 2 * N_KV * HEAD_DIM

kernel_name = 'yoco_s5_nsa_decoder_step'


def rmsnorm(x, g):
    xf = x.astype(jnp.float32)
    var = jnp.mean(xf * xf, axis=-1, keepdims=True)
    return (xf * lax.rsqrt(var + EPS) * g.astype(jnp.float32)).astype(x.dtype)


def partial_rope(x, pos):
    half = ROT_DIM // 2
    inv = ROPE_THETA ** (-jnp.arange(half, dtype=jnp.float32) / half)
    ang = pos.astype(jnp.float32)[:, None] * inv[None, :]
    cos = jnp.cos(ang)[None, :, None, :]
    sin = jnp.sin(ang)[None, :, None, :]
    x1 = x[..., :half].astype(jnp.float32)
    x2 = x[..., half:ROT_DIM].astype(jnp.float32)
    rot = jnp.concatenate([x1 * cos - x2 * sin, x2 * cos + x1 * sin], axis=-1).astype(x.dtype)
    return jnp.concatenate([rot, x[..., ROT_DIM:]], axis=-1)


def masked_softmax(s, mask):
    s = jnp.where(mask, s.astype(jnp.float32), -jnp.inf)
    m = jnp.max(s, axis=-1, keepdims=True)
    m = jnp.where(jnp.isfinite(m), m, 0.0)
    e = jnp.where(mask, jnp.exp(s - m), 0.0)
    den = jnp.sum(e, axis=-1, keepdims=True)
    return e / jnp.where(den > 0.0, den, 1.0)


def s5_discretize(lam_re, lam_im, log_dt, b_re, b_im):
    dt = jnp.exp(log_dt.astype(jnp.float32))[:, None]
    lr = lam_re.astype(jnp.float32)
    li = lam_im.astype(jnp.float32)
    mag = jnp.exp(lr * dt)
    a_re = mag * jnp.cos(li * dt)
    a_im = mag * jnp.sin(li * dt)
    den = lr * lr + li * li
    nr = a_re - 1.0
    f_re = (nr * lr + a_im * li) / den
    f_im = (a_im * lr - nr * li) / den
    br = b_re.astype(jnp.float32)
    bi = b_im.astype(jnp.float32)
    bb_re = f_re[..., None] * br - f_im[..., None] * bi
    bb_im = f_re[..., None] * bi + f_im[..., None] * br
    return a_re, a_im, bb_re, bb_im


def _ssm_combine(e1, e2):
    a1r, a1i, b1r, b1i = e1
    a2r, a2i, b2r, b2i = e2
    return (a2r * a1r - a2i * a1i, a2r * a1i + a2i * a1r,
            a2r * b1r - a2i * b1i + b2r, a2r * b1i + a2i * b1r + b2i)


def s5_scan(u, h_re, h_im, a_re, a_im, bb_re, bb_im, c_re, c_im):
    bt, L = u.shape[:2]
    chunk = math.gcd(L, SCAN_CHUNK)
    uc = u.reshape(bt, L // chunk, chunk, N_GROUPS, SSM_GROUP).transpose(1, 0, 2, 3, 4)
    cr = c_re.astype(jnp.float32)
    ci = c_im.astype(jnp.float32)

    def step(carry, u_blk):
        hr, hi = carry
        br = jnp.einsum('bcgi,gpi->bcgp', u_blk, bb_re)
        bi = jnp.einsum('bcgi,gpi->bcgp', u_blk, bb_im)
        br = br.at[:, 0].add(a_re * hr - a_im * hi)
        bi = bi.at[:, 0].add(a_re * hi + a_im * hr)
        ar = jnp.broadcast_to(a_re, br.shape)
        ai = jnp.broadcast_to(a_im, br.shape)
        _, _, sr, si = lax.associative_scan(_ssm_combine, (ar, ai, br, bi), axis=1)
        y = jnp.einsum('bcgp,gip->bcgi', sr, cr) - jnp.einsum('bcgp,gip->bcgi', si, ci)
        return (sr[:, -1], si[:, -1]), y

    (hr, hi), y = lax.scan(step, (h_re, h_im), uc)
    y = y.transpose(1, 0, 2, 3, 4).reshape(bt, L, N_GROUPS, SSM_GROUP)
    return y, hr, hi


def s5_mixer(xn, h_re, h_im, lam_re, lam_im, log_dt, b_re, b_im, c_re, c_im, d_skip, w_glu, b_glu):
    bt, L, _ = xn.shape
    u = xn.astype(jnp.float32).reshape(bt, L, N_GROUPS, SSM_GROUP)
    a_re, a_im, bb_re, bb_im = s5_discretize(lam_re, lam_im, log_dt, b_re, b_im)
    y, hr, hi = s5_scan(u, h_re.astype(jnp.float32), h_im.astype(jnp.float32),
                        a_re, a_im, bb_re, bb_im, c_re, c_im)
    y = y.reshape(bt, L, D_MODEL) + d_skip.astype(jnp.float32) * xn.astype(jnp.float32)
    y = jax.nn.gelu(y).astype(xn.dtype)
    out = y * jax.nn.sigmoid(y @ w_glu + b_glu)
    return out, hr, hi


def sq_relu_mlp(x, w_up, w_down):
    h = jax.nn.relu(x @ w_up)
    return (h * h) @ w_down


def compress_rows(rows, w1, pe, w2):
    bt, T = rows.shape[:2]
    r = CMP_BLOCK // CMP_STRIDE
    n_ch = T // CMP_STRIDE
    n_cmp = n_ch - r + 1
    ch = rows[:, :n_ch * CMP_STRIDE].reshape(bt, n_ch, CMP_STRIDE, N_KV, HEAD_DIM)
    w1r = w1.reshape(r, CMP_STRIDE, HEAD_DIM, CMP_HIDDEN)
    proj = jnp.einsum('bnsgd,jsdh->bnjgh', ch, w1r)
    pre = jnp.einsum('ld,ldh->h', pe, w1)
    for j in range(r):
        pre = pre + proj[:, j:j + n_cmp, j]
    return jax.nn.gelu(pre) @ w2


def shared_kv_stream(x, pos0, past_cmp, past_slc, past_swa, kv_norm, w_kv,
                     cmp_w1_k, cmp_pe_k, cmp_w2_k, cmp_w1_v, cmp_pe_v, cmp_w2_v):
    bt, L, _ = x.shape
    pos = pos0 + jnp.arange(L)
    kv = (rmsnorm(x, kv_norm) @ w_kv).reshape(bt, L, N_BRANCH, 2, N_KV, HEAD_DIM)
    rows_cmp = kv[:, :, 0]
    rows_slc = jnp.stack([partial_rope(kv[:, :, 1, 0], pos), kv[:, :, 1, 1]], axis=2)
    rows_swa = jnp.stack([partial_rope(kv[:, :, 2, 0], pos), kv[:, :, 2, 1]], axis=2)
    if past_cmp is None:
        full_cmp, full_slc, local_swa = rows_cmp, rows_slc, rows_swa
        swa_start = pos0
    else:
        full_cmp = jnp.concatenate([past_cmp, rows_cmp], axis=1)
        full_slc = jnp.concatenate([past_slc, rows_slc], axis=1)
        local_swa = jnp.concatenate([past_swa, rows_swa], axis=1)
        swa_start = pos0 - past_swa.shape[1]
    k_cmp = compress_rows(full_cmp[:, :, 0], cmp_w1_k, cmp_pe_k, cmp_w2_k)
    v_cmp = compress_rows(full_cmp[:, :, 1], cmp_w1_v, cmp_pe_v, cmp_w2_v)
    return rows_cmp, rows_slc, full_slc, local_swa, swa_start, k_cmp, v_cmp


def nsa_attention(q, gates, k_cmp, v_cmp, full_slc, local_swa, q_off, swa_start):
    bt, L = q.shape[:2]
    T = full_slc.shape[1]
    n_cmp = k_cmp.shape[1]
    n_slc = -(-T // SLC_BLOCK)
    k_sel = min(N_SELECT, n_slc)
    scale = HEAD_DIM ** -0.5
    q_pos = q_off + jnp.arange(L)
    q_rot = partial_rope(q, q_pos)
    qg = q.reshape(bt, L, N_KV, GROUP_Q, HEAD_DIM) * scale
    qrg = q_rot.reshape(bt, L, N_KV, GROUP_Q, HEAD_DIM) * scale
    gg = gates.reshape(bt, L, N_KV, GROUP_Q, N_BRANCH)
    slc = jnp.pad(full_slc, ((0, 0), (0, n_slc * SLC_BLOCK - T), (0, 0), (0, 0), (0, 0)))
    slc = slc.reshape(bt, n_slc, SLC_BLOCK, 2, N_KV, HEAD_DIM).transpose(0, 4, 1, 2, 3, 5)
    ci = np.arange(n_cmp)[:, None]
    sj = np.arange(n_slc)[None, :]
    overlap = jnp.asarray(((ci * CMP_STRIDE < (sj + 1) * SLC_BLOCK)
                           & (ci * CMP_STRIDE + CMP_BLOCK > sj * SLC_BLOCK)).astype(np.float32))
    cmp_end = jnp.arange(n_cmp) * CMP_STRIDE + CMP_BLOCK - 1
    blk_idx = jnp.arange(n_slc)
    tw = local_swa.shape[1]
    swa = jnp.pad(local_swa, ((0, 0), (WINDOW, 0), (0, 0), (0, 0), (0, 0)))
    swa_pos = swa_start - WINDOW + jnp.arange(tw + WINDOW)
    b_ix = jnp.arange(bt)[:, None, None, None]
    g_ix = jnp.arange(N_KV)[None, None, :, None]
    qb = math.gcd(L, Q_BLOCK)

    def block(i):
        s = i * qb
        qc = lax.dynamic_slice_in_dim(qg, s, qb, axis=1)
        qr = lax.dynamic_slice_in_dim(qrg, s, qb, axis=1)
        gt = lax.dynamic_slice_in_dim(gg, s, qb, axis=1).astype(jnp.float32)
        t = q_off + s + jnp.arange(qb)
        sc = jnp.einsum('bqgrd,bngd->bqgrn', qc, k_cmp)
        m_c = (cmp_end[None, :] <= t[:, None])[None, :, None, None, :]
        p_c = masked_softmax(sc, m_c)
        o_c = jnp.einsum('bqgrn,bngd->bqgrd', p_c, v_cmp.astype(jnp.float32))
        imp = jnp.einsum('bqgrn,ns->bqgs', p_c, overlap)
        cur = t // SLC_BLOCK
        forced = (blk_idx[None, :] == 0) | (blk_idx[None, :] == cur[:, None]) | (blk_idx[None, :] == cur[:, None] - 1)
        eligible = blk_idx[None, :] * SLC_BLOCK <= t[:, None]
        imp = jnp.where(forced[None, :, None, :], FORCE_SCORE, imp)
        imp = jnp.where(eligible[None, :, None, :], imp, -jnp.inf)
        top_s, top_i = lax.top_k(imp, k_sel)
        sel = slc[b_ix, g_ix, top_i].reshape(bt, qb, N_KV, k_sel * SLC_BLOCK, 2, HEAD_DIM)
        key_pos = (top_i[..., None] * SLC_BLOCK + jnp.arange(SLC_BLOCK)).reshape(bt, qb, N_KV, k_sel * SLC_BLOCK)
        valid = jnp.repeat(jnp.isfinite(top_s), SLC_BLOCK, axis=-1)
        m_s = valid & (key_pos <= t[None, :, None, None])
        ss = jnp.einsum('bqgrd,bqgkd->bqgrk', qr, sel[..., 0, :])
        p_s = masked_softmax(ss, m_s[:, :, :, None, :])
        o_s = jnp.einsum('bqgrk,bqgkd->bqgrd', p_s, sel[..., 1, :].astype(jnp.float32))
        w0 = q_off + s - swa_start
        kw = lax.dynamic_slice_in_dim(swa, w0, WINDOW + qb, axis=1)
        pw = lax.dynamic_slice_in_dim(swa_pos, w0, WINDOW + qb)
        dist = t[:, None] - pw[None, :]
        m_w = ((dist >= 0) & (dist < WINDOW) & (pw[None, :] >= swa_start))[None, :, None, None, :]
        sw = jnp.einsum('bqgrd,bkgd->bqgrk', qr, kw[:, :, 0])
        p_w = masked_softmax(sw, m_w)
        o_w = jnp.einsum('bqgrk,bkgd->bqgrd', p_w, kw[:, :, 1].astype(jnp.float32))
        return o_c * gt[..., 0:1] + o_s * gt[..., 1:2] + o_w * gt[..., 2:3]

    out = lax.map(block, jnp.arange(L // qb))
    return out.transpose(1, 0, 2, 3, 4, 5).reshape(bt, L, Q_WIDTH)


def nsa_layer(x, g_pre, w_qg, w_o, g_post, k_cmp, v_cmp, full_slc, local_swa, q_off, swa_start):
    bt, L, _ = x.shape
    qg = rmsnorm(x, g_pre) @ w_qg
    q = qg[..., :Q_WIDTH].reshape(bt, L, N_HEADS, HEAD_DIM)
    gates = jax.nn.sigmoid(qg[..., Q_WIDTH:].astype(jnp.float32)).reshape(bt, L, N_HEADS, N_BRANCH)
    o = nsa_attention(q, gates, k_cmp, v_cmp, full_slc, local_swa, q_off, swa_start)
    return rmsnorm(o.astype(x.dtype) @ w_o, g_post)


def run_trunk(x, h_re0, h_im0, past_cmp, past_slc, past_swa, pos0, p):
    bt = x.shape[0]
    ssm_re, ssm_im = [], []
    for layer in range(DEPTH):
        if layer < N_A_LAYERS:
            if h_re0 is None:
                h_re = jnp.zeros((bt, N_GROUPS, SSM_STATE), jnp.float32)
                h_im = jnp.zeros((bt, N_GROUPS, SSM_STATE), jnp.float32)
            else:
                h_re, h_im = h_re0[layer], h_im0[layer]
            m, hr, hi = s5_mixer(rmsnorm(x, p['a_norm_pre'][layer]), h_re, h_im,
                                 p['a_lam_re'][layer], p['a_lam_im'][layer], p['a_log_dt'][layer],
                                 p['a_b_re'][layer], p['a_b_im'][layer], p['a_c_re'][layer], p['a_c_im'][layer],
                                 p['a_d'][layer], p['a_w_glu'][layer], p['a_b_glu'][layer])
            x = x + rmsnorm(m, p['a_norm_post'][layer])
            ssm_re.append(hr)
            ssm_im.append(hi)
        else:
            if layer == N_A_LAYERS:
                rows_cmp, rows_slc, full_slc, local_swa, swa_start, k_cmp, v_cmp = shared_kv_stream(
                    x, pos0, past_cmp, past_slc, past_swa, p['kv_norm'], p['w_kv'],
                    p['cmp_w1_k'], p['cmp_pe_k'], p['cmp_w2_k'], p['cmp_w1_v'], p['cmp_pe_v'], p['cmp_w2_v'])
            j = layer - N_A_LAYERS
            x = x + nsa_layer(x, p['b_norm_pre'][j], p['b_w_qg'][j], p['b_w_o'][j], p['b_norm_post'][j],
                              k_cmp, v_cmp, full_slc, local_swa, pos0, swa_start)
        h = sq_relu_mlp(rmsnorm(x, p['mlp_norm_pre'][layer]), p['mlp_w_up'][layer], p['mlp_w_down'][layer])
        x = x + rmsnorm(h, p['mlp_norm_post'][layer])
    w_keep = min(WINDOW, PAST_LEN)
    swa_buf = jnp.pad(local_swa, ((0, 0), (w_keep, 0), (0, 0), (0, 0), (0, 0)))[:, -w_keep:]
    return x, rows_cmp, rows_slc, swa_buf, jnp.stack(ssm_re), jnp.stack(ssm_im)


def setup_inputs(seed: int = 0) -> dict:
    key = jax.random.key(seed)
    ks = iter(jax.random.split(key, 48))

    def nrm(shape, scale):
        return jax.random.normal(next(ks), shape, jnp.float32) * scale

    n_pages = PAST_LEN // PAGE_SIZE
    n_used = DEC_BATCH * n_pages
    n_pool = n_used + n_used // 4
    w_keep = min(WINDOW, PAST_LEN)
    page_table = jax.random.permutation(next(ks), n_pool)[:n_used].reshape(DEC_BATCH, n_pages).astype(jnp.int32)
    lam_n = jnp.pi * jnp.arange(SSM_STATE, dtype=jnp.float32)
    return {
        'x_prompt': nrm((BATCH, SEQ, D_MODEL), 1.0),
        'x_sample': nrm((DEC_BATCH, DEC_SEQ, D_MODEL), 1.0),
        'cache_kv_cmp': nrm((n_pool, PAGE_SIZE, 2, N_KV, HEAD_DIM), 1.0),
        'cache_kv_slc': nrm((n_pool, PAGE_SIZE, 2, N_KV, HEAD_DIM), 1.0),
        'state_kv_swa': nrm((DEC_BATCH, w_keep, 2, N_KV, HEAD_DIM), 1.0),
        'state_ssm_re': nrm((N_A_LAYERS, DEC_BATCH, N_GROUPS, SSM_STATE), 0.1),
        'state_ssm_im': nrm((N_A_LAYERS, DEC_BATCH, N_GROUPS, SSM_STATE), 0.1),
        'page_table': page_table,
        'a_norm_pre': 1.0 + nrm((N_A_LAYERS, D_MODEL), 0.01),
        'a_lam_re': -0.5 + nrm((N_A_LAYERS, N_GROUPS, SSM_STATE), 0.01),
        'a_lam_im': lam_n + nrm((N_A_LAYERS, N_GROUPS, SSM_STATE), 0.01),
        'a_log_dt': jax.random.uniform(next(ks), (N_A_LAYERS, N_GROUPS), jnp.float32, math.log(1e-3), math.log(1e-1)),
        'a_b_re': nrm((N_A_LAYERS, N_GROUPS, SSM_STATE, SSM_GROUP), SSM_GROUP ** -0.5),
        'a_b_im': nrm((N_A_LAYERS, N_GROUPS, SSM_STATE, SSM_GROUP), SSM_GROUP ** -0.5),
        'a_c_re': nrm((N_A_LAYERS, N_GROUPS, SSM_GROUP, SSM_STATE), SSM_STATE ** -0.5),
        'a_c_im': nrm((N_A_LAYERS, N_GROUPS, SSM_GROUP, SSM_STATE), SSM_STATE ** -0.5),
        'a_d': nrm((N_A_LAYERS, D_MODEL), 1.0),
        'a_w_glu': nrm((N_A_LAYERS, D_MODEL, D_MODEL), D_MODEL ** -0.5),
        'a_b_glu': nrm((N_A_LAYERS, D_MODEL), 0.01),
        'a_norm_post': 1.0 + nrm((N_A_LAYERS, D_MODEL), 0.01),
        'kv_norm': 1.0 + nrm((D_MODEL,), 0.01),
        'w_kv': nrm((D_MODEL, KV_WIDTH), D_MODEL ** -0.5),
        'cmp_w1_k': nrm((CMP_BLOCK, HEAD_DIM, CMP_HIDDEN), (CMP_BLOCK * HEAD_DIM) ** -0.5),
        'cmp_pe_k': nrm((CMP_BLOCK, HEAD_DIM), 0.5),
        'cmp_w2_k': nrm((CMP_HIDDEN, HEAD_DIM), CMP_HIDDEN ** -0.5),
        'cmp_w1_v': nrm((CMP_BLOCK, HEAD_DIM, CMP_HIDDEN), (CMP_BLOCK * HEAD_DIM) ** -0.5),
        'cmp_pe_v': nrm((CMP_BLOCK, HEAD_DIM), 0.5),
        'cmp_w2_v': nrm((CMP_HIDDEN, HEAD_DIM), CMP_HIDDEN ** -0.5),
        'b_norm_pre': 1.0 + nrm((N_B_LAYERS, D_MODEL), 0.01),
        'b_w_qg': nrm((N_B_LAYERS, D_MODEL, Q_WIDTH + N_BRANCH * N_HEADS), D_MODEL ** -0.5),
        'b_w_o': nrm((N_B_LAYERS, Q_WIDTH, D_MODEL), Q_WIDTH ** -0.5),
        'b_norm_post': 1.0 + nrm((N_B_LAYERS, D_MODEL), 0.01),
        'mlp_norm_pre': 1.0 + nrm((DEPTH, D_MODEL), 0.01),
        'mlp_w_up': nrm((DEPTH, D_MODEL, D_FF), D_MODEL ** -0.5),
        'mlp_w_down': nrm((DEPTH, D_FF, D_MODEL), D_FF ** -0.5),
        'mlp_norm_post': 1.0 + nrm((DEPTH, D_MODEL), 0.01),
    }


def reference(x_prompt, x_sample, cache_kv_cmp, cache_kv_slc, state_kv_swa, state_ssm_re, state_ssm_im,
              page_table, a_norm_pre, a_lam_re, a_lam_im, a_log_dt, a_b_re, a_b_im, a_c_re, a_c_im, a_d,
              a_w_glu, a_b_glu, a_norm_post, kv_norm, w_kv, cmp_w1_k, cmp_pe_k, cmp_w2_k, cmp_w1_v,
              cmp_pe_v, cmp_w2_v, b_norm_pre, b_w_qg, b_w_o, b_norm_post, mlp_norm_pre, mlp_w_up,
              mlp_w_down, mlp_norm_post):
    params = {
        'a_norm_pre': a_norm_pre, 'a_lam_re': a_lam_re, 'a_lam_im': a_lam_im, 'a_log_dt': a_log_dt,
        'a_b_re': a_b_re, 'a_b_im': a_b_im, 'a_c_re': a_c_re, 'a_c_im': a_c_im, 'a_d': a_d,
        'a_w_glu': a_w_glu, 'a_b_glu': a_b_glu, 'a_norm_post': a_norm_post,
        'kv_norm': kv_norm, 'w_kv': w_kv,
        'cmp_w1_k': cmp_w1_k, 'cmp_pe_k': cmp_pe_k, 'cmp_w2_k': cmp_w2_k,
        'cmp_w1_v': cmp_w1_v, 'cmp_pe_v': cmp_pe_v, 'cmp_w2_v': cmp_w2_v,
        'b_norm_pre': b_norm_pre, 'b_w_qg': b_w_qg, 'b_w_o': b_w_o, 'b_norm_post': b_norm_post,
        'mlp_norm_pre': mlp_norm_pre, 'mlp_w_up': mlp_w_up, 'mlp_w_down': mlp_w_down,
        'mlp_norm_post': mlp_norm_post,
    }
    y_prompt, cmp_p, slc_p, swa_p, re_p, im_p = run_trunk(x_prompt, None, None, None, None, None, 0, params)
    n_pages = page_table.shape[1]
    past_cmp = cache_kv_cmp[page_table].reshape(page_table.shape[0], n_pages * PAGE_SIZE, 2, N_KV, HEAD_DIM)
    past_slc = cache_kv_slc[page_table].reshape(page_table.shape[0], n_pages * PAGE_SIZE, 2, N_KV, HEAD_DIM)
    y_sample, cmp_s, slc_s, swa_s, re_s, im_s = run_trunk(
        x_sample, state_ssm_re, state_ssm_im, past_cmp, past_slc, state_kv_swa, PAST_LEN, params)
    return (y_prompt, y_sample, cmp_p, cmp_s, slc_p, slc_s, swa_p, swa_s, re_p, im_p, re_s, im_s)
```

```python
import functools
import math

import jax
import jax.numpy as jnp
import numpy as np
from jax import lax
from jax.experimental import pallas as pl
from jax.experimental.pallas import tpu as pltpu

F32 = jnp.float32
BF16 = jnp.bfloat16

D_MODEL = 2048
N_HEADS = 16
HEAD_DIM = 128
N_KV = 4
GROUP_Q = N_HEADS // N_KV
N_BRANCH = 3
ROT_DIM = HEAD_DIM // 4
ROPE_THETA = 500000.0
SSM_GROUP = 16
N_GROUPS = D_MODEL // SSM_GROUP
SSM_STATE = 64
S5_CHUNK = 16
S5_CW = S5_CHUNK * SSM_GROUP
D_FF = 4 * D_MODEL
CMP_BLOCK = 32
CMP_STRIDE = 16
CMP_HIDDEN = 2 * HEAD_DIM
SLC_BLOCK = 64
N_SELECT = 16
WINDOW = 512
PAGE = 128
FORCE_SCORE = 1.0e4
EPS = 1e-6
Q_WIDTH = N_HEADS * HEAD_DIM
GROUP_W = GROUP_Q * HEAD_DIM
KV_SLAB = 2 * N_KV * HEAD_DIM
SLC_TILE = 512
SWA_SPAN = WINDOW + PAGE
NEG = -1.0e30
VMEM_LIMIT = 56 * 1024 * 1024


def _cparams(sem):
    return pltpu.CompilerParams(dimension_semantics=sem, vmem_limit_bytes=VMEM_LIMIT)


def _rms(x, g):
    var = jnp.mean(x * x, axis=-1, keepdims=True)
    return x * lax.rsqrt(var + EPS) * g


def _gelu(x):
    return 0.5 * x * (1.0 + jnp.tanh(math.sqrt(2.0 / math.pi) * (x + 0.044715 * (x * x * x))))


def _sigmoid(x):
    return 1.0 / (1.0 + jnp.exp(-x))


def _dot(a, b):
    return jnp.dot(a, b, preferred_element_type=F32)


def _dot_nt(a, b):
    return lax.dot_general(a, b, (((1,), (1,)), ((), ())), preferred_element_type=F32)


def _rope128(x, c, s1, s2):
    return x * c + pltpu.roll(x, HEAD_DIM - ROT_DIM // 2, 1) * s1 + pltpu.roll(x, ROT_DIM // 2, 1) * s2


def _rope_tables(pos):
    half = ROT_DIM // 2
    inv = ROPE_THETA ** (-jnp.arange(half, dtype=F32) / half)
    ang = pos.astype(F32)[:, None] * inv[None, :]
    cos, sin = jnp.cos(ang), jnp.sin(ang)
    n = pos.shape[0]
    rest = HEAD_DIM - ROT_DIM
    c = jnp.concatenate([cos, cos, jnp.ones((n, rest), F32)], axis=1)
    s1 = jnp.concatenate([-sin, jnp.zeros((n, HEAD_DIM - half), F32)], axis=1)
    s2 = jnp.concatenate([jnp.zeros((n, half), F32), sin, jnp.zeros((n, rest), F32)], axis=1)
    return c, s1, s2


def _row_tile(m):
    return 512 if m % 512 == 0 else m


def _kv_proj_kernel(x_ref, g_ref, w_ref, c_ref, s1_ref, s2_ref, o_ref, xn_ref):
    j = pl.program_id(1)

    @pl.when(j == 0)
    def _():
        xn_ref[...] = _rms(x_ref[...], g_ref[...]).astype(BF16)

    acc = _dot(xn_ref[...], w_ref[...])
    is_rope = jnp.logical_or(j == 2, j == 4)

    @pl.when(is_rope)
    def _():
        c, s1, s2 = c_ref[...], s1_ref[...], s2_ref[...]
        for h in range(N_KV):
            sl = slice(h * HEAD_DIM, (h + 1) * HEAD_DIM)
            o_ref[:, sl] = _rope128(acc[:, sl], c, s1, s2)

    @pl.when(jnp.logical_not(is_rope))
    def _():
        o_ref[...] = acc


def kv_proj(x, g, w_bf, tables):
    m = x.shape[0]
    tm = _row_tile(m)
    n = w_bf.shape[1]
    tn = N_KV * HEAD_DIM
    c, s1, s2 = tables
    tab = pl.BlockSpec((tm, HEAD_DIM), lambda i, j: (i, 0))
    return pl.pallas_call(
        _kv_proj_kernel,
        grid=(m // tm, n // tn),
        in_specs=[pl.BlockSpec((tm, D_MODEL), lambda i, j: (i, 0)),
                  pl.BlockSpec((1, D_MODEL), lambda i, j: (0, 0)),
                  pl.BlockSpec((D_MODEL, tn), lambda i, j: (0, j)),
                  tab, tab, tab],
        out_specs=pl.BlockSpec((tm, tn), lambda i, j: (i, j)),
        out_shape=jax.ShapeDtypeStruct((m, n), F32),
        scratch_shapes=[pltpu.VMEM((tm, D_MODEL), BF16)],
        compiler_params=_cparams(("parallel", "arbitrary")),
        name="kv_proj",
    )(x, g.reshape(1, -1), w_bf, c, s1, s2)


def _q_proj_kernel(x_ref, g_ref, w_ref, c_ref, s1_ref, s2_ref, q_ref, qr_ref, xn_ref):
    j = pl.program_id(1)

    @pl.when(j == 0)
    def _():
        xn_ref[...] = _rms(x_ref[...], g_ref[...]).astype(BF16)

    acc = _dot(xn_ref[...], w_ref[...])
    scale = HEAD_DIM ** -0.5
    c, s1, s2 = c_ref[...], s1_ref[...], s2_ref[...]
    q_ref[...] = (acc * scale).astype(BF16)
    for h in range(GROUP_Q):
        sl = slice(h * HEAD_DIM, (h + 1) * HEAD_DIM)
        qr_ref[:, sl] = (_rope128(acc[:, sl], c, s1, s2) * scale).astype(BF16)


def q_proj(x, g, wq_bf, tables):
    m = x.shape[0]
    tm = _row_tile(m)
    tn = GROUP_W
    c, s1, s2 = tables
    tab = pl.BlockSpec((tm, HEAD_DIM), lambda i, j: (i, 0))
    out = jax.ShapeDtypeStruct((m, Q_WIDTH), BF16)
    ospec = pl.BlockSpec((tm, tn), lambda i, j: (i, j))
    return pl.pallas_call(
        _q_proj_kernel,
        grid=(m // tm, Q_WIDTH // tn),
        in_specs=[pl.BlockSpec((tm, D_MODEL), lambda i, j: (i, 0)),
                  pl.BlockSpec((1, D_MODEL), lambda i, j: (0, 0)),
                  pl.BlockSpec((D_MODEL, tn), lambda i, j: (0, j)),
                  tab, tab, tab],
        out_specs=[ospec, ospec],
        out_shape=[out, out],
        scratch_shapes=[pltpu.VMEM((tm, D_MODEL), BF16)],
        compiler_params=_cparams(("parallel", "arbitrary")),
        name="q_proj",
    )(x, g.reshape(1, -1), wq_bf, c, s1, s2)


def _gate_proj_kernel(x_ref, g_ref, w_ref, o_ref):
    xn = _rms(x_ref[...], g_ref[...]).astype(BF16)
    o_ref[...] = _sigmoid(_dot(xn, w_ref[...]))


def gate_proj(x, g, wg_bf):
    m = x.shape[0]
    tm = _row_tile(m)
    n = wg_bf.shape[1]
    return pl.pallas_call(
        _gate_proj_kernel,
        grid=(m // tm,),
        in_specs=[pl.BlockSpec((tm, D_MODEL), lambda i: (i, 0)),
                  pl.BlockSpec((1, D_MODEL), lambda i: (0, 0)),
                  pl.BlockSpec((D_MODEL, n), lambda i: (0, 0))],
        out_specs=pl.BlockSpec((tm, n), lambda i: (i, 0)),
        out_shape=jax.ShapeDtypeStruct((m, n), F32),
        compiler_params=_cparams(("parallel",)),
        name="gate_proj",
    )(x, g.reshape(1, -1), wg_bf)


def _rms_cast_kernel(x_ref, g_ref, o_ref):
    o_ref[...] = _rms(x_ref[...], g_ref[...]).astype(BF16)


def rms_cast(x, g):
    m = x.shape[0]
    tm = _row_tile(m)
    return pl.pallas_call(
        _rms_cast_kernel,
        grid=(m // tm,),
        in_specs=[pl.BlockSpec((tm, D_MODEL), lambda i: (i, 0)),
                  pl.BlockSpec((1, D_MODEL), lambda i: (0, 0))],
        out_specs=pl.BlockSpec((tm, D_MODEL), lambda i: (i, 0)),
        out_shape=jax.ShapeDtypeStruct((m, D_MODEL), BF16),
        compiler_params=_cparams(("parallel",)),
        name="rms_cast",
    )(x, g.reshape(1, -1))


def _mlp_kernel(x_ref, gpre_ref, wup_ref, wdn_ref, gpost_ref, o_ref, xn_ref, acc_ref):
    j = pl.program_id(1)

    @pl.when(j == 0)
    def _():
        xn_ref[...] = _rms(x_ref[...], gpre_ref[...]).astype(BF16)
        acc_ref[...] = jnp.zeros_like(acc_ref)

    h = jnp.maximum(_dot(xn_ref[...], wup_ref[...]), 0.0)
    acc_ref[...] += _dot((h * h).astype(BF16), wdn_ref[...])

    @pl.when(j == pl.num_programs(1) - 1)
    def _():
        o_ref[...] = x_ref[...] + _rms(acc_ref[...], gpost_ref[...])


def mlp(x, gpre, wup_bf, wdn_bf, gpost):
    m = x.shape[0]
    tm = _row_tile(m)
    tf = 512
    return pl.pallas_call(
        _mlp_kernel,
        grid=(m // tm, D_FF // tf),
        in_specs=[pl.BlockSpec((tm, D_MODEL), lambda i, j: (i, 0)),
                  pl.BlockSpec((1, D_MODEL), lambda i, j: (0, 0)),
                  pl.BlockSpec((D_MODEL, tf), lambda i, j: (0, j)),
                  pl.BlockSpec((tf, D_MODEL), lambda i, j: (j, 0)),
                  pl.BlockSpec((1, D_MODEL), lambda i, j: (0, 0))],
        out_specs=pl.BlockSpec((tm, D_MODEL), lambda i, j: (i, 0)),
        out_shape=jax.ShapeDtypeStruct((m, D_MODEL), F32),
        scratch_shapes=[pltpu.VMEM((tm, D_MODEL), BF16), pltpu.VMEM((tm, D_MODEL), F32)],
        compiler_params=_cparams(("parallel", "arbitrary")),
        name="mlp",
    )(x, gpre.reshape(1, -1), wup_bf, wdn_bf, gpost.reshape(1, -1))


def _oproj_kernel(o_ref, w_ref, g_ref, res_ref, out_ref):
    out_ref[...] = res_ref[...] + _rms(_dot(o_ref[...], w_ref[...]), g_ref[...])


def oproj(o_bf, w_bf, g, res):
    m = o_bf.shape[0]
    tm = _row_tile(m)
    return pl.pallas_call(
        _oproj_kernel,
        grid=(m // tm,),
        in_specs=[pl.BlockSpec((tm, Q_WIDTH), lambda i: (i, 0)),
                  pl.BlockSpec((Q_WIDTH, D_MODEL), lambda i: (0, 0)),
                  pl.BlockSpec((1, D_MODEL), lambda i: (0, 0)),
                  pl.BlockSpec((tm, D_MODEL), lambda i: (i, 0))],
        out_specs=pl.BlockSpec((tm, D_MODEL), lambda i: (i, 0)),
        out_shape=jax.ShapeDtypeStruct((m, D_MODEL), F32),
        compiler_params=_cparams(("parallel",)),
        name="oproj",
    )(o_bf, w_bf, g.reshape(1, -1), res)


def _glu_kernel(x_ref, y_ref, gpre_ref, d_ref, w_ref, b_ref, gpost_ref, o_ref):
    x = x_ref[...]
    xn = _rms(x, gpre_ref[...])
    y = _gelu(y_ref[...] + d_ref[...] * xn)
    z = _dot(y.astype(BF16), w_ref[...]) + b_ref[...]
    o_ref[...] = x + _rms(y * _sigmoid(z), gpost_ref[...])


def glu_tail(x, y_ssm, gpre, d_skip, w_bf, b, gpost):
    m = x.shape[0]
    tm = _row_tile(m)
    vec = pl.BlockSpec((1, D_MODEL), lambda i: (0, 0))
    row = pl.BlockSpec((tm, D_MODEL), lambda i: (i, 0))
    return pl.pallas_call(
        _glu_kernel,
        grid=(m // tm,),
        in_specs=[row, row, vec, vec, pl.BlockSpec((D_MODEL, D_MODEL), lambda i: (0, 0)), vec, vec],
        out_specs=row,
        out_shape=jax.ShapeDtypeStruct((m, D_MODEL), F32),
        compiler_params=_cparams(("parallel",)),
        name="glu_tail",
    )(x, y_ssm, gpre.reshape(1, -1), d_skip.reshape(1, -1), w_bf, b.reshape(1, -1), gpost.reshape(1, -1))


S5_GB = 8


def _s5_prep_kernel(lr_ref, li_ref, ldt_ref, br_ref, bi_ref, cr_ref, ci_ref, ctr_ref, cti_ref,
                    kt_ref, xr_ref, xi_ref, wint_ref, pr_ref, pi_ref):
    lr = lr_ref[...]
    li = li_ref[...]
    dt = jnp.exp(ldt_ref[...])
    ldr = lr * dt
    ldi = li * dt
    mag = jnp.exp(ldr)
    a_re = mag * jnp.cos(ldi)
    a_im = mag * jnp.sin(ldi)
    den = lr * lr + li * li
    nr = a_re - 1.0
    f_re = (nr * lr + a_im * li) / den
    f_im = (a_im * lr - nr * li) / den
    br = br_ref[...]
    bi = bi_ref[...]
    bb_re = f_re * br - f_im * bi
    bb_im = f_re * bi + f_im * br
    k = (lax.broadcasted_iota(jnp.int32, (1, 1, S5_CW), 2) // SSM_GROUP).astype(F32)
    mk = jnp.exp(ldr * k)
    ak_re = mk * jnp.cos(ldi * k)
    ak_im = mk * jnp.sin(ldi * k)
    x_re = ak_re * bb_re - ak_im * bb_im
    x_im = ak_re * bb_im + ak_im * bb_re
    xr_ref[...] = x_re
    xi_ref[...] = x_im
    hp = lax.Precision.HIGHEST
    kt_ref[...] = (jnp.einsum('gip,gpn->gin', cr_ref[...], x_re, precision=hp, preferred_element_type=F32)
                   - jnp.einsum('gip,gpn->gin', ci_ref[...], x_im, precision=hp, preferred_element_type=F32))
    p_re = ak_re * a_re - ak_im * a_im
    p_im = ak_re * a_im + ak_im * a_re
    pr_ref[...] = p_re
    pi_ref[...] = p_im
    ctr = ctr_ref[...]
    cti = cti_ref[...]
    wint_ref[:, 0:SSM_STATE, :] = ctr * p_re - cti * p_im
    wint_ref[:, SSM_STATE:2 * SSM_STATE, :] = -(ctr * p_im + cti * p_re)


def s5_prep(lam_re, lam_im, log_dt, b_re, b_im, c_re, c_im):
    g, p = lam_re.shape
    gb = S5_GB
    col = pl.BlockSpec((gb, p, 1), lambda i: (i, 0, 0))
    wide = pl.BlockSpec((gb, p, S5_CW), lambda i: (i, 0, 0))
    cmat = pl.BlockSpec((gb, SSM_GROUP, p), lambda i: (i, 0, 0))
    wide_shape = jax.ShapeDtypeStruct((g, p, S5_CW), F32)
    tile16 = lambda a: jnp.tile(a, (1, 1, S5_CHUNK))
    return pl.pallas_call(
        _s5_prep_kernel,
        grid=(g // gb,),
        in_specs=[col, col, pl.BlockSpec((gb, 1, 1), lambda i: (i, 0, 0)), wide, wide, cmat, cmat, wide, wide],
        out_specs=[pl.BlockSpec((gb, SSM_GROUP, S5_CW), lambda i: (i, 0, 0)), wide, wide,
                   pl.BlockSpec((gb, 2 * p, S5_CW), lambda i: (i, 0, 0)), wide, wide],
        out_shape=[jax.ShapeDtypeStruct((g, SSM_GROUP, S5_CW), F32), wide_shape, wide_shape,
                   jax.ShapeDtypeStruct((g, 2 * p, S5_CW), F32), wide_shape, wide_shape],
        compiler_params=_cparams(("parallel",)),
        name="s5_prep",
    )(lam_re.reshape(g, p, 1), lam_im.reshape(g, p, 1), log_dt.reshape(g, 1, 1),
      tile16(b_re), tile16(b_im), c_re, c_im,
      tile16(c_re.transpose(0, 2, 1)), tile16(c_im.transpose(0, 2, 1)))


def _s5_assemble(kt, x_re, x_im):
    g = kt.shape[0]
    t = S5_CHUNK
    km = kt.reshape(g, SSM_GROUP, t, SSM_GROUP)
    lag = np.arange(t)[None, :] - np.arange(t)[:, None]
    w = km[:, :, np.clip(lag, 0, t - 1), :]
    w = jnp.where((lag >= 0)[None, None, :, :, None], w, 0.0)
    w_intra = w.transpose(0, 2, 4, 3, 1).reshape(g, S5_CW, S5_CW)
    rev = lambda x: x.reshape(g, SSM_STATE, t, SSM_GROUP)[:, :, ::-1, :].reshape(g, SSM_STATE, S5_CW)
    return w_intra.astype(BF16), rev(x_re).astype(BF16), rev(x_im).astype(BF16)


def _s5_apply_kernel(u_ref, wi_ref, wsr_ref, wsi_ref, wint_ref, ar_ref, ai_ref, h0r_ref, h0i_ref,
                     y_ref, hfr_ref, hfi_ref, sr_ref, si_ref, hpr_ref, hpi_ref, *, nb, nc):
    u = u_ref[0]
    sr_ref[...] = _dot_nt(u, wsr_ref[0])
    si_ref[...] = _dot_nt(u, wsi_ref[0])
    ar = ar_ref[0]
    ai = ai_ref[0]
    h0 = tuple((h0r_ref[0, b:b + 1, :], h0i_ref[0, b:b + 1, :]) for b in range(nb))

    def step(c, hs):
        out = []
        for b in range(nb):
            hr, hi = hs[b]
            row = b * nc + c
            hpr_ref[pl.ds(row, 1), :] = hr
            hpi_ref[pl.ds(row, 1), :] = hi
            nr = ar * hr - ai * hi + sr_ref[pl.ds(row, 1), :]
            ni = ar * hi + ai * hr + si_ref[pl.ds(row, 1), :]
            out.append((nr, ni))
        return tuple(out)

    hs = lax.fori_loop(0, nc, step, h0)
    for b in range(nb):
        hfr_ref[0, b:b + 1, :] = hs[b][0]
        hfi_ref[0, b:b + 1, :] = hs[b][1]
    wint = wint_ref[0]
    y_ref[0] = (_dot(u, wi_ref[0])
                + _dot(hpr_ref[...].astype(BF16), wint[0:SSM_STATE])
                + _dot(hpi_ref[...].astype(BF16), wint[SSM_STATE:2 * SSM_STATE]))


def s5_apply(u, w_intra, ws_re, ws_im, w_int, a_re, a_im, h0_re, h0_im, nb, nc):
    g = u.shape[0]
    mc = nb * nc
    p = SSM_STATE
    blk = lambda s: pl.BlockSpec((1,) + s, lambda i: (i, 0, 0))
    return pl.pallas_call(
        functools.partial(_s5_apply_kernel, nb=nb, nc=nc),
        grid=(g,),
        in_specs=[blk((mc, S5_CW)), blk((S5_CW, S5_CW)), blk((p, S5_CW)), blk((p, S5_CW)), blk((2 * p, S5_CW)),
                  blk((1, p)), blk((1, p)), blk((nb, p)), blk((nb, p))],
        out_specs=[blk((mc, S5_CW)), blk((nb, p)), blk((nb, p))],
        out_shape=[jax.ShapeDtypeStruct((g, mc, S5_CW), F32),
                   jax.ShapeDtypeStruct((g, nb, p), F32), jax.ShapeDtypeStruct((g, nb, p), F32)],
        scratch_shapes=[pltpu.VMEM((mc, p), F32)] * 4,
        compiler_params=_cparams(("parallel",)),
        name="s5_apply",
    )(u, w_intra, ws_re, ws_im, w_int, a_re, a_im, h0_re, h0_im)


CMP_PAGES = 16
CMP_ROWS = CMP_PAGES * (PAGE // CMP_STRIDE)
CMP_FLAT = CMP_STRIDE * HEAD_DIM


def _cmp_proj_kernel(pt_ref, x_ref, wk_ref, wv_ref, o_ref, lhs_ref, *, row_w, col0):
    pg = pl.program_id(1) % CMP_PAGES
    per_page = PAGE // CMP_STRIDE
    for kv in range(2):
        for g in range(N_KV):
            c = col0 + kv * N_KV + g
            for s in range(CMP_STRIDE):
                piece = x_ref[0, pl.ds(s * row_w + c, per_page, stride=CMP_STRIDE * row_w), :]
                lhs_ref[kv, g, pl.ds(pg * per_page, per_page), pl.ds(s * HEAD_DIM, HEAD_DIM)] = piece

    @pl.when(pg == CMP_PAGES - 1)
    def _():
        for kv, w_ref in ((0, wk_ref), (1, wv_ref)):
            for g in range(N_KV):
                o_ref[0, kv, g] = _dot(lhs_ref[kv, g].astype(BF16), w_ref[...])


def cmp_proj(pages3, col_block, page_table, wk_cat, wv_cat):
    nb, npg = page_table.shape
    assert npg % CMP_PAGES == 0, "compression consumes whole groups of pages"
    n_ch = npg * (PAGE // CMP_STRIDE)
    row_w = pages3.shape[2] // HEAD_DIM
    pages3 = pages3.reshape(pages3.shape[0], PAGE * row_w, HEAD_DIM)
    grid_spec = pltpu.PrefetchScalarGridSpec(
        num_scalar_prefetch=1,
        grid=(nb, npg),
        in_specs=[pl.BlockSpec((1, PAGE * row_w, HEAD_DIM), lambda b, p, pt: (pt[b, p], 0, 0)),
                  pl.BlockSpec((CMP_FLAT, 2 * CMP_HIDDEN), lambda b, p, pt: (0, 0)),
                  pl.BlockSpec((CMP_FLAT, 2 * CMP_HIDDEN), lambda b, p, pt: (0, 0))],
        out_specs=pl.BlockSpec((1, 2, N_KV, CMP_ROWS, 2 * CMP_HIDDEN), lambda b, p, pt: (b, 0, 0, p // CMP_PAGES, 0)),
        scratch_shapes=[pltpu.VMEM((2, N_KV, CMP_ROWS, CMP_FLAT), F32)],
    )
    return pl.pallas_call(
        functools.partial(_cmp_proj_kernel, row_w=row_w, col0=col_block * (KV_SLAB // HEAD_DIM)),
        grid_spec=grid_spec,
        out_shape=jax.ShapeDtypeStruct((nb, 2, N_KV, n_ch, 2 * CMP_HIDDEN), F32),
        compiler_params=_cparams(("parallel", "arbitrary")),
        name="cmp_proj",
    )(page_table, pages3, wk_cat, wv_cat)


def _cmp_mlp_kernel(p_ref, w1_ref, pe_ref, w2_ref, o_ref):
    proj = p_ref[0, 0, 0]
    n_ch = proj.shape[0]
    pre0 = jnp.sum(pe_ref[0] * w1_ref[0], axis=0, keepdims=True)
    first = proj[:, 0:CMP_HIDDEN]
    second = pltpu.roll(proj[:, CMP_HIDDEN:2 * CMP_HIDDEN], n_ch - 1, 0)
    pre = (pre0 + first) + second
    o_ref[0, 0, 0] = _dot(_gelu(pre).astype(BF16), w2_ref[0])


def cmp_mlp(proj, w1, pe, w2_bf):
    nb, _, _, n_ch, _ = proj.shape
    flat = CMP_BLOCK * HEAD_DIM
    return pl.pallas_call(
        _cmp_mlp_kernel,
        grid=(nb, 2, N_KV),
        in_specs=[pl.BlockSpec((1, 1, 1, n_ch, 2 * CMP_HIDDEN), lambda b, k, g: (b, k, g, 0, 0)),
                  pl.BlockSpec((1, flat, CMP_HIDDEN), lambda b, k, g: (k, 0, 0)),
                  pl.BlockSpec((1, flat, 1), lambda b, k, g: (k, 0, 0)),
                  pl.BlockSpec((1, CMP_HIDDEN, HEAD_DIM), lambda b, k, g: (k, 0, 0))],
        out_specs=pl.BlockSpec((1, 1, 1, n_ch, HEAD_DIM), lambda b, k, g: (b, k, g, 0, 0)),
        out_shape=jax.ShapeDtypeStruct((nb, 2, N_KV, n_ch, HEAD_DIM), F32),
        compiler_params=_cparams(("parallel", "parallel", "parallel")),
        name="cmp_mlp",
    )(proj, w1, pe, w2_bf)


def _kv_pack_kernel(pt_ref, x_ref, t_ref, k_ref, vt_ref, *, n_pages):
    p = pl.program_id(1)

    def emit(src):
        for g in range(N_KV):
            k_ref[0, g] = src[0, :, pl.ds(g * HEAD_DIM, HEAD_DIM)].astype(BF16)
            v = src[0, :, pl.ds((N_KV + g) * HEAD_DIM, HEAD_DIM)]
            vt_ref[0, g, 0] = v.T.astype(BF16)

    @pl.when(p < n_pages)
    def _():
        emit(x_ref)

    @pl.when(p >= n_pages)
    def _():
        emit(t_ref)


def kv_pack(pages3, col_block, page_table, tail, vt_tile):
    nb, n_pages = page_table.shape
    if tail is None:
        n_tail = 0
        tail = jnp.zeros((nb, PAGE, KV_SLAB), F32)
    else:
        n_tail = tail.shape[1] // PAGE
    n_tot = n_pages + n_tail
    per = vt_tile // PAGE
    grid_spec = pltpu.PrefetchScalarGridSpec(
        num_scalar_prefetch=1,
        grid=(nb, n_tot),
        in_specs=[pl.BlockSpec((1, PAGE, KV_SLAB),
                               lambda b, p, pt: (pt[b, jnp.minimum(p, n_pages - 1)], 0, col_block)),
                  pl.BlockSpec((1, PAGE, KV_SLAB), lambda b, p, pt: (b, jnp.maximum(p - n_pages, 0), 0))],
        out_specs=[pl.BlockSpec((1, N_KV, PAGE, HEAD_DIM), lambda b, p, pt: (b, 0, p, 0)),
                   pl.BlockSpec((1, N_KV, 1, HEAD_DIM, PAGE), lambda b, p, pt: (b, 0, p // per, 0, p % per))],
    )
    return pl.pallas_call(
        functools.partial(_kv_pack_kernel, n_pages=n_pages),
        grid_spec=grid_spec,
        out_shape=[jax.ShapeDtypeStruct((nb, N_KV, n_tot * PAGE, HEAD_DIM), BF16),
                   jax.ShapeDtypeStruct((nb, N_KV, n_tot // per, HEAD_DIM, vt_tile), BF16)],
        compiler_params=_cparams(("parallel", "arbitrary")),
        name="kv_pack",
    )(page_table, pages3, tail)


def _split3(x):
    hi = x.astype(BF16)
    r1 = x - hi.astype(F32)
    mid = r1.astype(BF16)
    lo = (r1 - mid.astype(F32)).astype(BF16)
    return hi, mid, lo


def _nsa_kernel(q_ref, qr_ref, gt_ref, kc_ref, vct_ref, ks_ref, vst_ref, kw_ref, vwt_ref, o_ref,
                imp_ref, sel_ref, m_ref, l_ref, acc_ref, *, tq, q_off, swa_base):
    rq = GROUP_Q * tq
    n_cp = kc_ref.shape[2]
    n_blk = sel_ref.shape[0]
    tw = kw_ref.shape[2]
    i = pl.program_id(2)
    t0 = q_off + i * tq
    lane = lax.broadcasted_iota(jnp.int32, (1, rq), 1)
    t_lane = t0 + lane % tq

    def rows(ref):
        x = ref[...]
        return jnp.concatenate([x[:, r * HEAD_DIM:(r + 1) * HEAD_DIM] for r in range(GROUP_Q)], axis=0)

    q2 = rows(q_ref)
    qr2 = rows(qr_ref)

    def softmax0(s, valid):
        s = jnp.where(valid, s, NEG)
        m = jnp.max(s, axis=0, keepdims=True)
        e = jnp.where(valid, jnp.exp(s - m), 0.0)
        den = jnp.sum(e, axis=0, keepdims=True)
        return e / jnp.where(den > 0.0, den, 1.0)

    sc = _dot_nt(kc_ref[0, 0], q2)
    n_io = lax.broadcasted_iota(jnp.int32, (n_cp, 1), 0)
    p_c = softmax0(sc, (n_io * CMP_STRIDE + (CMP_BLOCK - 1)) <= t_lane)
    o_c = _dot(vct_ref[0, 0], p_c.astype(BF16))
    p_sum = p_c
    for r in range(1, GROUP_Q):
        p_sum = p_sum + pltpu.roll(p_c, r * tq, 1)
    s_col = lax.broadcasted_iota(jnp.int32, (n_blk, 1), 0)
    n_row = lax.broadcasted_iota(jnp.int32, (1, n_cp), 1)
    ov = jnp.logical_and(n_row * CMP_STRIDE < (s_col + 1) * SLC_BLOCK,
                         n_row * CMP_STRIDE + CMP_BLOCK > s_col * SLC_BLOCK)
    ov = jnp.where(ov, 1.0, 0.0).astype(BF16)
    hi, mid, lo = _split3(p_sum)
    imp = (_dot(ov, hi) + _dot(ov, mid)) + _dot(ov, lo)
    cur = t_lane // SLC_BLOCK
    forced = jnp.logical_or(s_col == 0, jnp.logical_or(s_col == cur, s_col == cur - 1))
    elig = s_col * SLC_BLOCK <= t_lane
    imp = jnp.where(forced, FORCE_SCORE, imp)
    imp = jnp.where(elig, imp, -jnp.inf)
    imp_ref[...] = imp

    def rank_step(sp, cnt):
        row = imp_ref[pl.ds(sp, 1), :]
        beats = jnp.logical_or(row > imp, jnp.logical_and(row == imp, sp < s_col))
        return cnt + jnp.where(beats, 1.0, 0.0)

    cnt = lax.fori_loop(0, n_blk, rank_step, jnp.zeros((n_blk, rq), F32))
    sel_ref[...] = jnp.where(jnp.logical_and(elig, cnt < float(N_SELECT)), 1.0, 0.0)

    m_ref[...] = jnp.full_like(m_ref, NEG)
    l_ref[...] = jnp.zeros_like(l_ref)
    acc_ref[...] = jnp.zeros_like(acc_ref)
    per = SLC_TILE // SLC_BLOCK
    k_io = lax.broadcasted_iota(jnp.int32, (SLC_TILE, 1), 0)

    def slc_step(kt, carry):
        base = pl.multiple_of(kt * SLC_TILE, SLC_TILE)
        s = _dot_nt(ks_ref[0, 0, pl.ds(base, SLC_TILE), :], qr2)
        selx = jnp.concatenate(
            [jnp.broadcast_to(sel_ref[pl.ds(kt * per + jb, 1), :], (SLC_BLOCK, rq)) for jb in range(per)], axis=0)
        valid = jnp.logical_and(selx > 0.5, (base + k_io) <= t_lane)
        s = jnp.where(valid, s, NEG)
        m_old = m_ref[...]
        m_new = jnp.maximum(m_old, jnp.max(s, axis=0, keepdims=True))
        alpha = jnp.exp(m_old - m_new)
        e = jnp.where(valid, jnp.exp(s - m_new), 0.0)
        l_ref[...] = alpha * l_ref[...] + jnp.sum(e, axis=0, keepdims=True)
        acc_ref[...] = alpha * acc_ref[...] + _dot(vst_ref[0, 0, kt], e.astype(BF16))
        m_ref[...] = m_new
        return carry

    n_kt = (t0 + tq - 1) // SLC_TILE + 1
    lax.fori_loop(0, n_kt, slc_step, 0)
    l = l_ref[...]
    o_s = acc_ref[...] / jnp.where(l > 0.0, l, 1.0)

    start = jnp.clip(t0 - WINDOW - swa_base, 0, tw - SWA_SPAN)
    start = pl.multiple_of(start, PAGE)
    sw = _dot_nt(kw_ref[0, 0, pl.ds(start, SWA_SPAN), :], qr2)
    key_pos = swa_base + start + lax.broadcasted_iota(jnp.int32, (SWA_SPAN, 1), 0)
    dist = t_lane - key_pos
    p_w = softmax0(sw, jnp.logical_and(jnp.logical_and(dist >= 0, dist < WINDOW), key_pos >= swa_base))
    p_w = p_w.astype(BF16)
    o_w = jnp.zeros((HEAD_DIM, rq), F32)
    for jt in range(SWA_SPAN // PAGE):
        o_w = o_w + _dot(vwt_ref[0, 0, start // PAGE + jt], p_w[jt * PAGE:(jt + 1) * PAGE, :])

    gt = gt_ref[0, 0, 0]
    o_t = o_c * gt[0:1, :] + o_s * gt[1:2, :] + o_w * gt[2:3, :]
    for c in range(rq // HEAD_DIM):
        blk = o_t[:, c * HEAD_DIM:(c + 1) * HEAD_DIM].T.astype(BF16)
        per_blk = HEAD_DIM // tq
        for rr in range(per_blk):
            r = c * per_blk + rr
            o_ref[:, r * HEAD_DIM:(r + 1) * HEAD_DIM] = blk[rr * tq:(rr + 1) * tq, :]


def nsa_attention(q, qr, gates_t, kc, vct, ks, vst, kw, vwt, *, nb, nq, tq, q_off, swa_base):
    rq = GROUP_Q * tq
    n_cp = kc.shape[2]
    tk = ks.shape[2]
    n_blk = tk // SLC_BLOCK
    tw = kw.shape[2]
    qspec = pl.BlockSpec((tq, GROUP_W), lambda b, g, i: (b * nq + i, g))
    full = lambda a: pl.BlockSpec((1, 1) + a.shape[2:], lambda b, g, i: (b, g) + (0,) * (a.ndim - 2))
    return pl.pallas_call(
        functools.partial(_nsa_kernel, tq=tq, q_off=q_off, swa_base=swa_base),
        grid=(nb, N_KV, nq),
        in_specs=[qspec, qspec,
                  pl.BlockSpec((1, 1, 1, N_BRANCH, rq), lambda b, g, i: (b, g, i, 0, 0)),
                  full(kc), full(vct), full(ks), full(vst), full(kw), full(vwt)],
        out_specs=qspec,
        out_shape=jax.ShapeDtypeStruct((nb * nq * tq, Q_WIDTH), BF16),
        scratch_shapes=[pltpu.VMEM((n_blk, rq), F32), pltpu.VMEM((n_blk, rq), F32),
                        pltpu.VMEM((1, rq), F32), pltpu.VMEM((1, rq), F32), pltpu.VMEM((HEAD_DIM, rq), F32)],
        compiler_params=_cparams(("parallel", "parallel", "arbitrary")),
        name="nsa_attention",
    )(q, qr, gates_t, kc, vct, ks, vst, kw, vwt)


def _s5_layer(x, h0_re, h0_im, lp, nb, seq):
    m = x.shape[0]
    nc = -(-seq // S5_CHUNK)
    pad = nc * S5_CHUNK - seq
    xn = rms_cast(x, lp['norm_pre'])
    u = xn.reshape(nb, seq, N_GROUPS, SSM_GROUP)
    u = jnp.pad(u, ((0, 0), (pad, 0), (0, 0), (0, 0)))
    u = u.reshape(nb, nc, S5_CHUNK, N_GROUPS, SSM_GROUP).transpose(3, 0, 1, 2, 4).reshape(N_GROUPS, nb * nc, S5_CW)
    w_int = lp['w_int']
    if pad:
        w4 = w_int.reshape(N_GROUPS, 2 * SSM_STATE, S5_CHUNK, SSM_GROUP)
        w_int = jnp.pad(w4[:, :, :S5_CHUNK - pad], ((0, 0), (0, 0), (pad, 0), (0, 0))).reshape(w_int.shape)
    n_real = S5_CHUNK - pad
    col = (n_real - 1) * SSM_GROUP
    a_re = lp['p_re'][:, :, col].reshape(N_GROUPS, 1, SSM_STATE)
    a_im = lp['p_im'][:, :, col].reshape(N_GROUPS, 1, SSM_STATE)
    if h0_re is None:
        h0_re = jnp.zeros((N_GROUPS, nb, SSM_STATE), F32)
        h0_im = h0_re
    else:
        h0_re = h0_re.transpose(1, 0, 2)
        h0_im = h0_im.transpose(1, 0, 2)
    y, hf_re, hf_im = s5_apply(u, lp['w_intra'], lp['ws_re'], lp['ws_im'], w_int.astype(BF16),
                               a_re, a_im, h0_re, h0_im, nb, nc)
    y = y.reshape(N_GROUPS, nb, nc, S5_CHUNK, SSM_GROUP).transpose(1, 2, 3, 0, 4).reshape(nb, nc * S5_CHUNK, D_MODEL)
    y = y[:, pad:].reshape(m, D_MODEL)
    x = glu_tail(x, y, lp['norm_pre'], lp['d'], lp['w_glu'], lp['b_glu'], lp['norm_post'])
    return x, hf_re.transpose(1, 0, 2), hf_im.transpose(1, 0, 2)


def _gates_t(gates, nb, nq, tq):
    g = gates[:, :N_HEADS * N_BRANCH].reshape(nb, nq, tq, N_KV, GROUP_Q, N_BRANCH)
    return g.transpose(0, 3, 1, 5, 4, 2).reshape(nb, N_KV, nq, N_BRANCH, GROUP_Q * tq)


def _pad_rows(a, nb, seq, tq):
    if seq == tq or seq % tq == 0:
        return a
    a = a.reshape(nb, seq, -1)
    return jnp.pad(a, ((0, 0), (0, tq - seq), (0, 0))).reshape(nb * tq, -1)


def _nsa_layer(x, lp, kvs, tables, nb, seq, tq, q_off, swa_base):
    q, qr = q_proj(x, lp['norm_pre'], lp['w_q'], tables)
    gates = gate_proj(x, lp['norm_pre'], lp['w_g'])
    nq = -(-seq // tq)
    o = nsa_attention(_pad_rows(q, nb, seq, tq), _pad_rows(qr, nb, seq, tq),
                      _gates_t(_pad_rows(gates, nb, seq, tq), nb, nq, tq),
                      *kvs, nb=nb, nq=nq, tq=tq, q_off=q_off, swa_base=swa_base)
    if nq * tq != seq:
        o = o.reshape(nb, nq * tq, Q_WIDTH)[:, :seq].reshape(nb * seq, Q_WIDTH)
    return oproj(o, lp['w_o'], lp['norm_post'], x)


def _trunk(x, nb, seq, pos0, h0_re, h0_im, past, prm):
    m = nb * seq
    ssm_re, ssm_im = [], []
    n_a = len(prm['a'])
    for layer in range(n_a):
        x, hr, hi = _s5_layer(x, None if h0_re is None else h0_re[layer],
                              None if h0_im is None else h0_im[layer], prm['a'][layer], nb, seq)
        ssm_re.append(hr)
        ssm_im.append(hi)
        ml = prm['mlp'][layer]
        x = mlp(x, ml['norm_pre'], ml['w_up'], ml['w_down'], ml['norm_post'])

    pos = pos0 + jnp.tile(jnp.arange(seq), nb)
    tables = _rope_tables(pos)
    kv = kv_proj(x, prm['kv_norm'], prm['w_kv'], tables)
    rows_cmp = kv[:, 0:KV_SLAB].reshape(nb, seq, 2, N_KV, HEAD_DIM)
    rows_slc = kv[:, KV_SLAB:2 * KV_SLAB].reshape(nb, seq, 2, N_KV, HEAD_DIM)
    rows_swa = kv[:, 2 * KV_SLAB:3 * KV_SLAB]
    no_tail = None
    if past is None:
        npg = seq // PAGE
        table = jnp.arange(nb * npg, dtype=jnp.int32).reshape(nb, npg)
        kv3 = kv.reshape(nb * npg, PAGE, 3 * KV_SLAB)
        cproj = cmp_proj(kv3, 0, table, prm['cmp_w1k_cat'], prm['cmp_w1v_cat'])
        ks, vst = kv_pack(kv3, 1, table, no_tail, SLC_TILE)
        kw, vwt = kv_pack(kv3, 2, table, no_tail, PAGE)
        swa_base = 0
        tq = 128
        swa_buf = rows_swa.reshape(nb, seq, 2, N_KV, HEAD_DIM)[:, seq - WINDOW:]
    else:
        cache_cmp, cache_slc, state_swa, table = past
        n_pool = cache_cmp.shape[0]
        npg = table.shape[1]
        cproj = cmp_proj(cache_cmp.reshape(n_pool, PAGE, KV_SLAB), 0, table, prm['cmp_w1k_cat'], prm['cmp_w1v_cat'])
        t_real = npg * PAGE + seq
        t_pad = -(-t_real // SLC_TILE) * SLC_TILE
        tail = jnp.pad(kv[:, KV_SLAB:2 * KV_SLAB].reshape(nb, seq, KV_SLAB),
                       ((0, 0), (0, t_pad - npg * PAGE - seq), (0, 0)))
        ks, vst = kv_pack(cache_slc.reshape(n_pool, PAGE, KV_SLAB), 0, table, tail, SLC_TILE)
        w_keep = state_swa.shape[1]
        local = jnp.concatenate([state_swa.reshape(nb, w_keep, KV_SLAB), rows_swa.reshape(nb, seq, KV_SLAB)], axis=1)
        swa_buf = local[:, -w_keep:].reshape(nb, w_keep, 2, N_KV, HEAD_DIM)
        tw = -(-max(local.shape[1], SWA_SPAN) // PAGE) * PAGE
        local = jnp.pad(local, ((0, 0), (0, tw - local.shape[1]), (0, 0)))
        lt = jnp.arange(nb * (tw // PAGE), dtype=jnp.int32).reshape(nb, tw // PAGE)
        kw, vwt = kv_pack(local.reshape(nb * (tw // PAGE), PAGE, KV_SLAB), 0, lt, no_tail, PAGE)
        swa_base = pos0 - w_keep
        tq = 32
    cmp_out = cmp_mlp(cproj, prm['cmp_w1'], prm['cmp_pe'], prm['cmp_w2'])
    kc = cmp_out[:, 0].astype(BF16)
    vct = cmp_out[:, 1].transpose(0, 1, 3, 2).astype(BF16)
    kvs = (kc, vct, ks, vst, kw, vwt)

    for j, lp in enumerate(prm['b']):
        x = _nsa_layer(x, lp, kvs, tables, nb, seq, tq, pos0, swa_base)
        ml = prm['mlp'][n_a + j]
        x = mlp(x, ml['norm_pre'], ml['w_up'], ml['w_down'], ml['norm_post'])
    return x, rows_cmp, rows_slc, swa_buf, jnp.stack(ssm_re), jnp.stack(ssm_im)


def kernel(x_prompt, x_sample, cache_kv_cmp, cache_kv_slc, state_kv_swa, state_ssm_re, state_ssm_im, page_table,
           a_norm_pre, a_lam_re, a_lam_im, a_log_dt, a_b_re, a_b_im, a_c_re, a_c_im, a_d, a_w_glu, a_b_glu,
           a_norm_post, kv_norm, w_kv, cmp_w1_k, cmp_pe_k, cmp_w2_k, cmp_w1_v, cmp_pe_v, cmp_w2_v, b_norm_pre,
           b_w_qg, b_w_o, b_norm_post, mlp_norm_pre, mlp_w_up, mlp_w_down, mlp_norm_post):
    n_a = a_norm_pre.shape[0]
    n_b = b_norm_pre.shape[0]
    prm = {'a': [], 'b': [], 'mlp': []}
    for l in range(n_a):
        kt, x_re, x_im, w_int, p_re, p_im = s5_prep(a_lam_re[l], a_lam_im[l], a_log_dt[l], a_b_re[l], a_b_im[l],
                                                    a_c_re[l], a_c_im[l])
        w_intra, ws_re, ws_im = _s5_assemble(kt, x_re, x_im)
        prm['a'].append(dict(norm_pre=a_norm_pre[l], norm_post=a_norm_post[l], d=a_d[l],
                             w_glu=a_w_glu[l].astype(BF16), b_glu=a_b_glu[l],
                             w_intra=w_intra, ws_re=ws_re, ws_im=ws_im, w_int=w_int, p_re=p_re, p_im=p_im))
    n_gate = N_HEADS * N_BRANCH
    for l in range(n_b):
        w_g = jnp.pad(b_w_qg[l][:, Q_WIDTH:], ((0, 0), (0, HEAD_DIM - n_gate))).astype(BF16)
        prm['b'].append(dict(norm_pre=b_norm_pre[l], norm_post=b_norm_post[l],
                             w_q=b_w_qg[l][:, :Q_WIDTH].astype(BF16), w_g=w_g, w_o=b_w_o[l].astype(BF16)))
    for l in range(n_a + n_b):
        prm['mlp'].append(dict(norm_pre=mlp_norm_pre[l], norm_post=mlp_norm_post[l],
                               w_up=mlp_w_up[l].astype(BF16), w_down=mlp_w_down[l].astype(BF16)))
    prm['kv_norm'] = kv_norm
    prm['w_kv'] = w_kv.astype(BF16)
    r = CMP_BLOCK // CMP_STRIDE
    cat = lambda w: w.reshape(r, CMP_FLAT, CMP_HIDDEN).transpose(1, 0, 2).reshape(CMP_FLAT, r * CMP_HIDDEN).astype(BF16)
    prm['cmp_w1k_cat'] = cat(cmp_w1_k)
    prm['cmp_w1v_cat'] = cat(cmp_w1_v)
    flat = CMP_BLOCK * HEAD_DIM
    prm['cmp_w1'] = jnp.stack([cmp_w1_k.reshape(flat, CMP_HIDDEN), cmp_w1_v.reshape(flat, CMP_HIDDEN)])
    prm['cmp_pe'] = jnp.stack([cmp_pe_k.reshape(flat, 1), cmp_pe_v.reshape(flat, 1)])
    prm['cmp_w2'] = jnp.stack([cmp_w2_k, cmp_w2_v]).astype(BF16)

    bp, sp, _ = x_prompt.shape
    y_p, cmp_p, slc_p, swa_p, re_p, im_p = _trunk(x_prompt.reshape(bp * sp, D_MODEL), bp, sp, 0,
                                                  None, None, None, prm)
    bs, ss, _ = x_sample.shape
    past_len = page_table.shape[1] * PAGE
    y_s, cmp_s, slc_s, swa_s, re_s, im_s = _trunk(x_sample.reshape(bs * ss, D_MODEL), bs, ss, past_len,
                                                  state_ssm_re, state_ssm_im,
                                                  (cache_kv_cmp, cache_kv_slc, state_kv_swa, page_table), prm)
    return (y_p.reshape(bp, sp, D_MODEL), y_s.reshape(bs, ss, D_MODEL), cmp_p, cmp_s, slc_p, slc_s,
            swa_p, swa_s, re_p, im_p, re_s, im_s)
```

```python
import functools
import math

import jax
import jax.numpy as jnp
import numpy as np
from jax import lax
from jax.experimental import pallas as pl
from jax.experimental.pallas import tpu as pltpu

F32 = jnp.float32
BF16 = jnp.bfloat16

D_MODEL = 2048
N_HEADS = 16
HEAD_DIM = 128
N_KV = 4
GROUP_Q = N_HEADS // N_KV
N_BRANCH = 3
ROT_DIM = HEAD_DIM // 4
ROPE_THETA = 500000.0
SSM_GROUP = 16
N_GROUPS = D_MODEL // SSM_GROUP
SSM_STATE = 64
S5_CHUNK = 16
S5_CW = S5_CHUNK * SSM_GROUP
D_FF = 4 * D_MODEL
CMP_BLOCK = 32
CMP_STRIDE = 16
CMP_HIDDEN = 2 * HEAD_DIM
SLC_BLOCK = 64
N_SELECT = 16
WINDOW = 512
PAGE = 128
FORCE_SCORE = 1.0e4
EPS = 1e-6
Q_WIDTH = N_HEADS * HEAD_DIM
GROUP_W = GROUP_Q * HEAD_DIM
KV_SLAB = 2 * N_KV * HEAD_DIM
SLC_TILE = 512
SWA_SPAN = WINDOW + PAGE
NEG = -1.0e30
VMEM_LIMIT = 56 * 1024 * 1024


def _cparams(sem):
    return pltpu.CompilerParams(dimension_semantics=sem, vmem_limit_bytes=VMEM_LIMIT)


def _rms(x, g):
    var = jnp.mean(x * x, axis=-1, keepdims=True)
    return x * lax.rsqrt(var + EPS) * g


def _gelu(x):
    return 0.5 * x * (1.0 + jnp.tanh(math.sqrt(2.0 / math.pi) * (x + 0.044715 * (x * x * x))))


def _sigmoid(x):
    return 1.0 / (1.0 + jnp.exp(-x))


def _dot(a, b):
    return jnp.dot(a, b, preferred_element_type=F32)


def _dot_nt(a, b):
    return lax.dot_general(a, b, (((1,), (1,)), ((), ())), preferred_element_type=F32)


def _rope128(x, c, s1, s2):
    return x * c + pltpu.roll(x, HEAD_DIM - ROT_DIM // 2, 1) * s1 + pltpu.roll(x, ROT_DIM // 2, 1) * s2


def _rope_tables(pos):
    half = ROT_DIM // 2
    inv = ROPE_THETA ** (-jnp.arange(half, dtype=F32) / half)
    ang = pos.astype(F32)[:, None] * inv[None, :]
    cos, sin = jnp.cos(ang), jnp.sin(ang)
    n = pos.shape[0]
    rest = HEAD_DIM - ROT_DIM
    c = jnp.concatenate([cos, cos, jnp.ones((n, rest), F32)], axis=1)
    s1 = jnp.concatenate([-sin, jnp.zeros((n, HEAD_DIM - half), F32)], axis=1)
    s2 = jnp.concatenate([jnp.zeros((n, half), F32), sin, jnp.zeros((n, rest), F32)], axis=1)
    return c, s1, s2


def _row_tile(m):
    return 512 if m % 512 == 0 else m


def _kv_proj_kernel(x_ref, g_ref, w_ref, c_ref, s1_ref, s2_ref, o_ref, xn_ref):
    j = pl.program_id(1)

    @pl.when(j == 0)
    def _():
        xn_ref[...] = _rms(x_ref[...], g_ref[...]).astype(BF16)

    acc = _dot(xn_ref[...], w_ref[...])
    is_rope = jnp.logical_or(j == 2, j == 4)

    @pl.when(is_rope)
    def _():
        c, s1, s2 = c_ref[...], s1_ref[...], s2_ref[...]
        for h in range(N_KV):
            sl = slice(h * HEAD_DIM, (h + 1) * HEAD_DIM)
            o_ref[:, sl] = _rope128(acc[:, sl], c, s1, s2)

    @pl.when(jnp.logical_not(is_rope))
    def _():
        o_ref[...] = acc


def kv_proj(x, g, w_bf, tables):
    m = x.shape[0]
    tm = _row_tile(m)
    n = w_bf.shape[1]
    tn = N_KV * HEAD_DIM
    c, s1, s2 = tables
    tab = pl.BlockSpec((tm, HEAD_DIM), lambda i, j: (i, 0))
    return pl.pallas_call(
        _kv_proj_kernel,
        grid=(m // tm, n // tn),
        in_specs=[pl.BlockSpec((tm, D_MODEL), lambda i, j: (i, 0)),
                  pl.BlockSpec((1, D_MODEL), lambda i, j: (0, 0)),
                  pl.BlockSpec((D_MODEL, tn), lambda i, j: (0, j)),
                  tab, tab, tab],
        out_specs=pl.BlockSpec((tm, tn), lambda i, j: (i, j)),
        out_shape=jax.ShapeDtypeStruct((m, n), F32),
        scratch_shapes=[pltpu.VMEM((tm, D_MODEL), BF16)],
        compiler_params=_cparams(("parallel", "arbitrary")),
        name="kv_proj",
    )(x, g.reshape(1, -1), w_bf, c, s1, s2)


def _q_proj_kernel(x_ref, g_ref, w_ref, c_ref, s1_ref, s2_ref, q_ref, qr_ref, xn_ref):
    j = pl.program_id(1)

    @pl.when(j == 0)
    def _():
        xn_ref[...] = _rms(x_ref[...], g_ref[...]).astype(BF16)

    acc = _dot(xn_ref[...], w_ref[...])
    scale = HEAD_DIM ** -0.5
    c, s1, s2 = c_ref[...], s1_ref[...], s2_ref[...]
    q_ref[...] = (acc * scale).astype(BF16)
    for h in range(GROUP_Q):
        sl = slice(h * HEAD_DIM, (h + 1) * HEAD_DIM)
        qr_ref[:, sl] = (_rope128(acc[:, sl], c, s1, s2) * scale).astype(BF16)


def q_proj(x, g, wq_bf, tables):
    m = x.shape[0]
    tm = _row_tile(m)
    tn = GROUP_W
    c, s1, s2 = tables
    tab = pl.BlockSpec((tm, HEAD_DIM), lambda i, j: (i, 0))
    out = jax.ShapeDtypeStruct((m, Q_WIDTH), BF16)
    ospec = pl.BlockSpec((tm, tn), lambda i, j: (i, j))
    return pl.pallas_call(
        _q_proj_kernel,
        grid=(m // tm, Q_WIDTH // tn),
        in_specs=[pl.BlockSpec((tm, D_MODEL), lambda i, j: (i, 0)),
                  pl.BlockSpec((1, D_MODEL), lambda i, j: (0, 0)),
                  pl.BlockSpec((D_MODEL, tn), lambda i, j: (0, j)),
                  tab, tab, tab],
        out_specs=[ospec, ospec],
        out_shape=[out, out],
        scratch_shapes=[pltpu.VMEM((tm, D_MODEL), BF16)],
        compiler_params=_cparams(("parallel", "arbitrary")),
        name="q_proj",
    )(x, g.reshape(1, -1), wq_bf, c, s1, s2)


def _gate_proj_kernel(x_ref, g_ref, w_ref, o_ref):
    xn = _rms(x_ref[...], g_ref[...]).astype(BF16)
    o_ref[...] = _sigmoid(_dot(xn, w_ref[...]))


def gate_proj(x, g, wg_bf):
    m = x.shape[0]
    tm = _row_tile(m)
    n = wg_bf.shape[1]
    return pl.pallas_call(
        _gate_proj_kernel,
        grid=(m // tm,),
        in_specs=[pl.BlockSpec((tm, D_MODEL), lambda i: (i, 0)),
                  pl.BlockSpec((1, D_MODEL), lambda i: (0, 0)),
                  pl.BlockSpec((D_MODEL, n), lambda i: (0, 0))],
        out_specs=pl.BlockSpec((tm, n), lambda i: (i, 0)),
        out_shape=jax.ShapeDtypeStruct((m, n), F32),
        compiler_params=_cparams(("parallel",)),
        name="gate_proj",
    )(x, g.reshape(1, -1), wg_bf)


def _rms_cast_kernel(x_ref, g_ref, o_ref):
    o_ref[...] = _rms(x_ref[...], g_ref[...]).astype(BF16)


def rms_cast(x, g):
    m = x.shape[0]
    tm = _row_tile(m)
    return pl.pallas_call(
        _rms_cast_kernel,
        grid=(m // tm,),
        in_specs=[pl.BlockSpec((tm, D_MODEL), lambda i: (i, 0)),
                  pl.BlockSpec((1, D_MODEL), lambda i: (0, 0))],
        out_specs=pl.BlockSpec((tm, D_MODEL), lambda i: (i, 0)),
        out_shape=jax.ShapeDtypeStruct((m, D_MODEL), BF16),
        compiler_params=_cparams(("parallel",)),
        name="rms_cast",
    )(x, g.reshape(1, -1))


def _mlp_kernel(x_ref, gpre_ref, wup_ref, wdn_ref, gpost_ref, o_ref, xn_ref, acc_ref):
    j = pl.program_id(1)

    @pl.when(j == 0)
    def _():
        xn_ref[...] = _rms(x_ref[...], gpre_ref[...]).astype(BF16)
        acc_ref[...] = jnp.zeros_like(acc_ref)

    h = jnp.maximum(_dot(xn_ref[...], wup_ref[...]), 0.0)
    acc_ref[...] += _dot((h * h).astype(BF16), wdn_ref[...])

    @pl.when(j == pl.num_programs(1) - 1)
    def _():
        o_ref[...] = x_ref[...] + _rms(acc_ref[...], gpost_ref[...])


def mlp(x, gpre, wup_bf, wdn_bf, gpost):
    m = x.shape[0]
    tm = _row_tile(m)
    tf = 512
    return pl.pallas_call(
        _mlp_kernel,
        grid=(m // tm, D_FF // tf),
        in_specs=[pl.BlockSpec((tm, D_MODEL), lambda i, j: (i, 0)),
                  pl.BlockSpec((1, D_MODEL), lambda i, j: (0, 0)),
                  pl.BlockSpec((D_MODEL, tf), lambda i, j: (0, j)),
                  pl.BlockSpec((tf, D_MODEL), lambda i, j: (j, 0)),
                  pl.BlockSpec((1, D_MODEL), lambda i, j: (0, 0))],
        out_specs=pl.BlockSpec((tm, D_MODEL), lambda i, j: (i, 0)),
        out_shape=jax.ShapeDtypeStruct((m, D_MODEL), F32),
        scratch_shapes=[pltpu.VMEM((tm, D_MODEL), BF16), pltpu.VMEM((tm, D_MODEL), F32)],
        compiler_params=_cparams(("parallel", "arbitrary")),
        name="mlp",
    )(x, gpre.reshape(1, -1), wup_bf, wdn_bf, gpost.reshape(1, -1))


def _oproj_kernel(o_ref, w_ref, g_ref, res_ref, out_ref):
    out_ref[...] = res_ref[...] + _rms(_dot(o_ref[...], w_ref[...]), g_ref[...])


def oproj(o_bf, w_bf, g, res):
    m = o_bf.shape[0]
    tm = _row_tile(m)
    return pl.pallas_call(
        _oproj_kernel,
        grid=(m // tm,),
        in_specs=[pl.BlockSpec((tm, Q_WIDTH), lambda i: (i, 0)),
                  pl.BlockSpec((Q_WIDTH, D_MODEL), lambda i: (0, 0)),
                  pl.BlockSpec((1, D_MODEL), lambda i: (0, 0)),
                  pl.BlockSpec((tm, D_MODEL), lambda i: (i, 0))],
        out_specs=pl.BlockSpec((tm, D_MODEL), lambda i: (i, 0)),
        out_shape=jax.ShapeDtypeStruct((m, D_MODEL), F32),
        compiler_params=_cparams(("parallel",)),
        name="oproj",
    )(o_bf, w_bf, g.reshape(1, -1), res)


def _glu_kernel(x_ref, y_ref, gpre_ref, d_ref, w_ref, b_ref, gpost_ref, o_ref):
    x = x_ref[...]
    xn = _rms(x, gpre_ref[...])
    y = _gelu(y_ref[...] + d_ref[...] * xn)
    z = _dot(y.astype(BF16), w_ref[...]) + b_ref[...]
    o_ref[...] = x + _rms(y * _sigmoid(z), gpost_ref[...])


def glu_tail(x, y_ssm, gpre, d_skip, w_bf, b, gpost):
    m = x.shape[0]
    tm = _row_tile(m)
    vec = pl.BlockSpec((1, D_MODEL), lambda i: (0, 0))
    row = pl.BlockSpec((tm, D_MODEL), lambda i: (i, 0))
    return pl.pallas_call(
        _glu_kernel,
        grid=(m // tm,),
        in_specs=[row, row, vec, vec, pl.BlockSpec((D_MODEL, D_MODEL), lambda i: (0, 0)), vec, vec],
        out_specs=row,
        out_shape=jax.ShapeDtypeStruct((m, D_MODEL), F32),
        compiler_params=_cparams(("parallel",)),
        name="glu_tail",
    )(x, y_ssm, gpre.reshape(1, -1), d_skip.reshape(1, -1), w_bf, b.reshape(1, -1), gpost.reshape(1, -1))


S5_GB = 8


def _s5_prep_kernel(lr_ref, li_ref, ldt_ref, br_ref, bi_ref, cr_ref, ci_ref, ctr_ref, cti_ref,
                    kt_ref, xr_ref, xi_ref, wint_ref, pr_ref, pi_ref):
    lr = lr_ref[...]
    li = li_ref[...]
    dt = jnp.exp(ldt_ref[...])
    ldr = lr * dt
    ldi = li * dt
    mag = jnp.exp(ldr)
    a_re = mag * jnp.cos(ldi)
    a_im = mag * jnp.sin(ldi)
    den = lr * lr + li * li
    nr = a_re - 1.0
    f_re = (nr * lr + a_im * li) / den
    f_im = (a_im * lr - nr * li) / den
    br = br_ref[...]
    bi = bi_ref[...]
    bb_re = f_re * br - f_im * bi
    bb_im = f_re * bi + f_im * br
    k = (lax.broadcasted_iota(jnp.int32, (1, 1, S5_CW), 2) // SSM_GROUP).astype(F32)
    mk = jnp.exp(ldr * k)
    ak_re = mk * jnp.cos(ldi * k)
    ak_im = mk * jnp.sin(ldi * k)
    x_re = ak_re * bb_re - ak_im * bb_im
    x_im = ak_re * bb_im + ak_im * bb_re
    xr_ref[...] = x_re
    xi_ref[...] = x_im
    hp = lax.Precision.HIGHEST
    kt_ref[...] = (jnp.einsum('gip,gpn->gin', cr_ref[...], x_re, precision=hp, preferred_element_type=F32)
                   - jnp.einsum('gip,gpn->gin', ci_ref[...], x_im, precision=hp, preferred_element_type=F32))
    p_re = ak_re * a_re - ak_im * a_im
    p_im = ak_re * a_im + ak_im * a_re
    pr_ref[...] = p_re
    pi_ref[...] = p_im
    ctr = ctr_ref[...]
    cti = cti_ref[...]
    wint_ref[:, 0:SSM_STATE, :] = ctr * p_re - cti * p_im
    wint_ref[:, SSM_STATE:2 * SSM_STATE, :] = -(ctr * p_im + cti * p_re)


def s5_prep(lam_re, lam_im, log_dt, b_re, b_im, c_re, c_im):
    g, p = lam_re.shape
    gb = S5_GB
    col = pl.BlockSpec((gb, p, 1), lambda i: (i, 0, 0))
    wide = pl.BlockSpec((gb, p, S5_CW), lambda i: (i, 0, 0))
    cmat = pl.BlockSpec((gb, SSM_GROUP, p), lambda i: (i, 0, 0))
    wide_shape = jax.ShapeDtypeStruct((g, p, S5_CW), F32)
    tile16 = lambda a: jnp.tile(a, (1, 1, S5_CHUNK))
    return pl.pallas_call(
        _s5_prep_kernel,
        grid=(g // gb,),
        in_specs=[col, col, pl.BlockSpec((gb, 1, 1), lambda i: (i, 0, 0)), wide, wide, cmat, cmat, wide, wide],
        out_specs=[pl.BlockSpec((gb, SSM_GROUP, S5_CW), lambda i: (i, 0, 0)), wide, wide,
                   pl.BlockSpec((gb, 2 * p, S5_CW), lambda i: (i, 0, 0)), wide, wide],
        out_shape=[jax.ShapeDtypeStruct((g, SSM_GROUP, S5_CW), F32), wide_shape, wide_shape,
                   jax.ShapeDtypeStruct((g, 2 * p, S5_CW), F32), wide_shape, wide_shape],
        compiler_params=_cparams(("parallel",)),
        name="s5_prep",
    )(lam_re.reshape(g, p, 1), lam_im.reshape(g, p, 1), log_dt.reshape(g, 1, 1),
      tile16(b_re), tile16(b_im), c_re, c_im,
      tile16(c_re.transpose(0, 2, 1)), tile16(c_im.transpose(0, 2, 1)))


def _s5_assemble(kt, x_re, x_im):
    g = kt.shape[0]
    t = S5_CHUNK
    km = kt.reshape(g, SSM_GROUP, t, SSM_GROUP)
    lag = np.arange(t)[None, :] - np.arange(t)[:, None]
    w = km[:, :, np.clip(lag, 0, t - 1), :]
    w = jnp.where((lag >= 0)[None, None, :, :, None], w, 0.0)
    w_intra = w.transpose(0, 2, 4, 3, 1).reshape(g, S5_CW, S5_CW)
    rev = lambda x: x.reshape(g, SSM_STATE, t, SSM_GROUP)[:, :, ::-1, :].reshape(g, SSM_STATE, S5_CW)
    return w_intra.astype(BF16), rev(x_re).astype(BF16), rev(x_im).astype(BF16)


def _s5_apply_kernel(u_ref, wi_ref, wsr_ref, wsi_ref, wint_ref, ar_ref, ai_ref, h0r_ref, h0i_ref,
                     y_ref, hfr_ref, hfi_ref, sr_ref, si_ref, hpr_ref, hpi_ref, *, nb, nc):
    u = u_ref[0]
    sr_ref[...] = _dot_nt(u, wsr_ref[0])
    si_ref[...] = _dot_nt(u, wsi_ref[0])
    ar = ar_ref[0]
    ai = ai_ref[0]
    h0 = tuple((h0r_ref[0, b:b + 1, :], h0i_ref[0, b:b + 1, :]) for b in range(nb))

    def step(c, hs):
        out = []
        for b in range(nb):
            hr, hi = hs[b]
            row = b * nc + c
            hpr_ref[pl.ds(row, 1), :] = hr
            hpi_ref[pl.ds(row, 1), :] = hi
            nr = ar * hr - ai * hi + sr_ref[pl.ds(row, 1), :]
            ni = ar * hi + ai * hr + si_ref[pl.ds(row, 1), :]
            out.append((nr, ni))
        return tuple(out)

    hs = lax.fori_loop(0, nc, step, h0)
    for b in range(nb):
        hfr_ref[0, b:b + 1, :] = hs[b][0]
        hfi_ref[0, b:b + 1, :] = hs[b][1]
    wint = wint_ref[0]
    y_ref[0] = (_dot(u, wi_ref[0])
                + _dot(hpr_ref[...].astype(BF16), wint[0:SSM_STATE])
                + _dot(hpi_ref[...].astype(BF16), wint[SSM_STATE:2 * SSM_STATE]))


def s5_apply(u, w_intra, ws_re, ws_im, w_int, a_re, a_im, h0_re, h0_im, nb, nc):
    g = u.shape[0]
    mc = nb * nc
    p = SSM_STATE
    blk = lambda s: pl.BlockSpec((1,) + s, lambda i: (i, 0, 0))
    return pl.pallas_call(
        functools.partial(_s5_apply_kernel, nb=nb, nc=nc),
        grid=(g,),
        in_specs=[blk((mc, S5_CW)), blk((S5_CW, S5_CW)), blk((p, S5_CW)), blk((p, S5_CW)), blk((2 * p, S5_CW)),
                  blk((1, p)), blk((1, p)), blk((nb, p)), blk((nb, p))],
        out_specs=[blk((mc, S5_CW)), blk((nb, p)), blk((nb, p))],
        out_shape=[jax.ShapeDtypeStruct((g, mc, S5_CW), F32),
                   jax.ShapeDtypeStruct((g, nb, p), F32), jax.ShapeDtypeStruct((g, nb, p), F32)],
        scratch_shapes=[pltpu.VMEM((mc, p), F32)] * 4,
        compiler_params=_cparams(("parallel",)),
        name="s5_apply",
    )(u, w_intra, ws_re, ws_im, w_int, a_re, a_im, h0_re, h0_im)


CMP_PAGES = 16
CMP_ROWS = CMP_PAGES * (PAGE // CMP_STRIDE)
CMP_FLAT = CMP_STRIDE * HEAD_DIM


PAGE_CHUNKS = KV_SLAB // HEAD_DIM
PAGE_ROWS = PAGE * PAGE_CHUNKS


def _page_specs(n, first):
    def spec(k):
        return pl.BlockSpec((1, PAGE_ROWS, HEAD_DIM), lambda b, p, pt: (pt[b, first(p) + k], 0, 0))
    return [spec(k) for k in range(n)]


def _cmp_proj_kernel(pt_ref, *refs):
    x_refs = refs[:CMP_PAGES]
    wk_ref, wv_ref, o_ref, lhs_ref = refs[CMP_PAGES:]
    per_page = PAGE // CMP_STRIDE
    for pg, x_ref in enumerate(x_refs):
        for kv in range(2):
            for g in range(N_KV):
                c = kv * N_KV + g
                for s in range(CMP_STRIDE):
                    piece = x_ref[0, pl.ds(s * PAGE_CHUNKS + c, per_page, stride=CMP_STRIDE * PAGE_CHUNKS), :]
                    lhs_ref[kv, g, pg * per_page:(pg + 1) * per_page, s * HEAD_DIM:(s + 1) * HEAD_DIM] = piece
    for kv, w_ref in ((0, wk_ref), (1, wv_ref)):
        for g in range(N_KV):
            o_ref[0, kv, g] = _dot(lhs_ref[kv, g].astype(BF16), w_ref[...])


def cmp_proj(pages, page_table, wk_cat, wv_cat):
    nb, npg = page_table.shape
    assert npg % CMP_PAGES == 0, "compression consumes whole groups of pages"
    n_ch = npg * (PAGE // CMP_STRIDE)
    wspec = pl.BlockSpec((CMP_FLAT, 2 * CMP_HIDDEN), lambda b, p, pt: (0, 0))
    grid_spec = pltpu.PrefetchScalarGridSpec(
        num_scalar_prefetch=1,
        grid=(nb, npg // CMP_PAGES),
        in_specs=_page_specs(CMP_PAGES, lambda p: p * CMP_PAGES) + [wspec, wspec],
        out_specs=pl.BlockSpec((1, 2, N_KV, CMP_ROWS, 2 * CMP_HIDDEN), lambda b, p, pt: (b, 0, 0, p, 0)),
        scratch_shapes=[pltpu.VMEM((2, N_KV, CMP_ROWS, CMP_FLAT), F32)],
    )
    return pl.pallas_call(
        _cmp_proj_kernel,
        grid_spec=grid_spec,
        out_shape=jax.ShapeDtypeStruct((nb, 2, N_KV, n_ch, 2 * CMP_HIDDEN), F32),
        compiler_params=_cparams(("parallel", "arbitrary")),
        name="cmp_proj",
    )(page_table, *([pages] * CMP_PAGES), wk_cat, wv_cat)


def _cmp_mlp_kernel(p_ref, w1_ref, pe_ref, w2_ref, o_ref):
    proj = p_ref[0, 0, 0]
    n_ch = proj.shape[0]
    pre0 = jnp.sum(pe_ref[0] * w1_ref[0], axis=0, keepdims=True)
    first = proj[:, 0:CMP_HIDDEN]
    second = pltpu.roll(proj[:, CMP_HIDDEN:2 * CMP_HIDDEN], n_ch - 1, 0)
    pre = (pre0 + first) + second
    o_ref[0, 0, 0] = _dot(_gelu(pre).astype(BF16), w2_ref[0])


def cmp_mlp(proj, w1, pe, w2_bf):
    nb, _, _, n_ch, _ = proj.shape
    flat = CMP_BLOCK * HEAD_DIM
    return pl.pallas_call(
        _cmp_mlp_kernel,
        grid=(nb, 2, N_KV),
        in_specs=[pl.BlockSpec((1, 1, 1, n_ch, 2 * CMP_HIDDEN), lambda b, k, g: (b, k, g, 0, 0)),
                  pl.BlockSpec((1, flat, CMP_HIDDEN), lambda b, k, g: (k, 0, 0)),
                  pl.BlockSpec((1, flat, 1), lambda b, k, g: (k, 0, 0)),
                  pl.BlockSpec((1, CMP_HIDDEN, HEAD_DIM), lambda b, k, g: (k, 0, 0))],
        out_specs=pl.BlockSpec((1, 1, 1, n_ch, HEAD_DIM), lambda b, k, g: (b, k, g, 0, 0)),
        out_shape=jax.ShapeDtypeStruct((nb, 2, N_KV, n_ch, HEAD_DIM), F32),
        compiler_params=_cparams(("parallel", "parallel", "parallel")),
        name="cmp_mlp",
    )(proj, w1, pe, w2_bf)


def _kv_pack_kernel(pt_ref, *refs, per, n_groups):
    x_refs = refs[:per]
    t_ref, k_ref, vt_ref = refs[per:]
    j = pl.program_id(1)

    def emit(head):
        for g in range(N_KV):
            for pg in range(per):
                sl = slice(pg * PAGE, (pg + 1) * PAGE)
                k_ref[0, g, sl, :] = head(pg, g).astype(BF16)
                vt_ref[0, g, 0, :, sl] = head(pg, N_KV + g).T.astype(BF16)

    @pl.when(j < n_groups)
    def _():
        emit(lambda pg, c: x_refs[pg][0, pl.ds(c, PAGE, stride=PAGE_CHUNKS), :])

    @pl.when(j >= n_groups)
    def _():
        emit(lambda pg, c: t_ref[0, pl.ds(pg * PAGE_ROWS + c, PAGE, stride=PAGE_CHUNKS), :])


def kv_pack(pages, page_table, tail, vt_tile):
    nb, n_pages = page_table.shape
    per = vt_tile // PAGE
    assert n_pages % per == 0
    n_groups = n_pages // per
    if tail is None:
        n_tail = 0
        tail = jnp.zeros((nb, per * PAGE_ROWS, HEAD_DIM), F32)
    else:
        n_tail = tail.shape[1] // vt_tile
        tail = tail.reshape(nb, n_tail * per * PAGE_ROWS, HEAD_DIM)
    n_tot = n_groups + n_tail
    grid_spec = pltpu.PrefetchScalarGridSpec(
        num_scalar_prefetch=1,
        grid=(nb, n_tot),
        in_specs=_page_specs(per, lambda p: jnp.minimum(p, n_groups - 1) * per)
        + [pl.BlockSpec((1, per * PAGE_ROWS, HEAD_DIM), lambda b, p, pt: (b, jnp.maximum(p - n_groups, 0), 0))],
        out_specs=[pl.BlockSpec((1, N_KV, vt_tile, HEAD_DIM), lambda b, p, pt: (b, 0, p, 0)),
                   pl.BlockSpec((1, N_KV, 1, HEAD_DIM, vt_tile), lambda b, p, pt: (b, 0, p, 0, 0))],
    )
    return pl.pallas_call(
        functools.partial(_kv_pack_kernel, per=per, n_groups=n_groups),
        grid_spec=grid_spec,
        out_shape=[jax.ShapeDtypeStruct((nb, N_KV, n_tot * vt_tile, HEAD_DIM), BF16),
                   jax.ShapeDtypeStruct((nb, N_KV, n_tot, HEAD_DIM, vt_tile), BF16)],
        compiler_params=_cparams(("parallel", "arbitrary")),
        name="kv_pack",
    )(page_table, *([pages] * per), tail)


def _split3(x):
    hi = x.astype(BF16)
    r1 = x - hi.astype(F32)
    mid = r1.astype(BF16)
    lo = (r1 - mid.astype(F32)).astype(BF16)
    return hi, mid, lo


def _softmax_cols(s, col_ok):
    m = jnp.max(s, axis=0, keepdims=True)
    e = jnp.exp(s - m)
    den = jnp.sum(e, axis=0, keepdims=True)
    return e * jnp.where(col_ok, 1.0 / den, 0.0)


def _nsa_kernel(q_ref, qr_ref, gt_ref, kc_ref, vct_ref, ks_ref, vst_ref, kw_ref, vwt_ref, o_ref,
                bias_ref, m_ref, l_ref, acc_ref, *, tq, q_off, swa_base):
    rq = GROUP_Q * tq
    w = min(rq, HEAD_DIM)
    reps = rq // w
    n_cp = kc_ref.shape[2]
    n_tiles = bias_ref.shape[0]
    per = SLC_TILE // SLC_BLOCK
    n_blk = n_tiles * per
    tw = kw_ref.shape[2]
    i = pl.program_id(2)
    t0 = q_off + i * tq
    t_w = t0 + lax.broadcasted_iota(jnp.int32, (1, w), 1) % tq

    def tile(x):
        return jnp.concatenate([x] * reps, axis=x.ndim - 1) if reps > 1 else x

    def rows(ref):
        x = ref[...]
        return jnp.concatenate([x[:, r * HEAD_DIM:(r + 1) * HEAD_DIM] for r in range(GROUP_Q)], axis=0)

    q2 = rows(q_ref)
    qr2 = rows(qr_ref)
    t_rq = tile(t_w)

    n_io = lax.broadcasted_iota(jnp.int32, (n_cp, 1), 0)
    bias_c = jnp.where((n_io * CMP_STRIDE + (CMP_BLOCK - 1)) <= t_w, 0.0, NEG)
    p_c = _softmax_cols(_dot_nt(kc_ref[0, 0], q2) + tile(bias_c), t_rq >= CMP_BLOCK - 1)
    o_c = _dot(vct_ref[0, 0], p_c.astype(BF16))
    if reps > 1:
        p_sum = p_c[:, 0:w]
        for r in range(1, reps):
            p_sum = p_sum + p_c[:, r * w:(r + 1) * w]
    else:
        p_sum = p_c
        for r in range(1, GROUP_Q):
            p_sum = p_sum + pltpu.roll(p_c, r * tq, 1)
    s_col = lax.broadcasted_iota(jnp.int32, (n_blk, 1), 0)
    n_row = lax.broadcasted_iota(jnp.int32, (1, n_cp), 1)
    ov = jnp.logical_and(n_row * CMP_STRIDE < (s_col + 1) * SLC_BLOCK,
                         n_row * CMP_STRIDE + CMP_BLOCK > s_col * SLC_BLOCK)
    ov = jnp.where(ov, 1.0, 0.0).astype(BF16)
    hi, mid, lo = _split3(p_sum)
    imp = (_dot(ov, hi) + _dot(ov, mid)) + _dot(ov, lo)
    cur = t_w // SLC_BLOCK
    forced = jnp.logical_or(s_col == 0, jnp.logical_or(s_col == cur, s_col == cur - 1))
    elig = s_col * SLC_BLOCK <= t_w
    imp = jnp.where(forced, FORCE_SCORE, imp)
    imp = jnp.where(elig, imp, -jnp.inf)

    s_colf = s_col.astype(F32)

    def pick(_, carry):
        work, sel = carry
        best = jnp.max(work, axis=0, keepdims=True)
        first = jnp.min(jnp.where(work == best, s_colf, float(n_blk)), axis=0, keepdims=True)
        hit = s_colf == first
        return jnp.where(hit, -jnp.inf, work), jnp.where(hit, 1.0, sel)

    _, sel = lax.fori_loop(0, N_SELECT, pick, (imp, jnp.zeros((n_blk, w), F32)))
    bias_s = jnp.where(jnp.logical_and(elig, sel > 0.5), 0.0, NEG).reshape(n_tiles, per, w)
    bias_s = jnp.concatenate([bias_s, jnp.zeros_like(bias_s)], axis=1).astype(BF16)
    bias_ref[...] = tile(bias_s)

    m_ref[...] = jnp.full_like(m_ref, NEG)
    l_ref[...] = jnp.zeros_like(l_ref)
    acc_ref[...] = jnp.zeros_like(acc_ref)
    qr_t = qr2.astype(F32).T.astype(BF16)
    k_io = lax.broadcasted_iota(jnp.int32, (SLC_TILE, 1), 0)
    c_io = lax.broadcasted_iota(jnp.int32, (1, HEAD_DIM), 1)
    onehot = jnp.where(k_io // SLC_BLOCK == c_io, 1.0, 0.0).astype(BF16)
    zpad = jnp.zeros((HEAD_DIM - 2 * per, rq), BF16)

    def scores(kt):
        base = pl.multiple_of(kt * SLC_TILE, SLC_TILE)
        lhs = jnp.concatenate([ks_ref[0, 0, pl.ds(base, SLC_TILE), :], onehot], axis=1)
        rhs = jnp.concatenate([qr_t, bias_ref[kt], zpad], axis=0)
        return _dot(lhs, rhs)

    def update(kt, s):
        m_old = m_ref[...]
        m_new = jnp.maximum(m_old, jnp.max(s, axis=0, keepdims=True))
        alpha = jnp.exp(m_old - m_new)
        e = jnp.exp(s - m_new)
        l_ref[...] = alpha * l_ref[...] + jnp.sum(e, axis=0, keepdims=True)
        acc_ref[...] = alpha * acc_ref[...] + _dot(vst_ref[0, 0, kt], e.astype(BF16))
        m_ref[...] = m_new

    def slc_step(kt, carry):
        update(kt, scores(kt))
        return carry

    k_last = t0 // SLC_TILE
    lax.fori_loop(0, k_last, slc_step, 0)
    causal = jnp.where((k_last * SLC_TILE + k_io) <= t_w, 0.0, NEG)
    update(k_last, scores(k_last) + tile(causal))
    o_s = acc_ref[...] * (1.0 / l_ref[...])

    start = jnp.clip(t0 - WINDOW - swa_base, 0, tw - SWA_SPAN)
    start = pl.multiple_of(start, PAGE)
    key_pos = swa_base + start + lax.broadcasted_iota(jnp.int32, (SWA_SPAN, 1), 0)
    dist = t_w - key_pos
    ok_w = jnp.logical_and(jnp.logical_and(dist >= 0, dist < WINDOW), key_pos >= swa_base)
    sw = _dot_nt(kw_ref[0, 0, pl.ds(start, SWA_SPAN), :], qr2) + tile(jnp.where(ok_w, 0.0, NEG))
    p_w = _softmax_cols(sw, True).astype(BF16)
    o_w = jnp.zeros((HEAD_DIM, rq), F32)
    for jt in range(SWA_SPAN // PAGE):
        o_w = o_w + _dot(vwt_ref[0, 0, start // PAGE + jt], p_w[jt * PAGE:(jt + 1) * PAGE, :])

    gt = gt_ref[0, 0, 0]
    o_t = o_c * gt[0:1, :] + o_s * gt[1:2, :] + o_w * gt[2:3, :]
    for c in range(rq // HEAD_DIM):
        blk = o_t[:, c * HEAD_DIM:(c + 1) * HEAD_DIM].T.astype(BF16)
        per_blk = HEAD_DIM // tq
        for rr in range(per_blk):
            r = c * per_blk + rr
            o_ref[:, r * HEAD_DIM:(r + 1) * HEAD_DIM] = blk[rr * tq:(rr + 1) * tq, :]


def nsa_attention(q, qr, gates_t, kc, vct, ks, vst, kw, vwt, *, nb, nq, tq, q_off, swa_base):
    rq = GROUP_Q * tq
    tk = ks.shape[2]
    assert SLC_TILE % tq == 0 and q_off % SLC_TILE == 0 and tk % SLC_TILE == 0
    qspec = pl.BlockSpec((tq, GROUP_W), lambda b, g, i: (b * nq + i, g))
    full = lambda a: pl.BlockSpec((1, 1) + a.shape[2:], lambda b, g, i: (b, g) + (0,) * (a.ndim - 2))
    return pl.pallas_call(
        functools.partial(_nsa_kernel, tq=tq, q_off=q_off, swa_base=swa_base),
        grid=(nb, N_KV, nq),
        in_specs=[qspec, qspec,
                  pl.BlockSpec((1, 1, 1, N_BRANCH, rq), lambda b, g, i: (b, g, i, 0, 0)),
                  full(kc), full(vct), full(ks), full(vst), full(kw), full(vwt)],
        out_specs=qspec,
        out_shape=jax.ShapeDtypeStruct((nb * nq * tq, Q_WIDTH), BF16),
        scratch_shapes=[pltpu.VMEM((tk // SLC_TILE, 2 * (SLC_TILE // SLC_BLOCK), rq), BF16),
                        pltpu.VMEM((1, rq), F32), pltpu.VMEM((1, rq), F32), pltpu.VMEM((HEAD_DIM, rq), F32)],
        compiler_params=_cparams(("parallel", "parallel", "arbitrary")),
        name="nsa_attention",
    )(q, qr, gates_t, kc, vct, ks, vst, kw, vwt)


def _s5_layer(x, h0_re, h0_im, lp, nb, seq):
    m = x.shape[0]
    nc = -(-seq // S5_CHUNK)
    pad = nc * S5_CHUNK - seq
    xn = rms_cast(x, lp['norm_pre'])
    u = xn.reshape(nb, seq, N_GROUPS, SSM_GROUP)
    u = jnp.pad(u, ((0, 0), (pad, 0), (0, 0), (0, 0)))
    u = u.reshape(nb, nc, S5_CHUNK, N_GROUPS, SSM_GROUP).transpose(3, 0, 1, 2, 4).reshape(N_GROUPS, nb * nc, S5_CW)
    w_int = lp['w_int']
    if pad:
        w4 = w_int.reshape(N_GROUPS, 2 * SSM_STATE, S5_CHUNK, SSM_GROUP)
        w_int = jnp.pad(w4[:, :, :S5_CHUNK - pad], ((0, 0), (0, 0), (pad, 0), (0, 0))).reshape(w_int.shape)
    n_real = S5_CHUNK - pad
    col = (n_real - 1) * SSM_GROUP
    a_re = lp['p_re'][:, :, col].reshape(N_GROUPS, 1, SSM_STATE)
    a_im = lp['p_im'][:, :, col].reshape(N_GROUPS, 1, SSM_STATE)
    if h0_re is None:
        h0_re = jnp.zeros((N_GROUPS, nb, SSM_STATE), F32)
        h0_im = h0_re
    else:
        h0_re = h0_re.transpose(1, 0, 2)
        h0_im = h0_im.transpose(1, 0, 2)
    y, hf_re, hf_im = s5_apply(u, lp['w_intra'], lp['ws_re'], lp['ws_im'], w_int.astype(BF16),
                               a_re, a_im, h0_re, h0_im, nb, nc)
    y = y.reshape(N_GROUPS, nb, nc, S5_CHUNK, SSM_GROUP).transpose(1, 2, 3, 0, 4).reshape(nb, nc * S5_CHUNK, D_MODEL)
    y = y[:, pad:].reshape(m, D_MODEL)
    x = glu_tail(x, y, lp['norm_pre'], lp['d'], lp['w_glu'], lp['b_glu'], lp['norm_post'])
    return x, hf_re.transpose(1, 0, 2), hf_im.transpose(1, 0, 2)


def _gates_t(gates, nb, nq, tq):
    g = gates[:, :N_HEADS * N_BRANCH].reshape(nb, nq, tq, N_KV, GROUP_Q, N_BRANCH)
    return g.transpose(0, 3, 1, 5, 4, 2).reshape(nb, N_KV, nq, N_BRANCH, GROUP_Q * tq)


def _pad_rows(a, nb, seq, tq):
    if seq == tq or seq % tq == 0:
        return a
    a = a.reshape(nb, seq, -1)
    return jnp.pad(a, ((0, 0), (0, tq - seq), (0, 0))).reshape(nb * tq, -1)


def _nsa_layer(x, lp, kvs, tables, nb, seq, tq, q_off, swa_base):
    q, qr = q_proj(x, lp['norm_pre'], lp['w_q'], tables)
    gates = gate_proj(x, lp['norm_pre'], lp['w_g'])
    nq = -(-seq // tq)
    o = nsa_attention(_pad_rows(q, nb, seq, tq), _pad_rows(qr, nb, seq, tq),
                      _gates_t(_pad_rows(gates, nb, seq, tq), nb, nq, tq),
                      *kvs, nb=nb, nq=nq, tq=tq, q_off=q_off, swa_base=swa_base)
    if nq * tq != seq:
        o = o.reshape(nb, nq * tq, Q_WIDTH)[:, :seq].reshape(nb * seq, Q_WIDTH)
    return oproj(o, lp['w_o'], lp['norm_post'], x)


def _trunk(x, nb, seq, pos0, h0_re, h0_im, past, prm):
    m = nb * seq
    ssm_re, ssm_im = [], []
    n_a = len(prm['a'])
    for layer in range(n_a):
        x, hr, hi = _s5_layer(x, None if h0_re is None else h0_re[layer],
                              None if h0_im is None else h0_im[layer], prm['a'][layer], nb, seq)
        ssm_re.append(hr)
        ssm_im.append(hi)
        ml = prm['mlp'][layer]
        x = mlp(x, ml['norm_pre'], ml['w_up'], ml['w_down'], ml['norm_post'])

    pos = pos0 + jnp.tile(jnp.arange(seq), nb)
    tables = _rope_tables(pos)
    kv = kv_proj(x, prm['kv_norm'], prm['w_kv'], tables)
    rows_cmp = kv[:, 0:KV_SLAB].reshape(nb, seq, 2, N_KV, HEAD_DIM)
    rows_slc = kv[:, KV_SLAB:2 * KV_SLAB].reshape(nb, seq, 2, N_KV, HEAD_DIM)
    rows_swa = kv[:, 2 * KV_SLAB:3 * KV_SLAB]
    no_tail = None
    if past is None:
        npg = seq // PAGE
        table = jnp.arange(nb * npg, dtype=jnp.int32).reshape(nb, npg)
        pages = lambda j: kv[:, j * KV_SLAB:(j + 1) * KV_SLAB].reshape(nb * npg, PAGE_ROWS, HEAD_DIM)
        cproj = cmp_proj(pages(0), table, prm['cmp_w1k_cat'], prm['cmp_w1v_cat'])
        ks, vst = kv_pack(pages(1), table, no_tail, SLC_TILE)
        kw, vwt = kv_pack(pages(2), table, no_tail, PAGE)
        swa_base = 0
        tq = 128
        swa_buf = rows_swa.reshape(nb, seq, 2, N_KV, HEAD_DIM)[:, seq - WINDOW:]
    else:
        cache_cmp, cache_slc, state_swa, table = past
        n_pool = cache_cmp.shape[0]
        npg = table.shape[1]
        cproj = cmp_proj(cache_cmp.reshape(n_pool, PAGE_ROWS, HEAD_DIM), table, prm['cmp_w1k_cat'], prm['cmp_w1v_cat'])
        t_real = npg * PAGE + seq
        t_pad = -(-t_real // SLC_TILE) * SLC_TILE
        tail = jnp.pad(kv[:, KV_SLAB:2 * KV_SLAB].reshape(nb, seq, KV_SLAB),
                       ((0, 0), (0, t_pad - npg * PAGE - seq), (0, 0)))
        ks, vst = kv_pack(cache_slc.reshape(n_pool, PAGE_ROWS, HEAD_DIM), table, tail, SLC_TILE)
        w_keep = state_swa.shape[1]
        local = jnp.concatenate([state_swa.reshape(nb, w_keep, KV_SLAB), rows_swa.reshape(nb, seq, KV_SLAB)], axis=1)
        swa_buf = local[:, -w_keep:].reshape(nb, w_keep, 2, N_KV, HEAD_DIM)
        tw = -(-max(local.shape[1], SWA_SPAN) // PAGE) * PAGE
        local = jnp.pad(local, ((0, 0), (0, tw - local.shape[1]), (0, 0)))
        lt = jnp.arange(nb * (tw // PAGE), dtype=jnp.int32).reshape(nb, tw // PAGE)
        kw, vwt = kv_pack(local.reshape(nb * (tw // PAGE), PAGE_ROWS, HEAD_DIM), lt, no_tail, PAGE)
        swa_base = pos0 - w_keep
        tq = 32
    cmp_out = cmp_mlp(cproj, prm['cmp_w1'], prm['cmp_pe'], prm['cmp_w2'])
    kc = cmp_out[:, 0].astype(BF16)
    vct = cmp_out[:, 1].transpose(0, 1, 3, 2).astype(BF16)
    kvs = (kc, vct, ks, vst, kw, vwt)

    for j, lp in enumerate(prm['b']):
        x = _nsa_layer(x, lp, kvs, tables, nb, seq, tq, pos0, swa_base)
        ml = prm['mlp'][n_a + j]
        x = mlp(x, ml['norm_pre'], ml['w_up'], ml['w_down'], ml['norm_post'])
    return x, rows_cmp, rows_slc, swa_buf, jnp.stack(ssm_re), jnp.stack(ssm_im)


def kernel(x_prompt, x_sample, cache_kv_cmp, cache_kv_slc, state_kv_swa, state_ssm_re, state_ssm_im, page_table,
           a_norm_pre, a_lam_re, a_lam_im, a_log_dt, a_b_re, a_b_im, a_c_re, a_c_im, a_d, a_w_glu, a_b_glu,
           a_norm_post, kv_norm, w_kv, cmp_w1_k, cmp_pe_k, cmp_w2_k, cmp_w1_v, cmp_pe_v, cmp_w2_v, b_norm_pre,
           b_w_qg, b_w_o, b_norm_post, mlp_norm_pre, mlp_w_up, mlp_w_down, mlp_norm_post):
    n_a = a_norm_pre.shape[0]
    n_b = b_norm_pre.shape[0]
    prm = {'a': [], 'b': [], 'mlp': []}
    for l in range(n_a):
        kt, x_re, x_im, w_int, p_re, p_im = s5_prep(a_lam_re[l], a_lam_im[l], a_log_dt[l], a_b_re[l], a_b_im[l],
                                                    a_c_re[l], a_c_im[l])
        w_intra, ws_re, ws_im = _s5_assemble(kt, x_re, x_im)
        prm['a'].append(dict(norm_pre=a_norm_pre[l], norm_post=a_norm_post[l], d=a_d[l],
                             w_glu=a_w_glu[l].astype(BF16), b_glu=a_b_glu[l],
                             w_intra=w_intra, ws_re=ws_re, ws_im=ws_im, w_int=w_int, p_re=p_re, p_im=p_im))
    n_gate = N_HEADS * N_BRANCH
    for l in range(n_b):
        w_g = jnp.pad(b_w_qg[l][:, Q_WIDTH:], ((0, 0), (0, HEAD_DIM - n_gate))).astype(BF16)
        prm['b'].append(dict(norm_pre=b_norm_pre[l], norm_post=b_norm_post[l],
                             w_q=b_w_qg[l][:, :Q_WIDTH].astype(BF16), w_g=w_g, w_o=b_w_o[l].astype(BF16)))
    for l in range(n_a + n_b):
        prm['mlp'].append(dict(norm_pre=mlp_norm_pre[l], norm_post=mlp_norm_post[l],
                               w_up=mlp_w_up[l].astype(BF16), w_down=mlp_w_down[l].astype(BF16)))
    prm['kv_norm'] = kv_norm
    prm['w_kv'] = w_kv.astype(BF16)
    r = CMP_BLOCK // CMP_STRIDE
    cat = lambda w: w.reshape(r, CMP_FLAT, CMP_HIDDEN).transpose(1, 0, 2).reshape(CMP_FLAT, r * CMP_HIDDEN).astype(BF16)
    prm['cmp_w1k_cat'] = cat(cmp_w1_k)
    prm['cmp_w1v_cat'] = cat(cmp_w1_v)
    flat = CMP_BLOCK * HEAD_DIM
    prm['cmp_w1'] = jnp.stack([cmp_w1_k.reshape(flat, CMP_HIDDEN), cmp_w1_v.reshape(flat, CMP_HIDDEN)])
    prm['cmp_pe'] = jnp.stack([cmp_pe_k.reshape(flat, 1), cmp_pe_v.reshape(flat, 1)])
    prm['cmp_w2'] = jnp.stack([cmp_w2_k, cmp_w2_v]).astype(BF16)

    bp, sp, _ = x_prompt.shape
    y_p, cmp_p, slc_p, swa_p, re_p, im_p = _trunk(x_prompt.reshape(bp * sp, D_MODEL), bp, sp, 0,
                                                  None, None, None, prm)
    bs, ss, _ = x_sample.shape
    past_len = page_table.shape[1] * PAGE
    y_s, cmp_s, slc_s, swa_s, re_s, im_s = _trunk(x_sample.reshape(bs * ss, D_MODEL), bs, ss, past_len,
                                                  state_ssm_re, state_ssm_im,
                                                  (cache_kv_cmp, cache_kv_slc, state_kv_swa, page_table), prm)
    return (y_p.reshape(bp, sp, D_MODEL), y_s.reshape(bs, ss, D_MODEL), cmp_p, cmp_s, slc_p, slc_s,
            swa_p, swa_s, re_p, im_p, re_s, im_s)
```

```python
import functools
import math

import jax
import jax.numpy as jnp
from jax import lax
from jax.experimental import pallas as pl
from jax.experimental.pallas import tpu as pltpu

F32 = jnp.float32
BF16 = jnp.bfloat16

D_MODEL = 2048
N_HEADS = 16
HEAD_DIM = 128
N_KV = 4
GROUP_Q = N_HEADS // N_KV
N_BRANCH = 3
ROT_DIM = HEAD_DIM // 4
ROPE_THETA = 500000.0
SSM_GROUP = 16
N_GROUPS = D_MODEL // SSM_GROUP
SSM_STATE = 64
S5_CHUNK = 16
S5_CW = S5_CHUNK * SSM_GROUP
D_FF = 4 * D_MODEL
CMP_BLOCK = 32
CMP_STRIDE = 16
CMP_HIDDEN = 2 * HEAD_DIM
SLC_BLOCK = 64
N_SELECT = 16
WINDOW = 512
PAGE = 128
FORCE_SCORE = 1.0e4
EPS = 1e-6
Q_WIDTH = N_HEADS * HEAD_DIM
GROUP_W = GROUP_Q * HEAD_DIM
KV_SLAB = 2 * N_KV * HEAD_DIM
SLC_TILE = 512
SWA_SPAN = WINDOW + PAGE
NEG = -1.0e30
VMEM_LIMIT = 56 * 1024 * 1024


def _cparams(sem):
    return pltpu.CompilerParams(dimension_semantics=sem, vmem_limit_bytes=VMEM_LIMIT)


def _rms(x, g):
    var = jnp.mean(x * x, axis=-1, keepdims=True)
    return x * lax.rsqrt(var + EPS) * g


def _gelu(x):
    return 0.5 * x * (1.0 + jnp.tanh(math.sqrt(2.0 / math.pi) * (x + 0.044715 * (x * x * x))))


def _sigmoid(x):
    return 1.0 / (1.0 + jnp.exp(-x))


def _dot(a, b):
    return jnp.dot(a, b, preferred_element_type=F32)


def _dot_nt(a, b):
    return lax.dot_general(a, b, (((1,), (1,)), ((), ())), preferred_element_type=F32)


def _rope128(x, c, s1, s2):
    return x * c + pltpu.roll(x, HEAD_DIM - ROT_DIM // 2, 1) * s1 + pltpu.roll(x, ROT_DIM // 2, 1) * s2


def _rope_tables(pos):
    half = ROT_DIM // 2
    inv = ROPE_THETA ** (-jnp.arange(half, dtype=F32) / half)
    ang = pos.astype(F32)[:, None] * inv[None, :]
    cos, sin = jnp.cos(ang), jnp.sin(ang)
    n = pos.shape[0]
    rest = HEAD_DIM - ROT_DIM
    c = jnp.concatenate([cos, cos, jnp.ones((n, rest), F32)], axis=1)
    s1 = jnp.concatenate([-sin, jnp.zeros((n, HEAD_DIM - half), F32)], axis=1)
    s2 = jnp.concatenate([jnp.zeros((n, half), F32), sin, jnp.zeros((n, rest), F32)], axis=1)
    return c, s1, s2


def _row_tile(m):
    return 512 if m % 512 == 0 else m


def _kv_proj_kernel(x_ref, g_ref, w_ref, c_ref, s1_ref, s2_ref, o_ref, xn_ref):
    j = pl.program_id(1)

    @pl.when(j == 0)
    def _():
        xn_ref[...] = _rms(x_ref[...], g_ref[...]).astype(BF16)

    acc = _dot(xn_ref[...], w_ref[...])
    is_rope = jnp.logical_or(j == 2, j == 4)

    @pl.when(is_rope)
    def _():
        c, s1, s2 = c_ref[...], s1_ref[...], s2_ref[...]
        for h in range(N_KV):
            sl = slice(h * HEAD_DIM, (h + 1) * HEAD_DIM)
            o_ref[:, sl] = _rope128(acc[:, sl], c, s1, s2)

    @pl.when(jnp.logical_not(is_rope))
    def _():
        o_ref[...] = acc


def kv_proj(x, g, w_bf, tables):
    m = x.shape[0]
    tm = _row_tile(m)
    n = w_bf.shape[1]
    tn = N_KV * HEAD_DIM
    c, s1, s2 = tables
    tab = pl.BlockSpec((tm, HEAD_DIM), lambda i, j: (i, 0))
    return pl.pallas_call(
        _kv_proj_kernel,
        grid=(m // tm, n // tn),
        in_specs=[pl.BlockSpec((tm, D_MODEL), lambda i, j: (i, 0)),
                  pl.BlockSpec((1, D_MODEL), lambda i, j: (0, 0)),
                  pl.BlockSpec((D_MODEL, tn), lambda i, j: (0, j)),
                  tab, tab, tab],
        out_specs=pl.BlockSpec((tm, tn), lambda i, j: (i, j)),
        out_shape=jax.ShapeDtypeStruct((m, n), F32),
        scratch_shapes=[pltpu.VMEM((tm, D_MODEL), BF16)],
        compiler_params=_cparams(("parallel", "arbitrary")),
        name="kv_proj",
    )(x, g.reshape(1, -1), w_bf, c, s1, s2)


def _q_proj_kernel(x_ref, g_ref, w_ref, c_ref, s1_ref, s2_ref, q_ref, qr_ref, xn_ref):
    j = pl.program_id(1)

    @pl.when(j == 0)
    def _():
        xn_ref[...] = _rms(x_ref[...], g_ref[...]).astype(BF16)

    acc = _dot(xn_ref[...], w_ref[...])
    scale = HEAD_DIM ** -0.5
    c, s1, s2 = c_ref[...], s1_ref[...], s2_ref[...]
    q_ref[...] = (acc * scale).astype(BF16)
    for h in range(GROUP_Q):
        sl = slice(h * HEAD_DIM, (h + 1) * HEAD_DIM)
        qr_ref[:, sl] = (_rope128(acc[:, sl], c, s1, s2) * scale).astype(BF16)


def q_proj(x, g, wq_bf, tables):
    m = x.shape[0]
    tm = _row_tile(m)
    tn = GROUP_W
    c, s1, s2 = tables
    tab = pl.BlockSpec((tm, HEAD_DIM), lambda i, j: (i, 0))
    out = jax.ShapeDtypeStruct((m, Q_WIDTH), BF16)
    ospec = pl.BlockSpec((tm, tn), lambda i, j: (i, j))
    return pl.pallas_call(
        _q_proj_kernel,
        grid=(m // tm, Q_WIDTH // tn),
        in_specs=[pl.BlockSpec((tm, D_MODEL), lambda i, j: (i, 0)),
                  pl.BlockSpec((1, D_MODEL), lambda i, j: (0, 0)),
                  pl.BlockSpec((D_MODEL, tn), lambda i, j: (0, j)),
                  tab, tab, tab],
        out_specs=[ospec, ospec],
        out_shape=[out, out],
        scratch_shapes=[pltpu.VMEM((tm, D_MODEL), BF16)],
        compiler_params=_cparams(("parallel", "arbitrary")),
        name="q_proj",
    )(x, g.reshape(1, -1), wq_bf, c, s1, s2)


def _gate_proj_kernel(x_ref, g_ref, w_ref, o_ref):
    xn = _rms(x_ref[...], g_ref[...]).astype(BF16)
    o_ref[...] = _sigmoid(_dot(xn, w_ref[...]))


def gate_proj(x, g, wg_bf):
    m = x.shape[0]
    tm = _row_tile(m)
    n = wg_bf.shape[1]
    return pl.pallas_call(
        _gate_proj_kernel,
        grid=(m // tm,),
        in_specs=[pl.BlockSpec((tm, D_MODEL), lambda i: (i, 0)),
                  pl.BlockSpec((1, D_MODEL), lambda i: (0, 0)),
                  pl.BlockSpec((D_MODEL, n), lambda i: (0, 0))],
        out_specs=pl.BlockSpec((tm, n), lambda i: (i, 0)),
        out_shape=jax.ShapeDtypeStruct((m, n), F32),
        compiler_params=_cparams(("parallel",)),
        name="gate_proj",
    )(x, g.reshape(1, -1), wg_bf)


def _rms_norm_kernel(x_ref, g_ref, o_ref):
    o_ref[...] = _rms(x_ref[...], g_ref[...])


def rms_norm(x, g):
    m = x.shape[0]
    tm = _row_tile(m)
    return pl.pallas_call(
        _rms_norm_kernel,
        grid=(m // tm,),
        in_specs=[pl.BlockSpec((tm, D_MODEL), lambda i: (i, 0)),
                  pl.BlockSpec((1, D_MODEL), lambda i: (0, 0))],
        out_specs=pl.BlockSpec((tm, D_MODEL), lambda i: (i, 0)),
        out_shape=jax.ShapeDtypeStruct((m, D_MODEL), F32),
        compiler_params=_cparams(("parallel",)),
        name="rms_norm",
    )(x, g.reshape(1, -1))


def _mlp_kernel(x_ref, gpre_ref, wup_ref, wdn_ref, gpost_ref, o_ref, xn_ref, acc_ref):
    j = pl.program_id(1)

    @pl.when(j == 0)
    def _():
        xn_ref[...] = _rms(x_ref[...], gpre_ref[...]).astype(BF16)
        acc_ref[...] = jnp.zeros_like(acc_ref)

    h = jnp.maximum(_dot(xn_ref[...], wup_ref[...]), 0.0)
    acc_ref[...] += _dot((h * h).astype(BF16), wdn_ref[...])

    @pl.when(j == pl.num_programs(1) - 1)
    def _():
        o_ref[...] = x_ref[...] + _rms(acc_ref[...], gpost_ref[...])


def mlp(x, gpre, wup_bf, wdn_bf, gpost):
    m = x.shape[0]
    tm = _row_tile(m)
    tf = 512
    return pl.pallas_call(
        _mlp_kernel,
        grid=(m // tm, D_FF // tf),
        in_specs=[pl.BlockSpec((tm, D_MODEL), lambda i, j: (i, 0)),
                  pl.BlockSpec((1, D_MODEL), lambda i, j: (0, 0)),
                  pl.BlockSpec((D_MODEL, tf), lambda i, j: (0, j)),
                  pl.BlockSpec((tf, D_MODEL), lambda i, j: (j, 0)),
                  pl.BlockSpec((1, D_MODEL), lambda i, j: (0, 0))],
        out_specs=pl.BlockSpec((tm, D_MODEL), lambda i, j: (i, 0)),
        out_shape=jax.ShapeDtypeStruct((m, D_MODEL), F32),
        scratch_shapes=[pltpu.VMEM((tm, D_MODEL), BF16), pltpu.VMEM((tm, D_MODEL), F32)],
        compiler_params=_cparams(("parallel", "arbitrary")),
        name="mlp",
    )(x, gpre.reshape(1, -1), wup_bf, wdn_bf, gpost.reshape(1, -1))


def _oproj_kernel(o_ref, w_ref, g_ref, res_ref, out_ref):
    out_ref[...] = res_ref[...] + _rms(_dot(o_ref[...], w_ref[...]), g_ref[...])


def oproj(o_bf, w_bf, g, res):
    m = o_bf.shape[0]
    tm = _row_tile(m)
    return pl.pallas_call(
        _oproj_kernel,
        grid=(m // tm,),
        in_specs=[pl.BlockSpec((tm, Q_WIDTH), lambda i: (i, 0)),
                  pl.BlockSpec((Q_WIDTH, D_MODEL), lambda i: (0, 0)),
                  pl.BlockSpec((1, D_MODEL), lambda i: (0, 0)),
                  pl.BlockSpec((tm, D_MODEL), lambda i: (i, 0))],
        out_specs=pl.BlockSpec((tm, D_MODEL), lambda i: (i, 0)),
        out_shape=jax.ShapeDtypeStruct((m, D_MODEL), F32),
        compiler_params=_cparams(("parallel",)),
        name="oproj",
    )(o_bf, w_bf, g.reshape(1, -1), res)


def _glu_kernel(x_ref, y_ref, gpre_ref, d_ref, w_ref, b_ref, gpost_ref, o_ref):
    x = x_ref[...]
    xn = _rms(x, gpre_ref[...])
    y = _gelu(y_ref[...] + d_ref[...] * xn)
    z = _dot(y.astype(BF16), w_ref[...]) + b_ref[...]
    o_ref[...] = x + _rms(y * _sigmoid(z), gpost_ref[...])


def glu_tail(x, y_ssm, gpre, d_skip, w_bf, b, gpost):
    m = x.shape[0]
    tm = _row_tile(m)
    vec = pl.BlockSpec((1, D_MODEL), lambda i: (0, 0))
    row = pl.BlockSpec((tm, D_MODEL), lambda i: (i, 0))
    return pl.pallas_call(
        _glu_kernel,
        grid=(m // tm,),
        in_specs=[row, row, vec, vec, pl.BlockSpec((D_MODEL, D_MODEL), lambda i: (0, 0)), vec, vec],
        out_specs=row,
        out_shape=jax.ShapeDtypeStruct((m, D_MODEL), F32),
        compiler_params=_cparams(("parallel",)),
        name="glu_tail",
    )(x, y_ssm, gpre.reshape(1, -1), d_skip.reshape(1, -1), w_bf, b.reshape(1, -1), gpost.reshape(1, -1))


S5_GB = 8


def _s5_prep_kernel(lr_ref, li_ref, ldt_ref, br_ref, bi_ref, cr_ref, ci_ref, ctr_ref, cti_ref,
                    kt_ref, xr_ref, xi_ref, wint_ref, pr_ref, pi_ref):
    lr = lr_ref[...]
    li = li_ref[...]
    dt = jnp.exp(ldt_ref[...])
    ldr = lr * dt
    ldi = li * dt
    mag = jnp.exp(ldr)
    a_re = mag * jnp.cos(ldi)
    a_im = mag * jnp.sin(ldi)
    den = lr * lr + li * li
    nr = a_re - 1.0
    f_re = (nr * lr + a_im * li) / den
    f_im = (a_im * lr - nr * li) / den
    br = br_ref[...]
    bi = bi_ref[...]
    bb_re = f_re * br - f_im * bi
    bb_im = f_re * bi + f_im * br
    k = (lax.broadcasted_iota(jnp.int32, (1, 1, S5_CW), 2) // SSM_GROUP).astype(F32)
    mk = jnp.exp(ldr * k)
    ak_re = mk * jnp.cos(ldi * k)
    ak_im = mk * jnp.sin(ldi * k)
    x_re = ak_re * bb_re - ak_im * bb_im
    x_im = ak_re * bb_im + ak_im * bb_re
    xr_ref[...] = x_re
    xi_ref[...] = x_im
    hp = lax.Precision.HIGHEST
    kt_ref[...] = (jnp.einsum('gip,gpn->gin', cr_ref[...], x_re, precision=hp, preferred_element_type=F32)
                   - jnp.einsum('gip,gpn->gin', ci_ref[...], x_im, precision=hp, preferred_element_type=F32))
    p_re = ak_re * a_re - ak_im * a_im
    p_im = ak_re * a_im + ak_im * a_re
    pr_ref[...] = p_re
    pi_ref[...] = p_im
    ctr = ctr_ref[...]
    cti = cti_ref[...]
    wint_ref[:, 0:SSM_STATE, :] = ctr * p_re - cti * p_im
    wint_ref[:, SSM_STATE:2 * SSM_STATE, :] = -(ctr * p_im + cti * p_re)


def s5_prep(lam_re, lam_im, log_dt, b_re, b_im, c_re, c_im):
    g, p = lam_re.shape
    gb = S5_GB
    col = pl.BlockSpec((gb, p, 1), lambda i: (i, 0, 0))
    wide = pl.BlockSpec((gb, p, S5_CW), lambda i: (i, 0, 0))
    cmat = pl.BlockSpec((gb, SSM_GROUP, p), lambda i: (i, 0, 0))
    wide_shape = jax.ShapeDtypeStruct((g, p, S5_CW), F32)
    tile16 = lambda a: jnp.tile(a, (1, 1, S5_CHUNK))
    return pl.pallas_call(
        _s5_prep_kernel,
        grid=(g // gb,),
        in_specs=[col, col, pl.BlockSpec((gb, 1, 1), lambda i: (i, 0, 0)), wide, wide, cmat, cmat, wide, wide],
        out_specs=[pl.BlockSpec((gb, SSM_GROUP, S5_CW), lambda i: (i, 0, 0)), wide, wide,
                   pl.BlockSpec((gb, 2 * p, S5_CW), lambda i: (i, 0, 0)), wide, wide],
        out_shape=[jax.ShapeDtypeStruct((g, SSM_GROUP, S5_CW), F32), wide_shape, wide_shape,
                   jax.ShapeDtypeStruct((g, 2 * p, S5_CW), F32), wide_shape, wide_shape],
        compiler_params=_cparams(("parallel",)),
        name="s5_prep",
    )(lam_re.reshape(g, p, 1), lam_im.reshape(g, p, 1), log_dt.reshape(g, 1, 1),
      tile16(b_re), tile16(b_im), c_re, c_im,
      tile16(c_re.transpose(0, 2, 1)), tile16(c_im.transpose(0, 2, 1)))


def _s5_assemble(kt, x_re, x_im, w_int):
    t, gb, nb = S5_CHUNK, S5_GB, N_GROUPS // S5_GB
    eye = jnp.eye(gb, dtype=F32)
    ktr = jnp.flip(kt.reshape(nb, gb, SSM_GROUP, t, SSM_GROUP), axis=3)
    w_lag = ktr.transpose(0, 3, 1, 4, 2)[:, :, :, :, None, :] * eye[None, None, :, None, :, None]
    w_lag = w_lag.reshape(nb, t * HEAD_DIM, HEAD_DIM)

    def inj(x):
        xr = jnp.flip(x.reshape(nb, gb, SSM_STATE, t, SSM_GROUP), axis=3)
        w = xr.transpose(0, 3, 1, 4, 2)[:, :, :, :, None, :] * eye[None, None, :, None, :, None]
        return w.reshape(nb, t * HEAD_DIM, gb * SSM_STATE)

    w_inj = jnp.concatenate([inj(x_re), inj(x_im)], axis=2)
    w6 = w_int.reshape(nb, gb, 2, SSM_STATE, t, SSM_GROUP).transpose(0, 2, 1, 3, 4, 5)
    w_car = w6[:, :, :, :, :, None, :] * eye[None, None, :, None, None, :, None]
    w_car = w_car.reshape(nb, 2 * gb * SSM_STATE, t * HEAD_DIM)
    return w_lag.astype(BF16), w_inj.astype(BF16), w_car.astype(BF16)


S5_HW = S5_GB * SSM_STATE


def _s5_block_kernel(x_ref, wlag_ref, winj_ref, wcar_ref, ar_ref, ai_ref, h0_ref, y_ref, hf_ref,
                     xc_ref, s_ref, hp_ref, *, nbk, nc):
    mc = nbk * nc
    t = S5_CHUNK
    for s in range(t):
        xc_ref[:, s * HEAD_DIM:(s + 1) * HEAD_DIM] = x_ref[pl.ds(s, mc, stride=t), :].astype(BF16)
    s_ref[...] = _dot(xc_ref[...], winj_ref[0])
    ar = ar_ref[0]
    ai = ai_ref[0]
    h0 = tuple((h0_ref[0, 0, b:b + 1, 0:S5_HW], h0_ref[0, 0, b:b + 1, S5_HW:2 * S5_HW]) for b in range(nbk))

    def step(c, hs):
        out = []
        for b in range(nbk):
            hr, hi = hs[b]
            row = b * nc + c
            hp_ref[pl.ds(row, 1), 0:S5_HW] = hr
            hp_ref[pl.ds(row, 1), S5_HW:2 * S5_HW] = hi
            nr = ar * hr - ai * hi + s_ref[pl.ds(row, 1), 0:S5_HW]
            ni = ar * hi + ai * hr + s_ref[pl.ds(row, 1), S5_HW:2 * S5_HW]
            out.append((nr, ni))
        return tuple(out)

    hs = lax.fori_loop(0, nc, step, h0)
    for b in range(nbk):
        hf_ref[0, 0, b:b + 1, 0:S5_HW] = hs[b][0]
        hf_ref[0, 0, b:b + 1, S5_HW:2 * S5_HW] = hs[b][1]
    hp = hp_ref[...].astype(BF16)
    for tt in range(t):
        y = (_dot(xc_ref[:, 0:(tt + 1) * HEAD_DIM], wlag_ref[0, (t - 1 - tt) * HEAD_DIM:, :])
             + _dot(hp, wcar_ref[0, :, tt * HEAD_DIM:(tt + 1) * HEAD_DIM]))
        y_ref[pl.ds(tt, mc, stride=t), :] = y


def s5_block(xn, w_lag, w_inj, w_car, a_re, a_im, h0, nbk, nc):
    m = xn.shape[0]
    rows = nbk * nc * S5_CHUNK
    n_step = m // rows
    n_blk = N_GROUPS // S5_GB
    kdim = S5_CHUNK * HEAD_DIM
    wspec = lambda a: pl.BlockSpec((1,) + a.shape[1:], lambda g, b: (g,) + (0,) * (a.ndim - 1))
    hspec = pl.BlockSpec((1, 1, nbk, 2 * S5_HW), lambda g, b: (g, b, 0, 0))
    xspec = pl.BlockSpec((rows, HEAD_DIM), lambda g, b: (b, g))
    return pl.pallas_call(
        functools.partial(_s5_block_kernel, nbk=nbk, nc=nc),
        grid=(n_blk, n_step),
        in_specs=[xspec, wspec(w_lag), wspec(w_inj), wspec(w_car), wspec(a_re), wspec(a_im), hspec],
        out_specs=[xspec, hspec],
        out_shape=[jax.ShapeDtypeStruct((m, D_MODEL), F32), jax.ShapeDtypeStruct(h0.shape, F32)],
        scratch_shapes=[pltpu.VMEM((nbk * nc, kdim), BF16), pltpu.VMEM((nbk * nc, 2 * S5_HW), F32),
                        pltpu.VMEM((nbk * nc, 2 * S5_HW), F32)],
        compiler_params=_cparams(("parallel", "arbitrary")),
        name="s5_block",
    )(xn, w_lag, w_inj, w_car, a_re, a_im, h0)


CMP_PAGES = 16
CMP_ROWS = CMP_PAGES * (PAGE // CMP_STRIDE)
CMP_FLAT = CMP_STRIDE * HEAD_DIM


PAGE_CHUNKS = KV_SLAB // HEAD_DIM
PAGE_ROWS = PAGE * PAGE_CHUNKS


def _page_specs(n, first):
    def spec(k):
        return pl.BlockSpec((1, PAGE_ROWS, HEAD_DIM), lambda b, p, pt: (pt[b, first(p) + k], 0, 0))
    return [spec(k) for k in range(n)]


def _cmp_proj_kernel(pt_ref, *refs):
    x_refs = refs[:CMP_PAGES]
    wk_ref, wv_ref, o_ref, lhs_ref = refs[CMP_PAGES:]
    per_page = PAGE // CMP_STRIDE
    for pg, x_ref in enumerate(x_refs):
        for kv in range(2):
            for g in range(N_KV):
                c = kv * N_KV + g
                for s in range(CMP_STRIDE):
                    piece = x_ref[0, pl.ds(s * PAGE_CHUNKS + c, per_page, stride=CMP_STRIDE * PAGE_CHUNKS), :]
                    lhs_ref[kv, g, pg * per_page:(pg + 1) * per_page, s * HEAD_DIM:(s + 1) * HEAD_DIM] = piece
    for kv, w_ref in ((0, wk_ref), (1, wv_ref)):
        for g in range(N_KV):
            o_ref[0, kv, g] = _dot(lhs_ref[kv, g].astype(BF16), w_ref[...])


def cmp_proj(pages, page_table, wk_cat, wv_cat):
    nb, npg = page_table.shape
    assert npg % CMP_PAGES == 0, "compression consumes whole groups of pages"
    n_ch = npg * (PAGE // CMP_STRIDE)
    wspec = pl.BlockSpec((CMP_FLAT, 2 * CMP_HIDDEN), lambda b, p, pt: (0, 0))
    grid_spec = pltpu.PrefetchScalarGridSpec(
        num_scalar_prefetch=1,
        grid=(nb, npg // CMP_PAGES),
        in_specs=_page_specs(CMP_PAGES, lambda p: p * CMP_PAGES) + [wspec, wspec],
        out_specs=pl.BlockSpec((1, 2, N_KV, CMP_ROWS, 2 * CMP_HIDDEN), lambda b, p, pt: (b, 0, 0, p, 0)),
        scratch_shapes=[pltpu.VMEM((2, N_KV, CMP_ROWS, CMP_FLAT), F32)],
    )
    return pl.pallas_call(
        _cmp_proj_kernel,
        grid_spec=grid_spec,
        out_shape=jax.ShapeDtypeStruct((nb, 2, N_KV, n_ch, 2 * CMP_HIDDEN), F32),
        compiler_params=_cparams(("parallel", "arbitrary")),
        name="cmp_proj",
    )(page_table, *([pages] * CMP_PAGES), wk_cat, wv_cat)


def _cmp_mlp_kernel(p_ref, w1_ref, pe_ref, w2_ref, o_ref):
    proj = p_ref[0, 0, 0]
    n_ch = proj.shape[0]
    pre0 = jnp.sum(pe_ref[0] * w1_ref[0], axis=0, keepdims=True)
    first = proj[:, 0:CMP_HIDDEN]
    second = pltpu.roll(proj[:, CMP_HIDDEN:2 * CMP_HIDDEN], n_ch - 1, 0)
    pre = (pre0 + first) + second
    o_ref[0, 0, 0] = _dot(_gelu(pre).astype(BF16), w2_ref[0])


def cmp_mlp(proj, w1, pe, w2_bf):
    nb, _, _, n_ch, _ = proj.shape
    flat = CMP_BLOCK * HEAD_DIM
    return pl.pallas_call(
        _cmp_mlp_kernel,
        grid=(nb, 2, N_KV),
        in_specs=[pl.BlockSpec((1, 1, 1, n_ch, 2 * CMP_HIDDEN), lambda b, k, g: (b, k, g, 0, 0)),
                  pl.BlockSpec((1, flat, CMP_HIDDEN), lambda b, k, g: (k, 0, 0)),
                  pl.BlockSpec((1, flat, 1), lambda b, k, g: (k, 0, 0)),
                  pl.BlockSpec((1, CMP_HIDDEN, HEAD_DIM), lambda b, k, g: (k, 0, 0))],
        out_specs=pl.BlockSpec((1, 1, 1, n_ch, HEAD_DIM), lambda b, k, g: (b, k, g, 0, 0)),
        out_shape=jax.ShapeDtypeStruct((nb, 2, N_KV, n_ch, HEAD_DIM), F32),
        compiler_params=_cparams(("parallel", "parallel", "parallel")),
        name="cmp_mlp",
    )(proj, w1, pe, w2_bf)


def _kv_pack_kernel(pt_ref, *refs, per, n_groups):
    x_refs = refs[:per]
    t_ref, k_ref, vt_ref = refs[per:]
    j = pl.program_id(1)

    def emit(head):
        for g in range(N_KV):
            for pg in range(per):
                sl = slice(pg * PAGE, (pg + 1) * PAGE)
                k_ref[0, g, sl, :] = head(pg, g).astype(BF16)
                vt_ref[0, g, 0, :, sl] = head(pg, N_KV + g).T.astype(BF16)

    @pl.when(j < n_groups)
    def _():
        emit(lambda pg, c: x_refs[pg][0, pl.ds(c, PAGE, stride=PAGE_CHUNKS), :])

    @pl.when(j >= n_groups)
    def _():
        emit(lambda pg, c: t_ref[0, pl.ds(pg * PAGE_ROWS + c, PAGE, stride=PAGE_CHUNKS), :])


def kv_pack(pages, page_table, tail, vt_tile):
    nb, n_pages = page_table.shape
    per = vt_tile // PAGE
    assert n_pages % per == 0
    n_groups = n_pages // per
    if tail is None:
        n_tail = 0
        tail = jnp.zeros((nb, per * PAGE_ROWS, HEAD_DIM), F32)
    else:
        n_tail = tail.shape[1] // vt_tile
        tail = tail.reshape(nb, n_tail * per * PAGE_ROWS, HEAD_DIM)
    n_tot = n_groups + n_tail
    grid_spec = pltpu.PrefetchScalarGridSpec(
        num_scalar_prefetch=1,
        grid=(nb, n_tot),
        in_specs=_page_specs(per, lambda p: jnp.minimum(p, n_groups - 1) * per)
        + [pl.BlockSpec((1, per * PAGE_ROWS, HEAD_DIM), lambda b, p, pt: (b, jnp.maximum(p - n_groups, 0), 0))],
        out_specs=[pl.BlockSpec((1, N_KV, vt_tile, HEAD_DIM), lambda b, p, pt: (b, 0, p, 0)),
                   pl.BlockSpec((1, N_KV, 1, HEAD_DIM, vt_tile), lambda b, p, pt: (b, 0, p, 0, 0))],
    )
    return pl.pallas_call(
        functools.partial(_kv_pack_kernel, per=per, n_groups=n_groups),
        grid_spec=grid_spec,
        out_shape=[jax.ShapeDtypeStruct((nb, N_KV, n_tot * vt_tile, HEAD_DIM), BF16),
                   jax.ShapeDtypeStruct((nb, N_KV, n_tot, HEAD_DIM, vt_tile), BF16)],
        compiler_params=_cparams(("parallel", "arbitrary")),
        name="kv_pack",
    )(page_table, *([pages] * per), tail)


def _split3(x):
    hi = x.astype(BF16)
    r1 = x - hi.astype(F32)
    mid = r1.astype(BF16)
    lo = (r1 - mid.astype(F32)).astype(BF16)
    return hi, mid, lo


def _softmax_cols(s, col_ok):
    m = jnp.max(s, axis=0, keepdims=True)
    e = jnp.exp(s - m)
    den = jnp.sum(e, axis=0, keepdims=True)
    return e * jnp.where(col_ok, 1.0 / den, 0.0)


def _nsa_kernel(q_ref, qr_ref, gt_ref, kc_ref, vct_ref, ks_ref, vst_ref, kw_ref, vwt_ref, o_ref,
                bias_ref, s_ref, m_ref, l_ref, acc_ref, *, tq, q_off, swa_base):
    rq = GROUP_Q * tq
    w = min(rq, HEAD_DIM)
    reps = rq // w
    n_cp = kc_ref.shape[2]
    n_tiles = bias_ref.shape[0]
    per = SLC_TILE // SLC_BLOCK
    n_blk = n_tiles * per
    tw = kw_ref.shape[2]
    i = pl.program_id(2)
    t0 = q_off + i * tq
    t_w = t0 + lax.broadcasted_iota(jnp.int32, (1, w), 1) % tq

    def tile(x):
        return jnp.concatenate([x] * reps, axis=x.ndim - 1) if reps > 1 else x

    def rows(ref):
        x = ref[...]
        return jnp.concatenate([x[:, r * HEAD_DIM:(r + 1) * HEAD_DIM] for r in range(GROUP_Q)], axis=0)

    q2 = rows(q_ref)
    qr2 = rows(qr_ref)
    t_rq = tile(t_w)

    n_io = lax.broadcasted_iota(jnp.int32, (n_cp, 1), 0)
    bias_c = jnp.where((n_io * CMP_STRIDE + (CMP_BLOCK - 1)) <= t_w, 0.0, NEG)
    p_c = _softmax_cols(_dot_nt(kc_ref[0, 0], q2) + tile(bias_c), t_rq >= CMP_BLOCK - 1)
    o_c = _dot(vct_ref[0, 0], p_c.astype(BF16))
    if reps > 1:
        p_sum = p_c[:, 0:w]
        for r in range(1, reps):
            p_sum = p_sum + p_c[:, r * w:(r + 1) * w]
    else:
        p_sum = p_c
        for r in range(1, GROUP_Q):
            p_sum = p_sum + pltpu.roll(p_c, r * tq, 1)
    s_col = lax.broadcasted_iota(jnp.int32, (n_blk, 1), 0)
    n_row = lax.broadcasted_iota(jnp.int32, (1, n_cp), 1)
    ov = jnp.logical_and(n_row * CMP_STRIDE < (s_col + 1) * SLC_BLOCK,
                         n_row * CMP_STRIDE + CMP_BLOCK > s_col * SLC_BLOCK)
    ov = jnp.where(ov, 1.0, 0.0).astype(BF16)
    hi, mid, lo = _split3(p_sum)
    imp = (_dot(ov, hi) + _dot(ov, mid)) + _dot(ov, lo)
    cur = t_w // SLC_BLOCK
    forced = jnp.logical_or(s_col == 0, jnp.logical_or(s_col == cur, s_col == cur - 1))
    elig = s_col * SLC_BLOCK <= t_w
    imp = jnp.where(forced, FORCE_SCORE, imp)
    imp = jnp.where(elig, imp, -jnp.inf)

    s_colf = s_col.astype(F32)

    def pick(_, carry):
        work, sel = carry
        best = jnp.max(work, axis=0, keepdims=True)
        first = jnp.min(jnp.where(work == best, s_colf, float(n_blk)), axis=0, keepdims=True)
        hit = s_colf == first
        return jnp.where(hit, -jnp.inf, work), jnp.where(hit, 1.0, sel)

    _, sel = lax.fori_loop(0, N_SELECT, pick, (imp, jnp.zeros((n_blk, w), F32)))
    bias_s = jnp.where(jnp.logical_and(elig, sel > 0.5), 0.0, NEG).reshape(n_tiles, per, w)
    bias_s = jnp.concatenate([bias_s, jnp.zeros_like(bias_s)], axis=1).astype(BF16)
    bias_ref[...] = tile(bias_s)

    m_ref[...] = jnp.full_like(m_ref, NEG)
    l_ref[...] = jnp.zeros_like(l_ref)
    acc_ref[...] = jnp.zeros_like(acc_ref)
    qr_t = qr2.astype(F32).T.astype(BF16)
    k_io = lax.broadcasted_iota(jnp.int32, (SLC_TILE, 1), 0)
    c_io = lax.broadcasted_iota(jnp.int32, (1, HEAD_DIM), 1)
    onehot = jnp.where(k_io // SLC_BLOCK == c_io, 1.0, 0.0).astype(BF16)
    zpad = jnp.zeros((HEAD_DIM - 2 * per, rq), BF16)

    def scores(kt):
        base = pl.multiple_of(kt * SLC_TILE, SLC_TILE)
        lhs = jnp.concatenate([ks_ref[0, 0, pl.ds(base, SLC_TILE), :], onehot], axis=1)
        rhs = jnp.concatenate([qr_t, bias_ref[kt], zpad], axis=0)
        return _dot(lhs, rhs)

    def update(kt, ss):
        m_old = m_ref[...]
        m_new = m_old
        for s in ss:
            m_new = jnp.maximum(m_new, jnp.max(s, axis=0, keepdims=True))
        alpha = jnp.exp(m_old - m_new)
        l_new = alpha * l_ref[...]
        acc = alpha * acc_ref[...]
        for j, s in enumerate(ss):
            e = jnp.exp(s - m_new)
            l_new = l_new + jnp.sum(e, axis=0, keepdims=True)
            acc = acc + _dot(vst_ref[0, 0, kt + j], e.astype(BF16))
        l_ref[...] = l_new
        acc_ref[...] = acc
        m_ref[...] = m_new

    k_last = t0 // SLC_TILE
    causal = tile(jnp.where((k_last * SLC_TILE + k_io) <= t_w, 0.0, NEG))
    if rq >= SLC_TILE:
        sa_ref, sb_ref = s_ref.at[0], s_ref.at[1]
        sa_ref[...] = scores(0)

        def body(i, carry):
            kt = 2 * i
            sb_ref[...] = scores(kt + 1)
            update(kt, [sa_ref[...]])
            sa_ref[...] = scores(kt + 2)
            update(kt + 1, [sb_ref[...]])
            return carry

        n_pair = k_last // 2
        lax.fori_loop(0, n_pair, body, 0)

        @pl.when(k_last % 2 == 0)
        def _():
            update(k_last, [sa_ref[...] + causal])

        @pl.when(k_last % 2 == 1)
        def _():
            sb_ref[...] = scores(k_last)
            update(k_last - 1, [sa_ref[...]])
            update(k_last, [sb_ref[...] + causal])
    else:
        joint = SLC_TILE // rq
        n_step = k_last // joint

        def body(i, carry):
            update(i * joint, [scores(i * joint + j) for j in range(joint)])
            return carry

        lax.fori_loop(0, n_step, body, 0)

        def single(kt, carry):
            update(kt, [scores(kt)])
            return carry

        lax.fori_loop(n_step * joint, k_last, single, 0)
        update(k_last, [scores(k_last) + causal])
    o_s = acc_ref[...] * (1.0 / l_ref[...])

    start = jnp.clip(t0 - WINDOW - swa_base, 0, tw - SWA_SPAN)
    start = pl.multiple_of(start, PAGE)
    key_pos = swa_base + start + lax.broadcasted_iota(jnp.int32, (SWA_SPAN, 1), 0)
    dist = t_w - key_pos
    ok_w = jnp.logical_and(jnp.logical_and(dist >= 0, dist < WINDOW), key_pos >= swa_base)
    sw = _dot_nt(kw_ref[0, 0, pl.ds(start, SWA_SPAN), :], qr2) + tile(jnp.where(ok_w, 0.0, NEG))
    p_w = _softmax_cols(sw, True).astype(BF16)
    o_w = jnp.zeros((HEAD_DIM, rq), F32)
    for jt in range(SWA_SPAN // PAGE):
        o_w = o_w + _dot(vwt_ref[0, 0, start // PAGE + jt], p_w[jt * PAGE:(jt + 1) * PAGE, :])

    gt = gt_ref[0, 0, 0]
    o_t = o_c * gt[0:1, :] + o_s * gt[1:2, :] + o_w * gt[2:3, :]
    for c in range(rq // HEAD_DIM):
        blk = o_t[:, c * HEAD_DIM:(c + 1) * HEAD_DIM].T.astype(BF16)
        per_blk = HEAD_DIM // tq
        for rr in range(per_blk):
            r = c * per_blk + rr
            o_ref[:, r * HEAD_DIM:(r + 1) * HEAD_DIM] = blk[rr * tq:(rr + 1) * tq, :]


def nsa_attention(q, qr, gates_t, kc, vct, ks, vst, kw, vwt, *, nb, nq, tq, q_off, swa_base):
    rq = GROUP_Q * tq
    tk = ks.shape[2]
    assert SLC_TILE % tq == 0 and q_off % SLC_TILE == 0 and tk % SLC_TILE == 0
    qspec = pl.BlockSpec((tq, GROUP_W), lambda b, g, i: (b * nq + i, g))
    full = lambda a: pl.BlockSpec((1, 1) + a.shape[2:], lambda b, g, i: (b, g) + (0,) * (a.ndim - 2))
    return pl.pallas_call(
        functools.partial(_nsa_kernel, tq=tq, q_off=q_off, swa_base=swa_base),
        grid=(nb, N_KV, nq),
        in_specs=[qspec, qspec,
                  pl.BlockSpec((1, 1, 1, N_BRANCH, rq), lambda b, g, i: (b, g, i, 0, 0)),
                  full(kc), full(vct), full(ks), full(vst), full(kw), full(vwt)],
        out_specs=qspec,
        out_shape=jax.ShapeDtypeStruct((nb * nq * tq, Q_WIDTH), BF16),
        scratch_shapes=[pltpu.VMEM((tk // SLC_TILE, 2 * (SLC_TILE // SLC_BLOCK), rq), BF16),
                        pltpu.VMEM((2, SLC_TILE, rq), F32),
                        pltpu.VMEM((1, rq), F32), pltpu.VMEM((1, rq), F32), pltpu.VMEM((HEAD_DIM, rq), F32)],
        compiler_params=_cparams(("parallel", "parallel", "arbitrary")),
        name="nsa_attention",
    )(q, qr, gates_t, kc, vct, ks, vst, kw, vwt)


def _s5_layer(x, h0_re, h0_im, lp, nb, seq):
    m = x.shape[0]
    nc = -(-seq // S5_CHUNK)
    pad = nc * S5_CHUNK - seq
    n_blk = N_GROUPS // S5_GB
    xn = rms_norm(x, lp['norm_pre'])
    w_int = lp['w_int']
    if pad == 0:
        nbk, nbp, xs = 1, nb, xn
    else:
        nbp = -(-nb // S5_CHUNK) * S5_CHUNK
        nbk = nbp
        xs = jnp.pad(xn.reshape(nb, seq, D_MODEL), ((0, nbp - nb), (pad, 0), (0, 0))).reshape(-1, D_MODEL)
        w4 = w_int.reshape(N_GROUPS, 2 * SSM_STATE, S5_CHUNK, SSM_GROUP)
        w_int = jnp.pad(w4[:, :, :S5_CHUNK - pad], ((0, 0), (0, 0), (pad, 0), (0, 0))).reshape(w_int.shape)
    n_step = nbp // nbk
    w_lag, w_inj, w_car = _s5_assemble(lp['kt'], lp['x_re'], lp['x_im'], w_int)
    col = (S5_CHUNK - pad - 1) * SSM_GROUP
    a_re = lp['p_re'][:, :, col].reshape(n_blk, 1, S5_HW)
    a_im = lp['p_im'][:, :, col].reshape(n_blk, 1, S5_HW)
    if h0_re is None:
        h0 = jnp.zeros((n_blk, n_step, nbk, 2 * S5_HW), F32)
    else:
        h0 = jnp.concatenate([h0_re.reshape(nb, n_blk, S5_HW), h0_im.reshape(nb, n_blk, S5_HW)], axis=2)
        h0 = jnp.pad(h0, ((0, nbp - nb), (0, 0), (0, 0))).transpose(1, 0, 2).reshape(n_blk, n_step, nbk, 2 * S5_HW)
    y, hf = s5_block(xs, w_lag, w_inj, w_car, a_re, a_im, h0, nbk, nc)
    y = y.reshape(nbp, nc * S5_CHUNK, D_MODEL)[:nb, pad:].reshape(m, D_MODEL)
    hf = hf.reshape(n_blk, nbp, 2, S5_GB, SSM_STATE)[:, :nb].transpose(2, 1, 0, 3, 4).reshape(2, nb, N_GROUPS, SSM_STATE)
    x = glu_tail(x, y, lp['norm_pre'], lp['d'], lp['w_glu'], lp['b_glu'], lp['norm_post'])
    return x, hf[0], hf[1]


def _gates_t(gates, nb, nq, tq):
    g = gates[:, :N_HEADS * N_BRANCH].reshape(nb, nq, tq, N_KV, GROUP_Q, N_BRANCH)
    return g.transpose(0, 3, 1, 5, 4, 2).reshape(nb, N_KV, nq, N_BRANCH, GROUP_Q * tq)


def _pad_rows(a, nb, seq, tq):
    if seq == tq or seq % tq == 0:
        return a
    a = a.reshape(nb, seq, -1)
    return jnp.pad(a, ((0, 0), (0, tq - seq), (0, 0))).reshape(nb * tq, -1)


def _nsa_layer(x, lp, kvs, tables, nb, seq, tq, q_off, swa_base):
    q, qr = q_proj(x, lp['norm_pre'], lp['w_q'], tables)
    gates = gate_proj(x, lp['norm_pre'], lp['w_g'])
    nq = -(-seq // tq)
    o = nsa_attention(_pad_rows(q, nb, seq, tq), _pad_rows(qr, nb, seq, tq),
                      _gates_t(_pad_rows(gates, nb, seq, tq), nb, nq, tq),
                      *kvs, nb=nb, nq=nq, tq=tq, q_off=q_off, swa_base=swa_base)
    if nq * tq != seq:
        o = o.reshape(nb, nq * tq, Q_WIDTH)[:, :seq].reshape(nb * seq, Q_WIDTH)
    return oproj(o, lp['w_o'], lp['norm_post'], x)


def _trunk(x, nb, seq, pos0, h0_re, h0_im, past, prm):
    m = nb * seq
    ssm_re, ssm_im = [], []
    n_a = len(prm['a'])
    for layer in range(n_a):
        x, hr, hi = _s5_layer(x, None if h0_re is None else h0_re[layer],
                              None if h0_im is None else h0_im[layer], prm['a'][layer], nb, seq)
        ssm_re.append(hr)
        ssm_im.append(hi)
        ml = prm['mlp'][layer]
        x = mlp(x, ml['norm_pre'], ml['w_up'], ml['w_down'], ml['norm_post'])

    pos = pos0 + jnp.tile(jnp.arange(seq), nb)
    tables = _rope_tables(pos)
    kv = kv_proj(x, prm['kv_norm'], prm['w_kv'], tables)
    rows_cmp = kv[:, 0:KV_SLAB].reshape(nb, seq, 2, N_KV, HEAD_DIM)
    rows_slc = kv[:, KV_SLAB:2 * KV_SLAB].reshape(nb, seq, 2, N_KV, HEAD_DIM)
    rows_swa = kv[:, 2 * KV_SLAB:3 * KV_SLAB]
    no_tail = None
    if past is None:
        npg = seq // PAGE
        table = jnp.arange(nb * npg, dtype=jnp.int32).reshape(nb, npg)
        pages = lambda j: kv[:, j * KV_SLAB:(j + 1) * KV_SLAB].reshape(nb * npg, PAGE_ROWS, HEAD_DIM)
        cproj = cmp_proj(pages(0), table, prm['cmp_w1k_cat'], prm['cmp_w1v_cat'])
        ks, vst = kv_pack(pages(1), table, no_tail, SLC_TILE)
        kw, vwt = kv_pack(pages(2), table, no_tail, PAGE)
        swa_base = 0
        tq = 128
        swa_buf = rows_swa.reshape(nb, seq, 2, N_KV, HEAD_DIM)[:, seq - WINDOW:]
    else:
        cache_cmp, cache_slc, state_swa, table = past
        n_pool = cache_cmp.shape[0]
        npg = table.shape[1]
        cproj = cmp_proj(cache_cmp.reshape(n_pool, PAGE_ROWS, HEAD_DIM), table, prm['cmp_w1k_cat'], prm['cmp_w1v_cat'])
        t_real = npg * PAGE + seq
        t_pad = -(-t_real // SLC_TILE) * SLC_TILE
        tail = jnp.pad(kv[:, KV_SLAB:2 * KV_SLAB].reshape(nb, seq, KV_SLAB),
                       ((0, 0), (0, t_pad - npg * PAGE - seq), (0, 0)))
        ks, vst = kv_pack(cache_slc.reshape(n_pool, PAGE_ROWS, HEAD_DIM), table, tail, SLC_TILE)
        w_keep = state_swa.shape[1]
        local = jnp.concatenate([state_swa.reshape(nb, w_keep, KV_SLAB), rows_swa.reshape(nb, seq, KV_SLAB)], axis=1)
        swa_buf = local[:, -w_keep:].reshape(nb, w_keep, 2, N_KV, HEAD_DIM)
        tw = -(-max(local.shape[1], SWA_SPAN) // PAGE) * PAGE
        local = jnp.pad(local, ((0, 0), (0, tw - local.shape[1]), (0, 0)))
        lt = jnp.arange(nb * (tw // PAGE), dtype=jnp.int32).reshape(nb, tw // PAGE)
        kw, vwt = kv_pack(local.reshape(nb * (tw // PAGE), PAGE_ROWS, HEAD_DIM), lt, no_tail, PAGE)
        swa_base = pos0 - w_keep
        tq = 32
    cmp_out = cmp_mlp(cproj, prm['cmp_w1'], prm['cmp_pe'], prm['cmp_w2'])
    kc = cmp_out[:, 0].astype(BF16)
    vct = cmp_out[:, 1].transpose(0, 1, 3, 2).astype(BF16)
    kvs = (kc, vct, ks, vst, kw, vwt)

    for j, lp in enumerate(prm['b']):
        x = _nsa_layer(x, lp, kvs, tables, nb, seq, tq, pos0, swa_base)
        ml = prm['mlp'][n_a + j]
        x = mlp(x, ml['norm_pre'], ml['w_up'], ml['w_down'], ml['norm_post'])
    return x, rows_cmp, rows_slc, swa_buf, jnp.stack(ssm_re), jnp.stack(ssm_im)


def kernel(x_prompt, x_sample, cache_kv_cmp, cache_kv_slc, state_kv_swa, state_ssm_re, state_ssm_im, page_table,
           a_norm_pre, a_lam_re, a_lam_im, a_log_dt, a_b_re, a_b_im, a_c_re, a_c_im, a_d, a_w_glu, a_b_glu,
           a_norm_post, kv_norm, w_kv, cmp_w1_k, cmp_pe_k, cmp_w2_k, cmp_w1_v, cmp_pe_v, cmp_w2_v, b_norm_pre,
           b_w_qg, b_w_o, b_norm_post, mlp_norm_pre, mlp_w_up, mlp_w_down, mlp_norm_post):
    n_a = a_norm_pre.shape[0]
    n_b = b_norm_pre.shape[0]
    prm = {'a': [], 'b': [], 'mlp': []}
    for l in range(n_a):
        kt, x_re, x_im, w_int, p_re, p_im = s5_prep(a_lam_re[l], a_lam_im[l], a_log_dt[l], a_b_re[l], a_b_im[l],
                                                    a_c_re[l], a_c_im[l])
        prm['a'].append(dict(norm_pre=a_norm_pre[l], norm_post=a_norm_post[l], d=a_d[l],
                             w_glu=a_w_glu[l].astype(BF16), b_glu=a_b_glu[l],
                             kt=kt, x_re=x_re, x_im=x_im, w_int=w_int, p_re=p_re, p_im=p_im))
    n_gate = N_HEADS * N_BRANCH
    for l in range(n_b):
        w_g = jnp.pad(b_w_qg[l][:, Q_WIDTH:], ((0, 0), (0, HEAD_DIM - n_gate))).astype(BF16)
        prm['b'].append(dict(norm_pre=b_norm_pre[l], norm_post=b_norm_post[l],
                             w_q=b_w_qg[l][:, :Q_WIDTH].astype(BF16), w_g=w_g, w_o=b_w_o[l].astype(BF16)))
    for l in range(n_a + n_b):
        prm['mlp'].append(dict(norm_pre=mlp_norm_pre[l], norm_post=mlp_norm_post[l],
                               w_up=mlp_w_up[l].astype(BF16), w_down=mlp_w_down[l].astype(BF16)))
    prm['kv_norm'] = kv_norm
    prm['w_kv'] = w_kv.astype(BF16)
    r = CMP_BLOCK // CMP_STRIDE
    cat = lambda w: w.reshape(r, CMP_FLAT, CMP_HIDDEN).transpose(1, 0, 2).reshape(CMP_FLAT, r * CMP_HIDDEN).astype(BF16)
    prm['cmp_w1k_cat'] = cat(cmp_w1_k)
    prm['cmp_w1v_cat'] = cat(cmp_w1_v)
    flat = CMP_BLOCK * HEAD_DIM
    prm['cmp_w1'] = jnp.stack([cmp_w1_k.reshape(flat, CMP_HIDDEN), cmp_w1_v.reshape(flat, CMP_HIDDEN)])
    prm['cmp_pe'] = jnp.stack([cmp_pe_k.reshape(flat, 1), cmp_pe_v.reshape(flat, 1)])
    prm['cmp_w2'] = jnp.stack([cmp_w2_k, cmp_w2_v]).astype(BF16)

    bp, sp, _ = x_prompt.shape
    y_p, cmp_p, slc_p, swa_p, re_p, im_p = _trunk(x_prompt.reshape(bp * sp, D_MODEL), bp, sp, 0,
                                                  None, None, None, prm)
    bs, ss, _ = x_sample.shape
    past_len = page_table.shape[1] * PAGE
    y_s, cmp_s, slc_s, swa_s, re_s, im_s = _trunk(x_sample.reshape(bs * ss, D_MODEL), bs, ss, past_len,
                                                  state_ssm_re, state_ssm_im,
                                                  (cache_kv_cmp, cache_kv_slc, state_kv_swa, page_table), prm)
    return (y_p.reshape(bp, sp, D_MODEL), y_s.reshape(bs, ss, D_MODEL), cmp_p, cmp_s, slc_p, slc_s,
            swa_p, swa_s, re_p, im_p, re_s, im_s)
```

```python
import functools
import math

import jax
import jax.numpy as jnp
from jax import lax
from jax.experimental import pallas as pl
from jax.experimental.pallas import tpu as pltpu

F32 = jnp.float32
BF16 = jnp.bfloat16

D_MODEL = 2048
N_HEADS = 16
HEAD_DIM = 128
N_KV = 4
GROUP_Q = N_HEADS // N_KV
N_BRANCH = 3
ROT_DIM = HEAD_DIM // 4
ROPE_THETA = 500000.0
SSM_GROUP = 16
N_GROUPS = D_MODEL // SSM_GROUP
SSM_STATE = 64
S5_CHUNK = 16
S5_CW = S5_CHUNK * SSM_GROUP
D_FF = 4 * D_MODEL
CMP_BLOCK = 32
CMP_STRIDE = 16
CMP_HIDDEN = 2 * HEAD_DIM
SLC_BLOCK = 64
N_SELECT = 16
WINDOW = 512
PAGE = 128
FORCE_SCORE = 1.0e4
EPS = 1e-6
Q_WIDTH = N_HEADS * HEAD_DIM
GROUP_W = GROUP_Q * HEAD_DIM
KV_SLAB = 2 * N_KV * HEAD_DIM
SLC_TILE = 512
SWA_SPAN = WINDOW + PAGE
NEG = -1.0e30
VMEM_LIMIT = 56 * 1024 * 1024


def _cparams(sem):
    return pltpu.CompilerParams(dimension_semantics=sem, vmem_limit_bytes=VMEM_LIMIT)


def _rms(x, g):
    var = jnp.mean(x * x, axis=-1, keepdims=True)
    return x * lax.rsqrt(var + EPS) * g


def _gelu(x):
    return 0.5 * x * (1.0 + jnp.tanh(math.sqrt(2.0 / math.pi) * (x + 0.044715 * (x * x * x))))


def _sigmoid(x):
    return 1.0 / (1.0 + jnp.exp(-x))


def _dot(a, b):
    return jnp.dot(a, b, preferred_element_type=F32)


def _dot_nt(a, b):
    return lax.dot_general(a, b, (((1,), (1,)), ((), ())), preferred_element_type=F32)


def _rope128(x, c, s1, s2):
    return x * c + pltpu.roll(x, HEAD_DIM - ROT_DIM // 2, 1) * s1 + pltpu.roll(x, ROT_DIM // 2, 1) * s2


def _rope_tables(pos):
    half = ROT_DIM // 2
    inv = ROPE_THETA ** (-jnp.arange(half, dtype=F32) / half)
    ang = pos.astype(F32)[:, None] * inv[None, :]
    cos, sin = jnp.cos(ang), jnp.sin(ang)
    n = pos.shape[0]
    rest = HEAD_DIM - ROT_DIM
    c = jnp.concatenate([cos, cos, jnp.ones((n, rest), F32)], axis=1)
    s1 = jnp.concatenate([-sin, jnp.zeros((n, HEAD_DIM - half), F32)], axis=1)
    s2 = jnp.concatenate([jnp.zeros((n, half), F32), sin, jnp.zeros((n, rest), F32)], axis=1)
    return c, s1, s2


def _row_tile(m):
    return 512 if m % 512 == 0 else m


def _kv_proj_kernel(x_ref, g_ref, w_ref, c_ref, s1_ref, s2_ref, o_ref, xn_ref):
    j = pl.program_id(1)

    @pl.when(j == 0)
    def _():
        xn_ref[...] = _rms(x_ref[...], g_ref[...]).astype(BF16)

    acc = _dot(xn_ref[...], w_ref[...])
    is_rope = jnp.logical_or(j == 2, j == 4)

    @pl.when(is_rope)
    def _():
        c, s1, s2 = c_ref[...], s1_ref[...], s2_ref[...]
        for h in range(N_KV):
            sl = slice(h * HEAD_DIM, (h + 1) * HEAD_DIM)
            o_ref[:, sl] = _rope128(acc[:, sl], c, s1, s2)

    @pl.when(jnp.logical_not(is_rope))
    def _():
        o_ref[...] = acc


def kv_proj(x, g, w_bf, tables):
    m = x.shape[0]
    tm = _row_tile(m)
    n = w_bf.shape[1]
    tn = N_KV * HEAD_DIM
    c, s1, s2 = tables
    tab = pl.BlockSpec((tm, HEAD_DIM), lambda i, j: (i, 0))
    return pl.pallas_call(
        _kv_proj_kernel,
        grid=(m // tm, n // tn),
        in_specs=[pl.BlockSpec((tm, D_MODEL), lambda i, j: (i, 0)),
                  pl.BlockSpec((1, D_MODEL), lambda i, j: (0, 0)),
                  pl.BlockSpec((D_MODEL, tn), lambda i, j: (0, j)),
                  tab, tab, tab],
        out_specs=pl.BlockSpec((tm, tn), lambda i, j: (i, j)),
        out_shape=jax.ShapeDtypeStruct((m, n), F32),
        scratch_shapes=[pltpu.VMEM((tm, D_MODEL), BF16)],
        compiler_params=_cparams(("parallel", "arbitrary")),
        name="kv_proj",
    )(x, g.reshape(1, -1), w_bf, c, s1, s2)


def _q_proj_kernel(x_ref, g_ref, w_ref, c_ref, s1_ref, s2_ref, q_ref, qr_ref, xn_ref):
    j = pl.program_id(1)

    @pl.when(j == 0)
    def _():
        xn_ref[...] = _rms(x_ref[...], g_ref[...]).astype(BF16)

    acc = _dot(xn_ref[...], w_ref[...])
    scale = HEAD_DIM ** -0.5
    c, s1, s2 = c_ref[...], s1_ref[...], s2_ref[...]
    q_ref[...] = (acc * scale).astype(BF16)
    for h in range(GROUP_Q):
        sl = slice(h * HEAD_DIM, (h + 1) * HEAD_DIM)
        qr_ref[:, sl] = (_rope128(acc[:, sl], c, s1, s2) * scale).astype(BF16)


def q_proj(x, g, wq_bf, tables):
    m = x.shape[0]
    tm = _row_tile(m)
    tn = GROUP_W
    c, s1, s2 = tables
    tab = pl.BlockSpec((tm, HEAD_DIM), lambda i, j: (i, 0))
    out = jax.ShapeDtypeStruct((m, Q_WIDTH), BF16)
    ospec = pl.BlockSpec((tm, tn), lambda i, j: (i, j))
    return pl.pallas_call(
        _q_proj_kernel,
        grid=(m // tm, Q_WIDTH // tn),
        in_specs=[pl.BlockSpec((tm, D_MODEL), lambda i, j: (i, 0)),
                  pl.BlockSpec((1, D_MODEL), lambda i, j: (0, 0)),
                  pl.BlockSpec((D_MODEL, tn), lambda i, j: (0, j)),
                  tab, tab, tab],
        out_specs=[ospec, ospec],
        out_shape=[out, out],
        scratch_shapes=[pltpu.VMEM((tm, D_MODEL), BF16)],
        compiler_params=_cparams(("parallel", "arbitrary")),
        name="q_proj",
    )(x, g.reshape(1, -1), wq_bf, c, s1, s2)


def _gate_proj_kernel(x_ref, g_ref, w_ref, o_ref):
    xn = _rms(x_ref[...], g_ref[...]).astype(BF16)
    o_ref[...] = _sigmoid(_dot(xn, w_ref[...]))


def gate_proj(x, g, wg_bf):
    m = x.shape[0]
    tm = _row_tile(m)
    n = wg_bf.shape[1]
    return pl.pallas_call(
        _gate_proj_kernel,
        grid=(m // tm,),
        in_specs=[pl.BlockSpec((tm, D_MODEL), lambda i: (i, 0)),
                  pl.BlockSpec((1, D_MODEL), lambda i: (0, 0)),
                  pl.BlockSpec((D_MODEL, n), lambda i: (0, 0))],
        out_specs=pl.BlockSpec((tm, n), lambda i: (i, 0)),
        out_shape=jax.ShapeDtypeStruct((m, n), F32),
        compiler_params=_cparams(("parallel",)),
        name="gate_proj",
    )(x, g.reshape(1, -1), wg_bf)


def _rms_norm_kernel(x_ref, g_ref, o_ref):
    o_ref[...] = _rms(x_ref[...], g_ref[...])


def rms_norm(x, g):
    m = x.shape[0]
    tm = _row_tile(m)
    return pl.pallas_call(
        _rms_norm_kernel,
        grid=(m // tm,),
        in_specs=[pl.BlockSpec((tm, D_MODEL), lambda i: (i, 0)),
                  pl.BlockSpec((1, D_MODEL), lambda i: (0, 0))],
        out_specs=pl.BlockSpec((tm, D_MODEL), lambda i: (i, 0)),
        out_shape=jax.ShapeDtypeStruct((m, D_MODEL), F32),
        compiler_params=_cparams(("parallel",)),
        name="rms_norm",
    )(x, g.reshape(1, -1))


def _mlp_kernel(x_ref, gpre_ref, wup_ref, wdn_ref, gpost_ref, o_ref, xn_ref, acc_ref):
    j = pl.program_id(1)

    @pl.when(j == 0)
    def _():
        xn_ref[...] = _rms(x_ref[...], gpre_ref[...]).astype(BF16)
        acc_ref[...] = jnp.zeros_like(acc_ref)

    h = jnp.maximum(_dot(xn_ref[...], wup_ref[...]), 0.0)
    acc_ref[...] += _dot((h * h).astype(BF16), wdn_ref[...])

    @pl.when(j == pl.num_programs(1) - 1)
    def _():
        o_ref[...] = x_ref[...] + _rms(acc_ref[...], gpost_ref[...])


def mlp(x, gpre, wup_bf, wdn_bf, gpost):
    m = x.shape[0]
    tm = _row_tile(m)
    tf = 512
    return pl.pallas_call(
        _mlp_kernel,
        grid=(m // tm, D_FF // tf),
        in_specs=[pl.BlockSpec((tm, D_MODEL), lambda i, j: (i, 0)),
                  pl.BlockSpec((1, D_MODEL), lambda i, j: (0, 0)),
                  pl.BlockSpec((D_MODEL, tf), lambda i, j: (0, j)),
                  pl.BlockSpec((tf, D_MODEL), lambda i, j: (j, 0)),
                  pl.BlockSpec((1, D_MODEL), lambda i, j: (0, 0))],
        out_specs=pl.BlockSpec((tm, D_MODEL), lambda i, j: (i, 0)),
        out_shape=jax.ShapeDtypeStruct((m, D_MODEL), F32),
        scratch_shapes=[pltpu.VMEM((tm, D_MODEL), BF16), pltpu.VMEM((tm, D_MODEL), F32)],
        compiler_params=_cparams(("parallel", "arbitrary")),
        name="mlp",
    )(x, gpre.reshape(1, -1), wup_bf, wdn_bf, gpost.reshape(1, -1))


def _oproj_kernel(o_ref, w_ref, g_ref, res_ref, out_ref):
    out_ref[...] = res_ref[...] + _rms(_dot(o_ref[...], w_ref[...]), g_ref[...])


def oproj(o_bf, w_bf, g, res):
    m = o_bf.shape[0]
    tm = _row_tile(m)
    return pl.pallas_call(
        _oproj_kernel,
        grid=(m // tm,),
        in_specs=[pl.BlockSpec((tm, Q_WIDTH), lambda i: (i, 0)),
                  pl.BlockSpec((Q_WIDTH, D_MODEL), lambda i: (0, 0)),
                  pl.BlockSpec((1, D_MODEL), lambda i: (0, 0)),
                  pl.BlockSpec((tm, D_MODEL), lambda i: (i, 0))],
        out_specs=pl.BlockSpec((tm, D_MODEL), lambda i: (i, 0)),
        out_shape=jax.ShapeDtypeStruct((m, D_MODEL), F32),
        compiler_params=_cparams(("parallel",)),
        name="oproj",
    )(o_bf, w_bf, g.reshape(1, -1), res)


def _glu_kernel(x_ref, y_ref, gpre_ref, d_ref, w_ref, b_ref, gpost_ref, o_ref):
    x = x_ref[...]
    xn = _rms(x, gpre_ref[...])
    y = _gelu(y_ref[...] + d_ref[...] * xn)
    z = _dot(y.astype(BF16), w_ref[...]) + b_ref[...]
    o_ref[...] = x + _rms(y * _sigmoid(z), gpost_ref[...])


def glu_tail(x, y_ssm, gpre, d_skip, w_bf, b, gpost):
    m = x.shape[0]
    tm = _row_tile(m)
    vec = pl.BlockSpec((1, D_MODEL), lambda i: (0, 0))
    row = pl.BlockSpec((tm, D_MODEL), lambda i: (i, 0))
    return pl.pallas_call(
        _glu_kernel,
        grid=(m // tm,),
        in_specs=[row, row, vec, vec, pl.BlockSpec((D_MODEL, D_MODEL), lambda i: (0, 0)), vec, vec],
        out_specs=row,
        out_shape=jax.ShapeDtypeStruct((m, D_MODEL), F32),
        compiler_params=_cparams(("parallel",)),
        name="glu_tail",
    )(x, y_ssm, gpre.reshape(1, -1), d_skip.reshape(1, -1), w_bf, b.reshape(1, -1), gpost.reshape(1, -1))


S5_GB = 8


def _s5_prep_kernel(lr_ref, li_ref, ldt_ref, br_ref, bi_ref, cr_ref, ci_ref, ctr_ref, cti_ref,
                    kt_ref, xr_ref, xi_ref, wint_ref, pr_ref, pi_ref):
    lr = lr_ref[...]
    li = li_ref[...]
    dt = jnp.exp(ldt_ref[...])
    ldr = lr * dt
    ldi = li * dt
    mag = jnp.exp(ldr)
    a_re = mag * jnp.cos(ldi)
    a_im = mag * jnp.sin(ldi)
    den = lr * lr + li * li
    nr = a_re - 1.0
    f_re = (nr * lr + a_im * li) / den
    f_im = (a_im * lr - nr * li) / den
    br = br_ref[...]
    bi = bi_ref[...]
    bb_re = f_re * br - f_im * bi
    bb_im = f_re * bi + f_im * br
    k = (lax.broadcasted_iota(jnp.int32, (1, 1, S5_CW), 2) // SSM_GROUP).astype(F32)
    mk = jnp.exp(ldr * k)
    ak_re = mk * jnp.cos(ldi * k)
    ak_im = mk * jnp.sin(ldi * k)
    x_re = ak_re * bb_re - ak_im * bb_im
    x_im = ak_re * bb_im + ak_im * bb_re
    xr_ref[...] = x_re
    xi_ref[...] = x_im
    hp = lax.Precision.HIGHEST
    kt_ref[...] = (jnp.einsum('gip,gpn->gin', cr_ref[...], x_re, precision=hp, preferred_element_type=F32)
                   - jnp.einsum('gip,gpn->gin', ci_ref[...], x_im, precision=hp, preferred_element_type=F32))
    p_re = ak_re * a_re - ak_im * a_im
    p_im = ak_re * a_im + ak_im * a_re
    pr_ref[...] = p_re
    pi_ref[...] = p_im
    ctr = ctr_ref[...]
    cti = cti_ref[...]
    wint_ref[:, 0:SSM_STATE, :] = ctr * p_re - cti * p_im
    wint_ref[:, SSM_STATE:2 * SSM_STATE, :] = -(ctr * p_im + cti * p_re)


def s5_prep(lam_re, lam_im, log_dt, b_re, b_im, c_re, c_im):
    g, p = lam_re.shape
    gb = S5_GB
    col = pl.BlockSpec((gb, p, 1), lambda i: (i, 0, 0))
    wide = pl.BlockSpec((gb, p, S5_CW), lambda i: (i, 0, 0))
    cmat = pl.BlockSpec((gb, SSM_GROUP, p), lambda i: (i, 0, 0))
    wide_shape = jax.ShapeDtypeStruct((g, p, S5_CW), F32)
    tile16 = lambda a: jnp.tile(a, (1, 1, S5_CHUNK))
    return pl.pallas_call(
        _s5_prep_kernel,
        grid=(g // gb,),
        in_specs=[col, col, pl.BlockSpec((gb, 1, 1), lambda i: (i, 0, 0)), wide, wide, cmat, cmat, wide, wide],
        out_specs=[pl.BlockSpec((gb, SSM_GROUP, S5_CW), lambda i: (i, 0, 0)), wide, wide,
                   pl.BlockSpec((gb, 2 * p, S5_CW), lambda i: (i, 0, 0)), wide, wide],
        out_shape=[jax.ShapeDtypeStruct((g, SSM_GROUP, S5_CW), F32), wide_shape, wide_shape,
                   jax.ShapeDtypeStruct((g, 2 * p, S5_CW), F32), wide_shape, wide_shape],
        compiler_params=_cparams(("parallel",)),
        name="s5_prep",
    )(lam_re.reshape(g, p, 1), lam_im.reshape(g, p, 1), log_dt.reshape(g, 1, 1),
      tile16(b_re), tile16(b_im), c_re, c_im,
      tile16(c_re.transpose(0, 2, 1)), tile16(c_im.transpose(0, 2, 1)))


def _s5_assemble(kt, x_re, x_im, w_int):
    t, gb, nb = S5_CHUNK, S5_GB, N_GROUPS // S5_GB
    grp = jnp.arange(gb)[:, None]
    own16 = (grp == jnp.arange(HEAD_DIM)[None, :] // SSM_GROUP)[None, None, :, None, :]
    own128 = (grp == jnp.arange(gb * HEAD_DIM)[None, :] // HEAD_DIM)[None, None, :, None, :]
    spread = lambda a, own: jnp.where(own, jnp.tile(a, (1, 1, 1, 1, gb)), 0.0).astype(BF16)
    ktr = jnp.flip(kt.reshape(nb, gb, SSM_GROUP, t, SSM_GROUP), axis=3).transpose(0, 3, 1, 4, 2)
    w_lag = spread(ktr, own16).reshape(nb, t * HEAD_DIM, HEAD_DIM)
    xs = jnp.stack([x_re, x_im], axis=1).reshape(nb, gb, 2, SSM_STATE, t, SSM_GROUP)
    xs = jnp.flip(xs, axis=4).transpose(0, 4, 1, 5, 2, 3).reshape(nb, t, gb, SSM_GROUP, HEAD_DIM)
    w_inj = spread(xs, own128).reshape(nb, t * HEAD_DIM, gb * HEAD_DIM)
    w5 = w_int.reshape(nb, gb, HEAD_DIM, t, SSM_GROUP).transpose(0, 3, 1, 2, 4)
    w_car = spread(w5, own16).reshape(nb, t, gb * HEAD_DIM, HEAD_DIM)
    return w_lag, w_inj, w_car


S5_SW = S5_GB * 2 * SSM_STATE


def _swap_halves(x):
    lane = lax.broadcasted_iota(jnp.int32, (1, x.shape[-1]), 1)
    return jnp.where(lane % (2 * SSM_STATE) < SSM_STATE,
                     pltpu.roll(x, x.shape[-1] - SSM_STATE, x.ndim - 1), pltpu.roll(x, SSM_STATE, x.ndim - 1))


def _s5_block_kernel(x_ref, wlag_ref, winj_ref, wcar_ref, a1_ref, a2_ref, a2s_ref, h0_ref, y_ref, hf_ref,
                     xc_ref, s_ref, ss_ref, hp_ref, *, nbk, nc):
    mc = nbk * nc
    t = S5_CHUNK
    for s in range(t):
        xc_ref[:, s * HEAD_DIM:(s + 1) * HEAD_DIM] = x_ref[pl.ds(s, mc, stride=t), :].astype(BF16)
    inc = _dot(xc_ref[...], winj_ref[0])
    s_ref[...] = inc
    ss_ref[...] = _swap_halves(inc)
    a1, a2, a2s = a1_ref[0], a2_ref[0], a2s_ref[0]
    h0 = tuple((h0_ref[0, 0, b:b + 1, :], _swap_halves(h0_ref[0, 0, b:b + 1, :])) for b in range(nbk))

    def step(c, hs):
        out = []
        for b in range(nbk):
            h, hx = hs[b]
            row = b * nc + c
            hp_ref[pl.ds(row, 1), :] = h
            out.append((a1 * h + a2 * hx + s_ref[pl.ds(row, 1), :],
                        a1 * hx + a2s * h + ss_ref[pl.ds(row, 1), :]))
        return tuple(out)

    hs = lax.fori_loop(0, nc, step, h0)
    for b in range(nbk):
        hf_ref[0, 0, b:b + 1, :] = hs[b][0]
    hp = hp_ref[...].astype(BF16)
    for tt in range(t):
        y = (_dot(xc_ref[:, 0:(tt + 1) * HEAD_DIM], wlag_ref[0, (t - 1 - tt) * HEAD_DIM:, :])
             + _dot(hp, wcar_ref[0, tt]))
        y_ref[pl.ds(tt, mc, stride=t), :] = y


def s5_block(xn, w_lag, w_inj, w_car, a1, a2, a2s, h0, nbk, nc):
    m = xn.shape[0]
    rows = nbk * nc * S5_CHUNK
    n_step = m // rows
    n_blk = N_GROUPS // S5_GB
    kdim = S5_CHUNK * HEAD_DIM
    wspec = lambda a: pl.BlockSpec((1,) + a.shape[1:], lambda g, b: (g,) + (0,) * (a.ndim - 1))
    hspec = pl.BlockSpec((1, 1, nbk, S5_SW), lambda g, b: (g, b, 0, 0))
    xspec = pl.BlockSpec((rows, HEAD_DIM), lambda g, b: (b, g))
    return pl.pallas_call(
        functools.partial(_s5_block_kernel, nbk=nbk, nc=nc),
        grid=(n_blk, n_step),
        in_specs=[xspec, wspec(w_lag), wspec(w_inj), wspec(w_car), wspec(a1), wspec(a2), wspec(a2s), hspec],
        out_specs=[xspec, hspec],
        out_shape=[jax.ShapeDtypeStruct((m, D_MODEL), F32), jax.ShapeDtypeStruct(h0.shape, F32)],
        scratch_shapes=[pltpu.VMEM((nbk * nc, kdim), BF16)] + [pltpu.VMEM((nbk * nc, S5_SW), F32)] * 3,
        compiler_params=_cparams(("parallel", "arbitrary")),
        name="s5_block",
    )(xn, w_lag, w_inj, w_car, a1, a2, a2s, h0)


CMP_PAGES = 16
CMP_ROWS = CMP_PAGES * (PAGE // CMP_STRIDE)
CMP_FLAT = CMP_STRIDE * HEAD_DIM


PAGE_CHUNKS = KV_SLAB // HEAD_DIM
PAGE_ROWS = PAGE * PAGE_CHUNKS


def _page_specs(n, first):
    def spec(k):
        return pl.BlockSpec((1, PAGE_ROWS, HEAD_DIM), lambda b, p, pt: (pt[b, first(p) + k], 0, 0))
    return [spec(k) for k in range(n)]


def _cmp_proj_kernel(pt_ref, *refs):
    x_refs = refs[:CMP_PAGES]
    wk_ref, wv_ref, o_ref, lhs_ref = refs[CMP_PAGES:]
    per_page = PAGE // CMP_STRIDE
    for pg, x_ref in enumerate(x_refs):
        for kv in range(2):
            for g in range(N_KV):
                c = kv * N_KV + g
                for s in range(CMP_STRIDE):
                    piece = x_ref[0, pl.ds(s * PAGE_CHUNKS + c, per_page, stride=CMP_STRIDE * PAGE_CHUNKS), :]
                    lhs_ref[kv, g, pg * per_page:(pg + 1) * per_page, s * HEAD_DIM:(s + 1) * HEAD_DIM] = piece
    for kv, w_ref in ((0, wk_ref), (1, wv_ref)):
        for g in range(N_KV):
            o_ref[0, kv, g] = _dot(lhs_ref[kv, g].astype(BF16), w_ref[...])


def cmp_proj(pages, page_table, wk_cat, wv_cat):
    nb, npg = page_table.shape
    assert npg % CMP_PAGES == 0, "compression consumes whole groups of pages"
    n_ch = npg * (PAGE // CMP_STRIDE)
    wspec = pl.BlockSpec((CMP_FLAT, 2 * CMP_HIDDEN), lambda b, p, pt: (0, 0))
    grid_spec = pltpu.PrefetchScalarGridSpec(
        num_scalar_prefetch=1,
        grid=(nb, npg // CMP_PAGES),
        in_specs=_page_specs(CMP_PAGES, lambda p: p * CMP_PAGES) + [wspec, wspec],
        out_specs=pl.BlockSpec((1, 2, N_KV, CMP_ROWS, 2 * CMP_HIDDEN), lambda b, p, pt: (b, 0, 0, p, 0)),
        scratch_shapes=[pltpu.VMEM((2, N_KV, CMP_ROWS, CMP_FLAT), F32)],
    )
    return pl.pallas_call(
        _cmp_proj_kernel,
        grid_spec=grid_spec,
        out_shape=jax.ShapeDtypeStruct((nb, 2, N_KV, n_ch, 2 * CMP_HIDDEN), F32),
        compiler_params=_cparams(("parallel", "arbitrary")),
        name="cmp_proj",
    )(page_table, *([pages] * CMP_PAGES), wk_cat, wv_cat)


def _cmp_mlp_kernel(p_ref, w1_ref, pe_ref, w2_ref, o_ref):
    proj = p_ref[0, 0, 0]
    n_ch = proj.shape[0]
    pre0 = jnp.sum(pe_ref[0] * w1_ref[0], axis=0, keepdims=True)
    first = proj[:, 0:CMP_HIDDEN]
    second = pltpu.roll(proj[:, CMP_HIDDEN:2 * CMP_HIDDEN], n_ch - 1, 0)
    pre = (pre0 + first) + second
    o_ref[0, 0, 0] = _dot(_gelu(pre).astype(BF16), w2_ref[0])


def cmp_mlp(proj, w1, pe, w2_bf):
    nb, _, _, n_ch, _ = proj.shape
    flat = CMP_BLOCK * HEAD_DIM
    return pl.pallas_call(
        _cmp_mlp_kernel,
        grid=(nb, 2, N_KV),
        in_specs=[pl.BlockSpec((1, 1, 1, n_ch, 2 * CMP_HIDDEN), lambda b, k, g: (b, k, g, 0, 0)),
                  pl.BlockSpec((1, flat, CMP_HIDDEN), lambda b, k, g: (k, 0, 0)),
                  pl.BlockSpec((1, flat, 1), lambda b, k, g: (k, 0, 0)),
                  pl.BlockSpec((1, CMP_HIDDEN, HEAD_DIM), lambda b, k, g: (k, 0, 0))],
        out_specs=pl.BlockSpec((1, 1, 1, n_ch, HEAD_DIM), lambda b, k, g: (b, k, g, 0, 0)),
        out_shape=jax.ShapeDtypeStruct((nb, 2, N_KV, n_ch, HEAD_DIM), F32),
        compiler_params=_cparams(("parallel", "parallel", "parallel")),
        name="cmp_mlp",
    )(proj, w1, pe, w2_bf)


def _kv_pack_kernel(pt_ref, *refs, per, n_groups):
    x_refs = refs[:per]
    t_ref, k_ref, vt_ref = refs[per:]
    j = pl.program_id(1)

    def emit(head):
        for g in range(N_KV):
            for pg in range(per):
                sl = slice(pg * PAGE, (pg + 1) * PAGE)
                k_ref[0, g, sl, :] = head(pg, g).astype(BF16)
                vt_ref[0, g, 0, :, sl] = head(pg, N_KV + g).T.astype(BF16)

    @pl.when(j < n_groups)
    def _():
        emit(lambda pg, c: x_refs[pg][0, pl.ds(c, PAGE, stride=PAGE_CHUNKS), :])

    @pl.when(j >= n_groups)
    def _():
        emit(lambda pg, c: t_ref[0, pl.ds(pg * PAGE_ROWS + c, PAGE, stride=PAGE_CHUNKS), :])


def kv_pack(pages, page_table, tail, vt_tile):
    nb, n_pages = page_table.shape
    per = vt_tile // PAGE
    assert n_pages % per == 0
    n_groups = n_pages // per
    if tail is None:
        n_tail = 0
        tail = jnp.zeros((nb, per * PAGE_ROWS, HEAD_DIM), F32)
    else:
        n_tail = tail.shape[1] // vt_tile
        tail = tail.reshape(nb, n_tail * per * PAGE_ROWS, HEAD_DIM)
    n_tot = n_groups + n_tail
    grid_spec = pltpu.PrefetchScalarGridSpec(
        num_scalar_prefetch=1,
        grid=(nb, n_tot),
        in_specs=_page_specs(per, lambda p: jnp.minimum(p, n_groups - 1) * per)
        + [pl.BlockSpec((1, per * PAGE_ROWS, HEAD_DIM), lambda b, p, pt: (b, jnp.maximum(p - n_groups, 0), 0))],
        out_specs=[pl.BlockSpec((1, N_KV, vt_tile, HEAD_DIM), lambda b, p, pt: (b, 0, p, 0)),
                   pl.BlockSpec((1, N_KV, 1, HEAD_DIM, vt_tile), lambda b, p, pt: (b, 0, p, 0, 0))],
    )
    return pl.pallas_call(
        functools.partial(_kv_pack_kernel, per=per, n_groups=n_groups),
        grid_spec=grid_spec,
        out_shape=[jax.ShapeDtypeStruct((nb, N_KV, n_tot * vt_tile, HEAD_DIM), BF16),
                   jax.ShapeDtypeStruct((nb, N_KV, n_tot, HEAD_DIM, vt_tile), BF16)],
        compiler_params=_cparams(("parallel", "arbitrary")),
        name="kv_pack",
    )(page_table, *([pages] * per), tail)


def _split3(x):
    hi = x.astype(BF16)
    r1 = x - hi.astype(F32)
    mid = r1.astype(BF16)
    lo = (r1 - mid.astype(F32)).astype(BF16)
    return hi, mid, lo


def _softmax_cols(s, col_ok):
    m = jnp.max(s, axis=0, keepdims=True)
    e = jnp.exp(s - m)
    den = jnp.sum(e, axis=0, keepdims=True)
    return e * jnp.where(col_ok, 1.0 / den, 0.0)


def _nsa_kernel(q_ref, qr_ref, gt_ref, kc_ref, vct_ref, ks_ref, vst_ref, kw_ref, vwt_ref, o_ref,
                bias_ref, s_ref, m_ref, l_ref, acc_ref, *, tq, q_off, swa_base):
    rq = GROUP_Q * tq
    w = min(rq, HEAD_DIM)
    reps = rq // w
    n_cp = kc_ref.shape[2]
    n_tiles = bias_ref.shape[0]
    per = SLC_TILE // SLC_BLOCK
    n_blk = n_tiles * per
    tw = kw_ref.shape[2]
    i = pl.program_id(2)
    t0 = q_off + i * tq
    t_w = t0 + lax.broadcasted_iota(jnp.int32, (1, w), 1) % tq

    def tile(x):
        return jnp.concatenate([x] * reps, axis=x.ndim - 1) if reps > 1 else x

    def rows(ref):
        x = ref[...]
        return jnp.concatenate([x[:, r * HEAD_DIM:(r + 1) * HEAD_DIM] for r in range(GROUP_Q)], axis=0)

    q2 = rows(q_ref)
    qr2 = rows(qr_ref)
    t_rq = tile(t_w)

    n_io = lax.broadcasted_iota(jnp.int32, (n_cp, 1), 0)
    bias_c = jnp.where((n_io * CMP_STRIDE + (CMP_BLOCK - 1)) <= t_w, 0.0, NEG)
    p_c = _softmax_cols(_dot_nt(kc_ref[0, 0], q2) + tile(bias_c), t_rq >= CMP_BLOCK - 1)
    o_c = _dot(vct_ref[0, 0], p_c.astype(BF16))
    if reps > 1:
        p_sum = p_c[:, 0:w]
        for r in range(1, reps):
            p_sum = p_sum + p_c[:, r * w:(r + 1) * w]
    else:
        p_sum = p_c
        for r in range(1, GROUP_Q):
            p_sum = p_sum + pltpu.roll(p_c, r * tq, 1)
    s_col = lax.broadcasted_iota(jnp.int32, (n_blk, 1), 0)
    n_row = lax.broadcasted_iota(jnp.int32, (1, n_cp), 1)
    ov = jnp.logical_and(n_row * CMP_STRIDE < (s_col + 1) * SLC_BLOCK,
                         n_row * CMP_STRIDE + CMP_BLOCK > s_col * SLC_BLOCK)
    ov = jnp.where(ov, 1.0, 0.0).astype(BF16)
    hi, mid, lo = _split3(p_sum)
    imp = (_dot(ov, hi) + _dot(ov, mid)) + _dot(ov, lo)
    cur = t_w // SLC_BLOCK
    forced = jnp.logical_or(s_col == 0, jnp.logical_or(s_col == cur, s_col == cur - 1))
    elig = s_col * SLC_BLOCK <= t_w
    imp = jnp.where(forced, FORCE_SCORE, imp)
    imp = jnp.where(elig, imp, -jnp.inf)

    s_colf = s_col.astype(F32)

    def pick(_, carry):
        work, sel = carry
        best = jnp.max(work, axis=0, keepdims=True)
        first = jnp.min(jnp.where(work == best, s_colf, float(n_blk)), axis=0, keepdims=True)
        hit = s_colf == first
        return jnp.where(hit, -jnp.inf, work), jnp.where(hit, 1.0, sel)

    _, sel = lax.fori_loop(0, N_SELECT, pick, (imp, jnp.zeros((n_blk, w), F32)))
    bias_s = jnp.where(jnp.logical_and(elig, sel > 0.5), 0.0, NEG).reshape(n_tiles, per, w)
    bias_s = jnp.concatenate([bias_s, jnp.zeros_like(bias_s)], axis=1).astype(BF16)
    bias_ref[...] = tile(bias_s)

    m_ref[...] = jnp.full_like(m_ref, NEG)
    l_ref[...] = jnp.zeros_like(l_ref)
    acc_ref[...] = jnp.zeros_like(acc_ref)
    qr_t = qr2.astype(F32).T.astype(BF16)
    k_io = lax.broadcasted_iota(jnp.int32, (SLC_TILE, 1), 0)
    c_io = lax.broadcasted_iota(jnp.int32, (1, HEAD_DIM), 1)
    onehot = jnp.where(k_io // SLC_BLOCK == c_io, 1.0, 0.0).astype(BF16)
    zpad = jnp.zeros((HEAD_DIM - 2 * per, rq), BF16)

    def scores(kt):
        base = pl.multiple_of(kt * SLC_TILE, SLC_TILE)
        lhs = jnp.concatenate([ks_ref[0, 0, pl.ds(base, SLC_TILE), :], onehot], axis=1)
        rhs = jnp.concatenate([qr_t, bias_ref[kt], zpad], axis=0)
        return _dot(lhs, rhs)

    def update(kt, ss):
        m_old = m_ref[...]
        m_new = m_old
        for s in ss:
            m_new = jnp.maximum(m_new, jnp.max(s, axis=0, keepdims=True))
        alpha = jnp.exp(m_old - m_new)
        l_new = alpha * l_ref[...]
        acc = alpha * acc_ref[...]
        for j, s in enumerate(ss):
            e = jnp.exp(s - m_new)
            l_new = l_new + jnp.sum(e, axis=0, keepdims=True)
            acc = acc + _dot(vst_ref[0, 0, kt + j], e.astype(BF16))
        l_ref[...] = l_new
        acc_ref[...] = acc
        m_ref[...] = m_new

    k_last = t0 // SLC_TILE
    causal = tile(jnp.where((k_last * SLC_TILE + k_io) <= t_w, 0.0, NEG))
    if rq >= SLC_TILE:
        sa_ref, sb_ref = s_ref.at[0], s_ref.at[1]
        sa_ref[...] = scores(0)

        def body(i, carry):
            kt = 2 * i
            sb_ref[...] = scores(kt + 1)
            update(kt, [sa_ref[...]])
            sa_ref[...] = scores(kt + 2)
            update(kt + 1, [sb_ref[...]])
            return carry

        n_pair = k_last // 2
        lax.fori_loop(0, n_pair, body, 0)

        @pl.when(k_last % 2 == 0)
        def _():
            update(k_last, [sa_ref[...] + causal])

        @pl.when(k_last % 2 == 1)
        def _():
            sb_ref[...] = scores(k_last)
            update(k_last - 1, [sa_ref[...]])
            update(k_last, [sb_ref[...] + causal])
    else:
        joint = SLC_TILE // rq
        n_step = k_last // joint

        def body(i, carry):
            update(i * joint, [scores(i * joint + j) for j in range(joint)])
            return carry

        lax.fori_loop(0, n_step, body, 0)

        def single(kt, carry):
            update(kt, [scores(kt)])
            return carry

        lax.fori_loop(n_step * joint, k_last, single, 0)
        update(k_last, [scores(k_last) + causal])
    o_s = acc_ref[...] * (1.0 / l_ref[...])

    start = jnp.clip(t0 - WINDOW - swa_base, 0, tw - SWA_SPAN)
    start = pl.multiple_of(start, PAGE)
    key_pos = swa_base + start + lax.broadcasted_iota(jnp.int32, (SWA_SPAN, 1), 0)
    dist = t_w - key_pos
    ok_w = jnp.logical_and(jnp.logical_and(dist >= 0, dist < WINDOW), key_pos >= swa_base)
    sw = _dot_nt(kw_ref[0, 0, pl.ds(start, SWA_SPAN), :], qr2) + tile(jnp.where(ok_w, 0.0, NEG))
    p_w = _softmax_cols(sw, True).astype(BF16)
    o_w = jnp.zeros((HEAD_DIM, rq), F32)
    for jt in range(SWA_SPAN // PAGE):
        o_w = o_w + _dot(vwt_ref[0, 0, start // PAGE + jt], p_w[jt * PAGE:(jt + 1) * PAGE, :])

    gt = gt_ref[0, 0, 0]
    o_t = o_c * gt[0:1, :] + o_s * gt[1:2, :] + o_w * gt[2:3, :]
    for c in range(rq // HEAD_DIM):
        blk = o_t[:, c * HEAD_DIM:(c + 1) * HEAD_DIM].T.astype(BF16)
        per_blk = HEAD_DIM // tq
        for rr in range(per_blk):
            r = c * per_blk + rr
            o_ref[:, r * HEAD_DIM:(r + 1) * HEAD_DIM] = blk[rr * tq:(rr + 1) * tq, :]


def nsa_attention(q, qr, gates_t, kc, vct, ks, vst, kw, vwt, *, nb, nq, tq, q_off, swa_base):
    rq = GROUP_Q * tq
    tk = ks.shape[2]
    assert SLC_TILE % tq == 0 and q_off % SLC_TILE == 0 and tk % SLC_TILE == 0
    qspec = pl.BlockSpec((tq, GROUP_W), lambda b, g, i: (b * nq + i, g))
    full = lambda a: pl.BlockSpec((1, 1) + a.shape[2:], lambda b, g, i: (b, g) + (0,) * (a.ndim - 2))
    return pl.pallas_call(
        functools.partial(_nsa_kernel, tq=tq, q_off=q_off, swa_base=swa_base),
        grid=(nb, N_KV, nq),
        in_specs=[qspec, qspec,
                  pl.BlockSpec((1, 1, 1, N_BRANCH, rq), lambda b, g, i: (b, g, i, 0, 0)),
                  full(kc), full(vct), full(ks), full(vst), full(kw), full(vwt)],
        out_specs=qspec,
        out_shape=jax.ShapeDtypeStruct((nb * nq * tq, Q_WIDTH), BF16),
        scratch_shapes=[pltpu.VMEM((tk // SLC_TILE, 2 * (SLC_TILE // SLC_BLOCK), rq), BF16),
                        pltpu.VMEM((2, SLC_TILE, rq), F32),
                        pltpu.VMEM((1, rq), F32), pltpu.VMEM((1, rq), F32), pltpu.VMEM((HEAD_DIM, rq), F32)],
        compiler_params=_cparams(("parallel", "parallel", "arbitrary")),
        name="nsa_attention",
    )(q, qr, gates_t, kc, vct, ks, vst, kw, vwt)


def _s5_layer(x, h0_re, h0_im, lp, nb, seq):
    m = x.shape[0]
    nc = -(-seq // S5_CHUNK)
    pad = nc * S5_CHUNK - seq
    n_blk = N_GROUPS // S5_GB
    xn = rms_norm(x, lp['norm_pre'])
    w_int = lp['w_int']
    if pad == 0:
        nbk, nbp, xs = 1, nb, xn
    else:
        nbp = -(-nb // S5_CHUNK) * S5_CHUNK
        nbk = nbp
        xs = jnp.pad(xn.reshape(nb, seq, D_MODEL), ((0, nbp - nb), (pad, 0), (0, 0))).reshape(-1, D_MODEL)
        w4 = w_int.reshape(N_GROUPS, 2 * SSM_STATE, S5_CHUNK, SSM_GROUP)
        w_int = jnp.pad(w4[:, :, :S5_CHUNK - pad], ((0, 0), (0, 0), (pad, 0), (0, 0))).reshape(w_int.shape)
    n_step = nbp // nbk
    w_lag, w_inj, w_car = _s5_assemble(lp['kt'], lp['x_re'], lp['x_im'], w_int)
    col = (S5_CHUNK - pad - 1) * SSM_GROUP
    a_re = lp['p_re'][:, :, col]
    a_im = lp['p_im'][:, :, col]
    halves = lambda u, v: jnp.concatenate([u, v], axis=1).reshape(n_blk, 1, S5_SW)
    a1, a2, a2s = halves(a_re, a_re), halves(-a_im, a_im), halves(a_im, -a_im)
    if h0_re is None:
        h0 = jnp.zeros((n_blk, n_step, nbk, S5_SW), F32)
    else:
        h0 = jnp.stack([h0_re, h0_im], axis=2).reshape(nb, n_blk, S5_SW)
        h0 = jnp.pad(h0, ((0, nbp - nb), (0, 0), (0, 0))).transpose(1, 0, 2).reshape(n_blk, n_step, nbk, S5_SW)
    y, hf = s5_block(xs, w_lag, w_inj, w_car, a1, a2, a2s, h0, nbk, nc)
    y = y.reshape(nbp, nc * S5_CHUNK, D_MODEL)[:nb, pad:].reshape(m, D_MODEL)
    hf = hf.reshape(n_blk, nbp, S5_GB, 2, SSM_STATE)[:, :nb].transpose(3, 1, 0, 2, 4).reshape(2, nb, N_GROUPS, SSM_STATE)
    x = glu_tail(x, y, lp['norm_pre'], lp['d'], lp['w_glu'], lp['b_glu'], lp['norm_post'])
    return x, hf[0], hf[1]


def _gates_t(gates, nb, nq, tq):
    g = gates[:, :N_HEADS * N_BRANCH].reshape(nb, nq, tq, N_KV, GROUP_Q, N_BRANCH)
    return g.transpose(0, 3, 1, 5, 4, 2).reshape(nb, N_KV, nq, N_BRANCH, GROUP_Q * tq)


def _pad_rows(a, nb, seq, tq):
    if seq == tq or seq % tq == 0:
        return a
    a = a.reshape(nb, seq, -1)
    return jnp.pad(a, ((0, 0), (0, tq - seq), (0, 0))).reshape(nb * tq, -1)


def _nsa_layer(x, lp, kvs, tables, nb, seq, tq, q_off, swa_base):
    q, qr = q_proj(x, lp['norm_pre'], lp['w_q'], tables)
    gates = gate_proj(x, lp['norm_pre'], lp['w_g'])
    nq = -(-seq // tq)
    o = nsa_attention(_pad_rows(q, nb, seq, tq), _pad_rows(qr, nb, seq, tq),
                      _gates_t(_pad_rows(gates, nb, seq, tq), nb, nq, tq),
                      *kvs, nb=nb, nq=nq, tq=tq, q_off=q_off, swa_base=swa_base)
    if nq * tq != seq:
        o = o.reshape(nb, nq * tq, Q_WIDTH)[:, :seq].reshape(nb * seq, Q_WIDTH)
    return oproj(o, lp['w_o'], lp['norm_post'], x)


def _trunk(x, nb, seq, pos0, h0_re, h0_im, past, prm):
    m = nb * seq
    ssm_re, ssm_im = [], []
    n_a = len(prm['a'])
    for layer in range(n_a):
        x, hr, hi = _s5_layer(x, None if h0_re is None else h0_re[layer],
                              None if h0_im is None else h0_im[layer], prm['a'][layer], nb, seq)
        ssm_re.append(hr)
        ssm_im.append(hi)
        ml = prm['mlp'][layer]
        x = mlp(x, ml['norm_pre'], ml['w_up'], ml['w_down'], ml['norm_post'])

    pos = pos0 + jnp.tile(jnp.arange(seq), nb)
    tables = _rope_tables(pos)
    kv = kv_proj(x, prm['kv_norm'], prm['w_kv'], tables)
    rows_cmp = kv[:, 0:KV_SLAB].reshape(nb, seq, 2, N_KV, HEAD_DIM)
    rows_slc = kv[:, KV_SLAB:2 * KV_SLAB].reshape(nb, seq, 2, N_KV, HEAD_DIM)
    rows_swa = kv[:, 2 * KV_SLAB:3 * KV_SLAB]
    no_tail = None
    if past is None:
        npg = seq // PAGE
        table = jnp.arange(nb * npg, dtype=jnp.int32).reshape(nb, npg)
        pages = lambda j: kv[:, j * KV_SLAB:(j + 1) * KV_SLAB].reshape(nb * npg, PAGE_ROWS, HEAD_DIM)
        cproj = cmp_proj(pages(0), table, prm['cmp_w1k_cat'], prm['cmp_w1v_cat'])
        ks, vst = kv_pack(pages(1), table, no_tail, SLC_TILE)
        kw, vwt = kv_pack(pages(2), table, no_tail, PAGE)
        swa_base = 0
        tq = 128
        swa_buf = rows_swa.reshape(nb, seq, 2, N_KV, HEAD_DIM)[:, seq - WINDOW:]
    else:
        cache_cmp, cache_slc, state_swa, table = past
        n_pool = cache_cmp.shape[0]
        npg = table.shape[1]
        cproj = cmp_proj(cache_cmp.reshape(n_pool, PAGE_ROWS, HEAD_DIM), table, prm['cmp_w1k_cat'], prm['cmp_w1v_cat'])
        t_real = npg * PAGE + seq
        t_pad = -(-t_real // SLC_TILE) * SLC_TILE
        tail = jnp.pad(kv[:, KV_SLAB:2 * KV_SLAB].reshape(nb, seq, KV_SLAB),
                       ((0, 0), (0, t_pad - npg * PAGE - seq), (0, 0)))
        ks, vst = kv_pack(cache_slc.reshape(n_pool, PAGE_ROWS, HEAD_DIM), table, tail, SLC_TILE)
        w_keep = state_swa.shape[1]
        local = jnp.concatenate([state_swa.reshape(nb, w_keep, KV_SLAB), rows_swa.reshape(nb, seq, KV_SLAB)], axis=1)
        swa_buf = local[:, -w_keep:].reshape(nb, w_keep, 2, N_KV, HEAD_DIM)
        tw = -(-max(local.shape[1], SWA_SPAN) // PAGE) * PAGE
        local = jnp.pad(local, ((0, 0), (0, tw - local.shape[1]), (0, 0)))
        lt = jnp.arange(nb * (tw // PAGE), dtype=jnp.int32).reshape(nb, tw // PAGE)
        kw, vwt = kv_pack(local.reshape(nb * (tw // PAGE), PAGE_ROWS, HEAD_DIM), lt, no_tail, PAGE)
        swa_base = pos0 - w_keep
        tq = 32
    cmp_out = cmp_mlp(cproj, prm['cmp_w1'], prm['cmp_pe'], prm['cmp_w2'])
    kc = cmp_out[:, 0].astype(BF16)
    vct = cmp_out[:, 1].transpose(0, 1, 3, 2).astype(BF16)
    kvs = (kc, vct, ks, vst, kw, vwt)

    for j, lp in enumerate(prm['b']):
        x = _nsa_layer(x, lp, kvs, tables, nb, seq, tq, pos0, swa_base)
        ml = prm['mlp'][n_a + j]
        x = mlp(x, ml['norm_pre'], ml['w_up'], ml['w_down'], ml['norm_post'])
    return x, rows_cmp, rows_slc, swa_buf, jnp.stack(ssm_re), jnp.stack(ssm_im)


def kernel(x_prompt, x_sample, cache_kv_cmp, cache_kv_slc, state_kv_swa, state_ssm_re, state_ssm_im, page_table,
           a_norm_pre, a_lam_re, a_lam_im, a_log_dt, a_b_re, a_b_im, a_c_re, a_c_im, a_d, a_w_glu, a_b_glu,
           a_norm_post, kv_norm, w_kv, cmp_w1_k, cmp_pe_k, cmp_w2_k, cmp_w1_v, cmp_pe_v, cmp_w2_v, b_norm_pre,
           b_w_qg, b_w_o, b_norm_post, mlp_norm_pre, mlp_w_up, mlp_w_down, mlp_norm_post):
    n_a = a_norm_pre.shape[0]
    n_b = b_norm_pre.shape[0]
    prm = {'a': [], 'b': [], 'mlp': []}
    for l in range(n_a):
        kt, x_re, x_im, w_int, p_re, p_im = s5_prep(a_lam_re[l], a_lam_im[l], a_log_dt[l], a_b_re[l], a_b_im[l],
                                                    a_c_re[l], a_c_im[l])
        prm['a'].append(dict(norm_pre=a_norm_pre[l], norm_post=a_norm_post[l], d=a_d[l],
                             w_glu=a_w_glu[l].astype(BF16), b_glu=a_b_glu[l],
                             kt=kt, x_re=x_re, x_im=x_im, w_int=w_int, p_re=p_re, p_im=p_im))
    n_gate = N_HEADS * N_BRANCH
    for l in range(n_b):
        w_g = jnp.pad(b_w_qg[l][:, Q_WIDTH:], ((0, 0), (0, HEAD_DIM - n_gate))).astype(BF16)
        prm['b'].append(dict(norm_pre=b_norm_pre[l], norm_post=b_norm_post[l],
                             w_q=b_w_qg[l][:, :Q_WIDTH].astype(BF16), w_g=w_g, w_o=b_w_o[l].astype(BF16)))
    for l in range(n_a + n_b):
        prm['mlp'].append(dict(norm_pre=mlp_norm_pre[l], norm_post=mlp_norm_post[l],
                               w_up=mlp_w_up[l].astype(BF16), w_down=mlp_w_down[l].astype(BF16)))
    prm['kv_norm'] = kv_norm
    prm['w_kv'] = w_kv.astype(BF16)
    r = CMP_BLOCK // CMP_STRIDE
    cat = lambda w: w.reshape(r, CMP_FLAT, CMP_HIDDEN).transpose(1, 0, 2).reshape(CMP_FLAT, r * CMP_HIDDEN).astype(BF16)
    prm['cmp_w1k_cat'] = cat(cmp_w1_k)
    prm['cmp_w1v_cat'] = cat(cmp_w1_v)
    flat = CMP_BLOCK * HEAD_DIM
    prm['cmp_w1'] = jnp.stack([cmp_w1_k.reshape(flat, CMP_HIDDEN), cmp_w1_v.reshape(flat, CMP_HIDDEN)])
    prm['cmp_pe'] = jnp.stack([cmp_pe_k.reshape(flat, 1), cmp_pe_v.reshape(flat, 1)])
    prm['cmp_w2'] = jnp.stack([cmp_w2_k, cmp_w2_v]).astype(BF16)

    bp, sp, _ = x_prompt.shape
    y_p, cmp_p, slc_p, swa_p, re_p, im_p = _trunk(x_prompt.reshape(bp * sp, D_MODEL), bp, sp, 0,
                                                  None, None, None, prm)
    bs, ss, _ = x_sample.shape
    past_len = page_table.shape[1] * PAGE
    y_s, cmp_s, slc_s, swa_s, re_s, im_s = _trunk(x_sample.reshape(bs * ss, D_MODEL), bs, ss, past_len,
                                                  state_ssm_re, state_ssm_im,
                                                  (cache_kv_cmp, cache_kv_slc, state_kv_swa, page_table), prm)
    return (y_p.reshape(bp, sp, D_MODEL), y_s.reshape(bs, ss, D_MODEL), cmp_p, cmp_s, slc_p, slc_s,
            swa_p, swa_s, re_p, im_p, re_s, im_s)
```

```python
import functools
import math

import jax
import jax.numpy as jnp
from jax import lax
from jax.experimental import pallas as pl
from jax.experimental.pallas import tpu as pltpu

F32 = jnp.float32
BF16 = jnp.bfloat16

D_MODEL = 2048
N_HEADS = 16
HEAD_DIM = 128
N_KV = 4
GROUP_Q = N_HEADS // N_KV
N_BRANCH = 3
ROT_DIM = HEAD_DIM // 4
ROPE_THETA = 500000.0
SSM_GROUP = 16
N_GROUPS = D_MODEL // SSM_GROUP
SSM_STATE = 64
S5_CHUNK = 16
D_FF = 4 * D_MODEL
CMP_BLOCK = 32
CMP_STRIDE = 16
CMP_HIDDEN = 2 * HEAD_DIM
SLC_BLOCK = 64
N_SELECT = 16
WINDOW = 512
PAGE = 128
FORCE_SCORE = 1.0e4
EPS = 1e-6
Q_WIDTH = N_HEADS * HEAD_DIM
GROUP_W = GROUP_Q * HEAD_DIM
KV_SLAB = 2 * N_KV * HEAD_DIM
SLC_TILE = 512
SWA_SPAN = WINDOW + PAGE
NEG = -1.0e30
VMEM_LIMIT = 56 * 1024 * 1024


def _cparams(sem):
    return pltpu.CompilerParams(dimension_semantics=sem, vmem_limit_bytes=VMEM_LIMIT)


def _rms(x, g):
    var = jnp.mean(x * x, axis=-1, keepdims=True)
    return x * lax.rsqrt(var + EPS) * g


def _gelu(x):
    return 0.5 * x * (1.0 + jnp.tanh(math.sqrt(2.0 / math.pi) * (x + 0.044715 * (x * x * x))))


def _sigmoid(x):
    return 1.0 / (1.0 + jnp.exp(-x))


def _dot(a, b):
    return jnp.dot(a, b, preferred_element_type=F32)


def _dot_nt(a, b):
    return lax.dot_general(a, b, (((1,), (1,)), ((), ())), preferred_element_type=F32)


def _rope128(x, c, s1, s2):
    return x * c + pltpu.roll(x, HEAD_DIM - ROT_DIM // 2, 1) * s1 + pltpu.roll(x, ROT_DIM // 2, 1) * s2


def _rope_tables(pos):
    half = ROT_DIM // 2
    inv = ROPE_THETA ** (-jnp.arange(half, dtype=F32) / half)
    ang = pos.astype(F32)[:, None] * inv[None, :]
    cos, sin = jnp.cos(ang), jnp.sin(ang)
    n = pos.shape[0]
    rest = HEAD_DIM - ROT_DIM
    c = jnp.concatenate([cos, cos, jnp.ones((n, rest), F32)], axis=1)
    s1 = jnp.concatenate([-sin, jnp.zeros((n, HEAD_DIM - half), F32)], axis=1)
    s2 = jnp.concatenate([jnp.zeros((n, half), F32), sin, jnp.zeros((n, rest), F32)], axis=1)
    return c, s1, s2


def _row_tile(m):
    return 512 if m % 512 == 0 else m


def _kv_proj_kernel(x_ref, g_ref, w_ref, c_ref, s1_ref, s2_ref, o_ref, xn_ref):
    j = pl.program_id(1)

    @pl.when(j == 0)
    def _():
        xn_ref[...] = _rms(x_ref[...], g_ref[...]).astype(BF16)

    acc = _dot(xn_ref[...], w_ref[...])
    is_rope = jnp.logical_or(j == 2, j == 4)

    @pl.when(is_rope)
    def _():
        c, s1, s2 = c_ref[...], s1_ref[...], s2_ref[...]
        for h in range(N_KV):
            sl = slice(h * HEAD_DIM, (h + 1) * HEAD_DIM)
            o_ref[:, sl] = _rope128(acc[:, sl], c, s1, s2)

    @pl.when(jnp.logical_not(is_rope))
    def _():
        o_ref[...] = acc


def kv_proj(x, g, w_bf, tables):
    m = x.shape[0]
    tm = _row_tile(m)
    n = w_bf.shape[1]
    tn = N_KV * HEAD_DIM
    c, s1, s2 = tables
    tab = pl.BlockSpec((tm, HEAD_DIM), lambda i, j: (i, 0))
    return pl.pallas_call(
        _kv_proj_kernel,
        grid=(m // tm, n // tn),
        in_specs=[pl.BlockSpec((tm, D_MODEL), lambda i, j: (i, 0)),
                  pl.BlockSpec((1, D_MODEL), lambda i, j: (0, 0)),
                  pl.BlockSpec((D_MODEL, tn), lambda i, j: (0, j)),
                  tab, tab, tab],
        out_specs=pl.BlockSpec((tm, tn), lambda i, j: (i, j)),
        out_shape=jax.ShapeDtypeStruct((m, n), F32),
        scratch_shapes=[pltpu.VMEM((tm, D_MODEL), BF16)],
        compiler_params=_cparams(("parallel", "arbitrary")),
        name="kv_proj",
    )(x, g.reshape(1, -1), w_bf, c, s1, s2)


def _q_proj_kernel(x_ref, g_ref, w_ref, c_ref, s1_ref, s2_ref, q_ref, qr_ref, xn_ref):
    j = pl.program_id(1)

    @pl.when(j == 0)
    def _():
        xn_ref[...] = _rms(x_ref[...], g_ref[...]).astype(BF16)

    acc = _dot(xn_ref[...], w_ref[...])
    scale = HEAD_DIM ** -0.5
    c, s1, s2 = c_ref[...], s1_ref[...], s2_ref[...]
    q_ref[...] = (acc * scale).astype(BF16)
    for h in range(GROUP_Q):
        sl = slice(h * HEAD_DIM, (h + 1) * HEAD_DIM)
        qr_ref[:, sl] = (_rope128(acc[:, sl], c, s1, s2) * scale).astype(BF16)


def q_proj(x, g, wq_bf, tables):
    m = x.shape[0]
    tm = _row_tile(m)
    tn = GROUP_W
    c, s1, s2 = tables
    tab = pl.BlockSpec((tm, HEAD_DIM), lambda i, j: (i, 0))
    out = jax.ShapeDtypeStruct((m, Q_WIDTH), BF16)
    ospec = pl.BlockSpec((tm, tn), lambda i, j: (i, j))
    return pl.pallas_call(
        _q_proj_kernel,
        grid=(m // tm, Q_WIDTH // tn),
        in_specs=[pl.BlockSpec((tm, D_MODEL), lambda i, j: (i, 0)),
                  pl.BlockSpec((1, D_MODEL), lambda i, j: (0, 0)),
                  pl.BlockSpec((D_MODEL, tn), lambda i, j: (0, j)),
                  tab, tab, tab],
        out_specs=[ospec, ospec],
        out_shape=[out, out],
        scratch_shapes=[pltpu.VMEM((tm, D_MODEL), BF16)],
        compiler_params=_cparams(("parallel", "arbitrary")),
        name="q_proj",
    )(x, g.reshape(1, -1), wq_bf, c, s1, s2)


def _gate_proj_kernel(x_ref, g_ref, w_ref, o_ref):
    xn = _rms(x_ref[...], g_ref[...]).astype(BF16)
    o_ref[...] = _sigmoid(_dot(xn, w_ref[...]))


def gate_proj(x, g, wg_bf):
    m = x.shape[0]
    tm = _row_tile(m)
    n = wg_bf.shape[1]
    return pl.pallas_call(
        _gate_proj_kernel,
        grid=(m // tm,),
        in_specs=[pl.BlockSpec((tm, D_MODEL), lambda i: (i, 0)),
                  pl.BlockSpec((1, D_MODEL), lambda i: (0, 0)),
                  pl.BlockSpec((D_MODEL, n), lambda i: (0, 0))],
        out_specs=pl.BlockSpec((tm, n), lambda i: (i, 0)),
        out_shape=jax.ShapeDtypeStruct((m, n), F32),
        compiler_params=_cparams(("parallel",)),
        name="gate_proj",
    )(x, g.reshape(1, -1), wg_bf)


def _rms_norm_kernel(x_ref, g_ref, o_ref):
    o_ref[...] = _rms(x_ref[...], g_ref[...])


def rms_norm(x, g):
    m = x.shape[0]
    tm = _row_tile(m)
    return pl.pallas_call(
        _rms_norm_kernel,
        grid=(m // tm,),
        in_specs=[pl.BlockSpec((tm, D_MODEL), lambda i: (i, 0)),
                  pl.BlockSpec((1, D_MODEL), lambda i: (0, 0))],
        out_specs=pl.BlockSpec((tm, D_MODEL), lambda i: (i, 0)),
        out_shape=jax.ShapeDtypeStruct((m, D_MODEL), F32),
        compiler_params=_cparams(("parallel",)),
        name="rms_norm",
    )(x, g.reshape(1, -1))


def _mlp_kernel(x_ref, gpre_ref, wup_ref, wdn_ref, gpost_ref, o_ref, xn_ref, acc_ref):
    j = pl.program_id(1)

    @pl.when(j == 0)
    def _():
        xn_ref[...] = _rms(x_ref[...], gpre_ref[...]).astype(BF16)
        acc_ref[...] = jnp.zeros_like(acc_ref)

    h = jnp.maximum(_dot(xn_ref[...], wup_ref[...]), 0.0)
    acc_ref[...] += _dot((h * h).astype(BF16), wdn_ref[...])

    @pl.when(j == pl.num_programs(1) - 1)
    def _():
        o_ref[...] = x_ref[...] + _rms(acc_ref[...], gpost_ref[...])


def mlp(x, gpre, wup_bf, wdn_bf, gpost):
    m = x.shape[0]
    tm = _row_tile(m)
    tf = 512
    return pl.pallas_call(
        _mlp_kernel,
        grid=(m // tm, D_FF // tf),
        in_specs=[pl.BlockSpec((tm, D_MODEL), lambda i, j: (i, 0)),
                  pl.BlockSpec((1, D_MODEL), lambda i, j: (0, 0)),
                  pl.BlockSpec((D_MODEL, tf), lambda i, j: (0, j)),
                  pl.BlockSpec((tf, D_MODEL), lambda i, j: (j, 0)),
                  pl.BlockSpec((1, D_MODEL), lambda i, j: (0, 0))],
        out_specs=pl.BlockSpec((tm, D_MODEL), lambda i, j: (i, 0)),
        out_shape=jax.ShapeDtypeStruct((m, D_MODEL), F32),
        scratch_shapes=[pltpu.VMEM((tm, D_MODEL), BF16), pltpu.VMEM((tm, D_MODEL), F32)],
        compiler_params=_cparams(("parallel", "arbitrary")),
        name="mlp",
    )(x, gpre.reshape(1, -1), wup_bf, wdn_bf, gpost.reshape(1, -1))


def _oproj_kernel(o_ref, w_ref, g_ref, res_ref, out_ref):
    out_ref[...] = res_ref[...] + _rms(_dot(o_ref[...], w_ref[...]), g_ref[...])


def oproj(o_bf, w_bf, g, res):
    m = o_bf.shape[0]
    tm = _row_tile(m)
    return pl.pallas_call(
        _oproj_kernel,
        grid=(m // tm,),
        in_specs=[pl.BlockSpec((tm, Q_WIDTH), lambda i: (i, 0)),
                  pl.BlockSpec((Q_WIDTH, D_MODEL), lambda i: (0, 0)),
                  pl.BlockSpec((1, D_MODEL), lambda i: (0, 0)),
                  pl.BlockSpec((tm, D_MODEL), lambda i: (i, 0))],
        out_specs=pl.BlockSpec((tm, D_MODEL), lambda i: (i, 0)),
        out_shape=jax.ShapeDtypeStruct((m, D_MODEL), F32),
        compiler_params=_cparams(("parallel",)),
        name="oproj",
    )(o_bf, w_bf, g.reshape(1, -1), res)


def _glu_kernel(x_ref, y_ref, gpre_ref, d_ref, w_ref, b_ref, gpost_ref, o_ref):
    x = x_ref[...]
    xn = _rms(x, gpre_ref[...])
    y = _gelu(y_ref[...] + d_ref[...] * xn)
    z = _dot(y.astype(BF16), w_ref[...]) + b_ref[...]
    o_ref[...] = x + _rms(y * _sigmoid(z), gpost_ref[...])


def glu_tail(x, y_ssm, gpre, d_skip, w_bf, b, gpost):
    m = x.shape[0]
    tm = _row_tile(m)
    vec = pl.BlockSpec((1, D_MODEL), lambda i: (0, 0))
    row = pl.BlockSpec((tm, D_MODEL), lambda i: (i, 0))
    return pl.pallas_call(
        _glu_kernel,
        grid=(m // tm,),
        in_specs=[row, row, vec, vec, pl.BlockSpec((D_MODEL, D_MODEL), lambda i: (0, 0)), vec, vec],
        out_specs=row,
        out_shape=jax.ShapeDtypeStruct((m, D_MODEL), F32),
        compiler_params=_cparams(("parallel",)),
        name="glu_tail",
    )(x, y_ssm, gpre.reshape(1, -1), d_skip.reshape(1, -1), w_bf, b.reshape(1, -1), gpost.reshape(1, -1))


S5_GB = 8
S5_SW = S5_GB * 2 * SSM_STATE


def _s5_prep_kernel(lrl_ref, lil_ref, ldt_ref, lrc_ref, lic_ref, btr_ref, bti_ref, ccat_ref, ctr_ref, cti_ref,
                    wlag_ref, winj_ref, dec_ref, *wcar_refs, pads):
    t = S5_CHUNK
    dt = jnp.exp(ldt_ref[...])
    lane = lax.broadcasted_iota(jnp.int32, (1, 1, HEAD_DIM), 2)
    re_half = lane < SSM_STATE
    lr, li = lrl_ref[...], lil_ref[...]
    ldr, ldi = lr * dt, li * dt
    mag = jnp.exp(ldr)
    a_re, a_im = mag * jnp.cos(ldi), mag * jnp.sin(ldi)
    den = lr * lr + li * li
    nr = a_re - 1.0
    f_re = (nr * lr + a_im * li) / den
    f_im = (a_im * lr - nr * li) / den
    bt_re = jnp.concatenate([btr_ref[...]] * t, axis=1)
    bt_im = jnp.concatenate([bti_ref[...]] * t, axis=1)
    bb_re = f_re * bt_re - f_im * bt_im
    bb_im = f_re * bt_im + f_im * bt_re
    k = (lax.broadcasted_iota(jnp.int32, (1, t * SSM_GROUP, 1), 1) // SSM_GROUP).astype(F32)
    mk = jnp.exp(ldr * k)
    ak_re, ak_im = mk * jnp.cos(ldi * k), mk * jnp.sin(ldi * k)
    xc = jnp.where(re_half, ak_re * bb_re - ak_im * bb_im, ak_re * bb_im + ak_im * bb_re)
    kt = jnp.einsum('gnp,gpl->gnl', xc, ccat_ref[...], precision=lax.Precision.HIGHEST,
                    preferred_element_type=F32)
    own = lane // SSM_GROUP
    winj_ref[...] = jnp.zeros_like(winj_ref)
    for g in range(S5_GB):
        kg = jnp.where(own[0] == g, kt[g], 0.0).astype(BF16)
        xg = xc[g].astype(BF16)
        for s in range(t):
            src = slice((t - 1 - s) * SSM_GROUP, (t - s) * SSM_GROUP)
            dst = slice(s * HEAD_DIM + g * SSM_GROUP, s * HEAD_DIM + (g + 1) * SSM_GROUP)
            wlag_ref[0, dst, :] = kg[src]
            winj_ref[0, dst, g * HEAD_DIM:(g + 1) * HEAD_DIM] = xg[src]
    for n, pad in enumerate(pads):
        nn = float(t - pad)
        mn = jnp.exp(ldr * nn)
        an_re, an_im = mn * jnp.cos(ldi * nn), mn * jnp.sin(ldi * nn)
        dec_ref[:, 2 * n:2 * n + 1, :] = an_re
        dec_ref[:, 2 * n + 1:2 * n + 2, :] = jnp.where(re_half, -an_im, an_im)
    lrc, lic = lrc_ref[...], lic_ref[...]
    magc = jnp.exp(lrc * dt)
    ac_re, ac_im = magc * jnp.cos(lic * dt), magc * jnp.sin(lic * dt)
    top = lax.broadcasted_iota(jnp.int32, (1, HEAD_DIM, 1), 1) < SSM_STATE
    ctr, cti = ctr_ref[...], cti_ref[...]
    for ref, pad in zip(wcar_refs, pads):
        if pad:
            ref[...] = jnp.zeros_like(ref)
    p_re, p_im = ac_re, ac_im
    for e in range(t):
        w = jnp.where(top, ctr * p_re - cti * p_im, -(ctr * p_im + cti * p_re))
        for g in range(S5_GB):
            wg = jnp.where(own[0] == g, w[g], 0.0).astype(BF16)
            for ref, pad in zip(wcar_refs, pads):
                if e + pad < t:
                    ref[0, e + pad, g * HEAD_DIM:(g + 1) * HEAD_DIM, :] = wg
        p_re, p_im = p_re * ac_re - p_im * ac_im, p_re * ac_im + p_im * ac_re


def s5_prep(lam_re, lam_im, log_dt, b_re, b_im, c_re, c_im, pads):
    g, p = lam_re.shape
    gb, t = S5_GB, S5_CHUNK
    n_blk = g // gb
    twice = lambda a, axis: jnp.concatenate([a, a], axis=axis)
    lanes = lambda a: twice(a, 1).reshape(g, 1, 2 * p)
    cols = lambda a: twice(a, 1).reshape(g, 2 * p, 1)
    bt = lambda a: twice(a.transpose(0, 2, 1), 2)
    crt, cit = c_re.transpose(0, 2, 1), c_im.transpose(0, 2, 1)
    over_h = lambda a: jnp.concatenate([a] * gb, axis=2)
    ccat = over_h(jnp.concatenate([crt, -cit], axis=1))
    blk3 = lambda s: pl.BlockSpec((gb,) + s, lambda i: (i, 0, 0))
    sq = blk3((2 * p, HEAD_DIM))
    car_spec = pl.BlockSpec((1, t, gb * HEAD_DIM, HEAD_DIM), lambda i: (i, 0, 0, 0))
    car_shape = jax.ShapeDtypeStruct((n_blk, t, gb * HEAD_DIM, HEAD_DIM), BF16)
    outs = pl.pallas_call(
        functools.partial(_s5_prep_kernel, pads=pads),
        grid=(n_blk,),
        in_specs=[blk3((1, 2 * p)), blk3((1, 2 * p)), blk3((1, 1)), blk3((2 * p, 1)), blk3((2 * p, 1)),
                  blk3((SSM_GROUP, 2 * p)), blk3((SSM_GROUP, 2 * p)), sq, sq, sq],
        out_specs=[pl.BlockSpec((1, t * HEAD_DIM, HEAD_DIM), lambda i: (i, 0, 0)),
                   pl.BlockSpec((1, t * HEAD_DIM, S5_SW), lambda i: (i, 0, 0)),
                   blk3((2 * len(pads), 2 * p))] + [car_spec] * len(pads),
        out_shape=[jax.ShapeDtypeStruct((n_blk, t * HEAD_DIM, HEAD_DIM), BF16),
                   jax.ShapeDtypeStruct((n_blk, t * HEAD_DIM, S5_SW), BF16),
                   jax.ShapeDtypeStruct((g, 2 * len(pads), 2 * p), F32)] + [car_shape] * len(pads),
        compiler_params=_cparams(("parallel",)),
        name="s5_prep",
    )(lanes(lam_re), lanes(lam_im), log_dt.reshape(g, 1, 1), cols(lam_re), cols(lam_im), bt(b_re), bt(b_im),
      ccat, over_h(twice(crt, 1)), over_h(twice(cit, 1)))
    w_lag, w_inj, dec = outs[:3]
    dec = dec.reshape(n_blk, gb, 2 * len(pads), 2 * p).transpose(2, 0, 1, 3).reshape(2 * len(pads), n_blk, 1, S5_SW)
    return dict(w_lag=w_lag, w_inj=w_inj,
                w_car={pad: outs[3 + n] for n, pad in enumerate(pads)},
                a1={pad: dec[2 * n] for n, pad in enumerate(pads)},
                a2={pad: dec[2 * n + 1] for n, pad in enumerate(pads)})


def _swap_halves(x):
    lane = lax.broadcasted_iota(jnp.int32, (1, x.shape[-1]), 1)
    return jnp.where(lane % (2 * SSM_STATE) < SSM_STATE,
                     pltpu.roll(x, x.shape[-1] - SSM_STATE, x.ndim - 1), pltpu.roll(x, SSM_STATE, x.ndim - 1))


def _s5_block_kernel(x_ref, wlag_ref, winj_ref, wcar_ref, a1_ref, a2_ref, a2s_ref, h0_ref, y_ref, hf_ref,
                     xc_ref, s_ref, ss_ref, hp_ref, *, nbk, nc):
    mc = nbk * nc
    t = S5_CHUNK
    for s in range(t):
        xc_ref[:, s * HEAD_DIM:(s + 1) * HEAD_DIM] = x_ref[pl.ds(s, mc, stride=t), :].astype(BF16)
    inc = _dot(xc_ref[...], winj_ref[0])
    s_ref[...] = inc
    ss_ref[...] = _swap_halves(inc)
    a1, a2, a2s = a1_ref[0], a2_ref[0], a2s_ref[0]
    h0 = tuple((h0_ref[0, 0, b:b + 1, :], _swap_halves(h0_ref[0, 0, b:b + 1, :])) for b in range(nbk))

    def step(c, hs):
        out = []
        for b in range(nbk):
            h, hx = hs[b]
            row = b * nc + c
            hp_ref[pl.ds(row, 1), :] = h
            out.append((a1 * h + a2 * hx + s_ref[pl.ds(row, 1), :],
                        a1 * hx + a2s * h + ss_ref[pl.ds(row, 1), :]))
        return tuple(out)

    hs = lax.fori_loop(0, nc, step, h0)
    for b in range(nbk):
        hf_ref[0, 0, b:b + 1, :] = hs[b][0]
    hp = hp_ref[...].astype(BF16)
    for tt in range(t):
        y = (_dot(xc_ref[:, 0:(tt + 1) * HEAD_DIM], wlag_ref[0, (t - 1 - tt) * HEAD_DIM:, :])
             + _dot(hp, wcar_ref[0, tt]))
        y_ref[pl.ds(tt, mc, stride=t), :] = y


def s5_block(xn, w_lag, w_inj, w_car, a1, a2, a2s, h0, nbk, nc):
    m = xn.shape[0]
    rows = nbk * nc * S5_CHUNK
    n_step = m // rows
    n_blk = N_GROUPS // S5_GB
    kdim = S5_CHUNK * HEAD_DIM
    wspec = lambda a: pl.BlockSpec((1,) + a.shape[1:], lambda g, b: (g,) + (0,) * (a.ndim - 1))
    hspec = pl.BlockSpec((1, 1, nbk, S5_SW), lambda g, b: (g, b, 0, 0))
    xspec = pl.BlockSpec((rows, HEAD_DIM), lambda g, b: (b, g))
    return pl.pallas_call(
        functools.partial(_s5_block_kernel, nbk=nbk, nc=nc),
        grid=(n_blk, n_step),
        in_specs=[xspec, wspec(w_lag), wspec(w_inj), wspec(w_car), wspec(a1), wspec(a2), wspec(a2s), hspec],
        out_specs=[xspec, hspec],
        out_shape=[jax.ShapeDtypeStruct((m, D_MODEL), F32), jax.ShapeDtypeStruct(h0.shape, F32)],
        scratch_shapes=[pltpu.VMEM((nbk * nc, kdim), BF16)] + [pltpu.VMEM((nbk * nc, S5_SW), F32)] * 3,
        compiler_params=_cparams(("parallel", "arbitrary")),
        name="s5_block",
    )(xn, w_lag, w_inj, w_car, a1, a2, a2s, h0)


CMP_PAGES = 16
CMP_ROWS = CMP_PAGES * (PAGE // CMP_STRIDE)
CMP_FLAT = CMP_STRIDE * HEAD_DIM


PAGE_CHUNKS = KV_SLAB // HEAD_DIM
PAGE_ROWS = PAGE * PAGE_CHUNKS


def _page_specs(n, first):
    def spec(k):
        return pl.BlockSpec((1, PAGE_ROWS, HEAD_DIM), lambda b, p, pt: (pt[b, first(p) + k], 0, 0))
    return [spec(k) for k in range(n)]


def _cmp_proj_kernel(pt_ref, *refs):
    x_refs = refs[:CMP_PAGES]
    wk_ref, wv_ref, o_ref, lhs_ref = refs[CMP_PAGES:]
    per_page = PAGE // CMP_STRIDE
    for pg, x_ref in enumerate(x_refs):
        for kv in range(2):
            for g in range(N_KV):
                c = kv * N_KV + g
                for s in range(CMP_STRIDE):
                    piece = x_ref[0, pl.ds(s * PAGE_CHUNKS + c, per_page, stride=CMP_STRIDE * PAGE_CHUNKS), :]
                    lhs_ref[kv, g, pg * per_page:(pg + 1) * per_page, s * HEAD_DIM:(s + 1) * HEAD_DIM] = piece
    for kv, w_ref in ((0, wk_ref), (1, wv_ref)):
        for g in range(N_KV):
            o_ref[0, kv, g] = _dot(lhs_ref[kv, g].astype(BF16), w_ref[...])


def cmp_proj(pages, page_table, wk_cat, wv_cat):
    nb, npg = page_table.shape
    assert npg % CMP_PAGES == 0, "compression consumes whole groups of pages"
    n_ch = npg * (PAGE // CMP_STRIDE)
    wspec = pl.BlockSpec((CMP_FLAT, 2 * CMP_HIDDEN), lambda b, p, pt: (0, 0))
    grid_spec = pltpu.PrefetchScalarGridSpec(
        num_scalar_prefetch=1,
        grid=(nb, npg // CMP_PAGES),
        in_specs=_page_specs(CMP_PAGES, lambda p: p * CMP_PAGES) + [wspec, wspec],
        out_specs=pl.BlockSpec((1, 2, N_KV, CMP_ROWS, 2 * CMP_HIDDEN), lambda b, p, pt: (b, 0, 0, p, 0)),
        scratch_shapes=[pltpu.VMEM((2, N_KV, CMP_ROWS, CMP_FLAT), F32)],
    )
    return pl.pallas_call(
        _cmp_proj_kernel,
        grid_spec=grid_spec,
        out_shape=jax.ShapeDtypeStruct((nb, 2, N_KV, n_ch, 2 * CMP_HIDDEN), F32),
        compiler_params=_cparams(("parallel", "arbitrary")),
        name="cmp_proj",
    )(page_table, *([pages] * CMP_PAGES), wk_cat, wv_cat)


def _cmp_mlp_kernel(p_ref, w1_ref, pe_ref, w2_ref, o_ref):
    proj = p_ref[0, 0, 0]
    n_ch = proj.shape[0]
    pre0 = jnp.sum(pe_ref[0] * w1_ref[0], axis=0, keepdims=True)
    first = proj[:, 0:CMP_HIDDEN]
    second = pltpu.roll(proj[:, CMP_HIDDEN:2 * CMP_HIDDEN], n_ch - 1, 0)
    pre = (pre0 + first) + second
    o_ref[0, 0, 0] = _dot(_gelu(pre).astype(BF16), w2_ref[0])


def cmp_mlp(proj, w1, pe, w2_bf):
    nb, _, _, n_ch, _ = proj.shape
    flat = CMP_BLOCK * HEAD_DIM
    return pl.pallas_call(
        _cmp_mlp_kernel,
        grid=(nb, 2, N_KV),
        in_specs=[pl.BlockSpec((1, 1, 1, n_ch, 2 * CMP_HIDDEN), lambda b, k, g: (b, k, g, 0, 0)),
                  pl.BlockSpec((1, flat, CMP_HIDDEN), lambda b, k, g: (k, 0, 0)),
                  pl.BlockSpec((1, flat, 1), lambda b, k, g: (k, 0, 0)),
                  pl.BlockSpec((1, CMP_HIDDEN, HEAD_DIM), lambda b, k, g: (k, 0, 0))],
        out_specs=pl.BlockSpec((1, 1, 1, n_ch, HEAD_DIM), lambda b, k, g: (b, k, g, 0, 0)),
        out_shape=jax.ShapeDtypeStruct((nb, 2, N_KV, n_ch, HEAD_DIM), F32),
        compiler_params=_cparams(("parallel", "parallel", "parallel")),
        name="cmp_mlp",
    )(proj, w1, pe, w2_bf)


def _kv_pack_kernel(pt_ref, *refs, per, n_groups):
    x_refs = refs[:per]
    t_ref, k_ref, vt_ref = refs[per:]
    j = pl.program_id(1)

    def emit(head):
        for g in range(N_KV):
            for pg in range(per):
                sl = slice(pg * PAGE, (pg + 1) * PAGE)
                k_ref[0, g, sl, :] = head(pg, g).astype(BF16)
                vt_ref[0, g, 0, :, sl] = head(pg, N_KV + g).T.astype(BF16)

    @pl.when(j < n_groups)
    def _():
        emit(lambda pg, c: x_refs[pg][0, pl.ds(c, PAGE, stride=PAGE_CHUNKS), :])

    @pl.when(j >= n_groups)
    def _():
        emit(lambda pg, c: t_ref[0, pl.ds(pg * PAGE_ROWS + c, PAGE, stride=PAGE_CHUNKS), :])


def kv_pack(pages, page_table, tail, vt_tile):
    nb, n_pages = page_table.shape
    per = vt_tile // PAGE
    assert n_pages % per == 0
    n_groups = n_pages // per
    if tail is None:
        n_tail = 0
        tail = jnp.zeros((nb, per * PAGE_ROWS, HEAD_DIM), F32)
    else:
        n_tail = tail.shape[1] // vt_tile
        tail = tail.reshape(nb, n_tail * per * PAGE_ROWS, HEAD_DIM)
    n_tot = n_groups + n_tail
    grid_spec = pltpu.PrefetchScalarGridSpec(
        num_scalar_prefetch=1,
        grid=(nb, n_tot),
        in_specs=_page_specs(per, lambda p: jnp.minimum(p, n_groups - 1) * per)
        + [pl.BlockSpec((1, per * PAGE_ROWS, HEAD_DIM), lambda b, p, pt: (b, jnp.maximum(p - n_groups, 0), 0))],
        out_specs=[pl.BlockSpec((1, N_KV, vt_tile, HEAD_DIM), lambda b, p, pt: (b, 0, p, 0)),
                   pl.BlockSpec((1, N_KV, 1, HEAD_DIM, vt_tile), lambda b, p, pt: (b, 0, p, 0, 0))],
    )
    return pl.pallas_call(
        functools.partial(_kv_pack_kernel, per=per, n_groups=n_groups),
        grid_spec=grid_spec,
        out_shape=[jax.ShapeDtypeStruct((nb, N_KV, n_tot * vt_tile, HEAD_DIM), BF16),
                   jax.ShapeDtypeStruct((nb, N_KV, n_tot, HEAD_DIM, vt_tile), BF16)],
        compiler_params=_cparams(("parallel", "arbitrary")),
        name="kv_pack",
    )(page_table, *([pages] * per), tail)


def _split3(x):
    hi = x.astype(BF16)
    r1 = x - hi.astype(F32)
    mid = r1.astype(BF16)
    lo = (r1 - mid.astype(F32)).astype(BF16)
    return hi, mid, lo


def _softmax_cols(s, col_ok):
    m = jnp.max(s, axis=0, keepdims=True)
    e = jnp.exp(s - m)
    den = jnp.sum(e, axis=0, keepdims=True)
    return e * jnp.where(col_ok, 1.0 / den, 0.0)


def _nsa_kernel(q_ref, qr_ref, gt_ref, kc_ref, vct_ref, ks_ref, vst_ref, kw_ref, vwt_ref, o_ref,
                bias_ref, s_ref, m_ref, l_ref, acc_ref, *, tq, q_off, swa_base):
    rq = GROUP_Q * tq
    w = min(rq, HEAD_DIM)
    reps = rq // w
    n_cp = kc_ref.shape[2]
    n_tiles = bias_ref.shape[0]
    per = SLC_TILE // SLC_BLOCK
    n_blk = n_tiles * per
    tw = kw_ref.shape[2]
    i = pl.program_id(2)
    t0 = q_off + i * tq
    t_w = t0 + lax.broadcasted_iota(jnp.int32, (1, w), 1) % tq

    def tile(x):
        return jnp.concatenate([x] * reps, axis=x.ndim - 1) if reps > 1 else x

    def rows(ref):
        x = ref[...]
        return jnp.concatenate([x[:, r * HEAD_DIM:(r + 1) * HEAD_DIM] for r in range(GROUP_Q)], axis=0)

    q2 = rows(q_ref)
    qr2 = rows(qr_ref)
    t_rq = tile(t_w)

    n_io = lax.broadcasted_iota(jnp.int32, (n_cp, 1), 0)
    bias_c = jnp.where((n_io * CMP_STRIDE + (CMP_BLOCK - 1)) <= t_w, 0.0, NEG)
    p_c = _softmax_cols(_dot_nt(kc_ref[0, 0], q2) + tile(bias_c), t_rq >= CMP_BLOCK - 1)
    o_c = _dot(vct_ref[0, 0], p_c.astype(BF16))
    if reps > 1:
        p_sum = p_c[:, 0:w]
        for r in range(1, reps):
            p_sum = p_sum + p_c[:, r * w:(r + 1) * w]
    else:
        p_sum = p_c
        for r in range(1, GROUP_Q):
            p_sum = p_sum + pltpu.roll(p_c, r * tq, 1)
    s_col = lax.broadcasted_iota(jnp.int32, (n_blk, 1), 0)
    n_row = lax.broadcasted_iota(jnp.int32, (1, n_cp), 1)
    ov = jnp.logical_and(n_row * CMP_STRIDE < (s_col + 1) * SLC_BLOCK,
                         n_row * CMP_STRIDE + CMP_BLOCK > s_col * SLC_BLOCK)
    ov = jnp.where(ov, 1.0, 0.0).astype(BF16)
    hi, mid, lo = _split3(p_sum)
    imp = (_dot(ov, hi) + _dot(ov, mid)) + _dot(ov, lo)
    cur = t_w // SLC_BLOCK
    forced = jnp.logical_or(s_col == 0, jnp.logical_or(s_col == cur, s_col == cur - 1))
    elig = s_col * SLC_BLOCK <= t_w
    imp = jnp.where(forced, FORCE_SCORE, imp)
    imp = jnp.where(elig, imp, -jnp.inf)

    s_colf = s_col.astype(F32)

    def pick(_, carry):
        work, sel = carry
        best = jnp.max(work, axis=0, keepdims=True)
        first = jnp.min(jnp.where(work == best, s_colf, float(n_blk)), axis=0, keepdims=True)
        hit = s_colf == first
        return jnp.where(hit, -jnp.inf, work), jnp.where(hit, 1.0, sel)

    _, sel = lax.fori_loop(0, N_SELECT, pick, (imp, jnp.zeros((n_blk, w), F32)))
    bias_s = jnp.where(jnp.logical_and(elig, sel > 0.5), 0.0, NEG).reshape(n_tiles, per, w)
    bias_s = jnp.concatenate([bias_s, jnp.zeros_like(bias_s)], axis=1).astype(BF16)
    bias_ref[...] = tile(bias_s)

    m_ref[...] = jnp.full_like(m_ref, NEG)
    l_ref[...] = jnp.zeros_like(l_ref)
    acc_ref[...] = jnp.zeros_like(acc_ref)
    qr_t = qr2.astype(F32).T.astype(BF16)
    k_io = lax.broadcasted_iota(jnp.int32, (SLC_TILE, 1), 0)
    c_io = lax.broadcasted_iota(jnp.int32, (1, HEAD_DIM), 1)
    onehot = jnp.where(k_io // SLC_BLOCK == c_io, 1.0, 0.0).astype(BF16)
    zpad = jnp.zeros((HEAD_DIM - 2 * per, rq), BF16)

    def scores(kt):
        base = pl.multiple_of(kt * SLC_TILE, SLC_TILE)
        lhs = jnp.concatenate([ks_ref[0, 0, pl.ds(base, SLC_TILE), :], onehot], axis=1)
        rhs = jnp.concatenate([qr_t, bias_ref[kt], zpad], axis=0)
        return _dot(lhs, rhs)

    def update(kt, ss):
        m_old = m_ref[...]
        m_new = m_old
        for s in ss:
            m_new = jnp.maximum(m_new, jnp.max(s, axis=0, keepdims=True))
        alpha = jnp.exp(m_old - m_new)
        l_new = alpha * l_ref[...]
        acc = alpha * acc_ref[...]
        for j, s in enumerate(ss):
            e = jnp.exp(s - m_new)
            l_new = l_new + jnp.sum(e, axis=0, keepdims=True)
            acc = acc + _dot(vst_ref[0, 0, kt + j], e.astype(BF16))
        l_ref[...] = l_new
        acc_ref[...] = acc
        m_ref[...] = m_new

    k_last = t0 // SLC_TILE
    causal = tile(jnp.where((k_last * SLC_TILE + k_io) <= t_w, 0.0, NEG))
    if rq >= SLC_TILE:
        sa_ref, sb_ref = s_ref.at[0], s_ref.at[1]
        sa_ref[...] = scores(0)

        def body(i, carry):
            kt = 2 * i
            sb_ref[...] = scores(kt + 1)
            update(kt, [sa_ref[...]])
            sa_ref[...] = scores(kt + 2)
            update(kt + 1, [sb_ref[...]])
            return carry

        n_pair = k_last // 2
        lax.fori_loop(0, n_pair, body, 0)

        @pl.when(k_last % 2 == 0)
        def _():
            update(k_last, [sa_ref[...] + causal])

        @pl.when(k_last % 2 == 1)
        def _():
            sb_ref[...] = scores(k_last)
            update(k_last - 1, [sa_ref[...]])
            update(k_last, [sb_ref[...] + causal])
    else:
        joint = SLC_TILE // rq
        n_step = k_last // joint

        def body(i, carry):
            update(i * joint, [scores(i * joint + j) for j in range(joint)])
            return carry

        lax.fori_loop(0, n_step, body, 0)

        def single(kt, carry):
            update(kt, [scores(kt)])
            return carry

        lax.fori_loop(n_step * joint, k_last, single, 0)
        update(k_last, [scores(k_last) + causal])
    o_s = acc_ref[...] * (1.0 / l_ref[...])

    start = jnp.clip(t0 - WINDOW - swa_base, 0, tw - SWA_SPAN)
    start = pl.multiple_of(start, PAGE)
    key_pos = swa_base + start + lax.broadcasted_iota(jnp.int32, (SWA_SPAN, 1), 0)
    dist = t_w - key_pos
    ok_w = jnp.logical_and(jnp.logical_and(dist >= 0, dist < WINDOW), key_pos >= swa_base)
    sw = _dot_nt(kw_ref[0, 0, pl.ds(start, SWA_SPAN), :], qr2) + tile(jnp.where(ok_w, 0.0, NEG))
    p_w = _softmax_cols(sw, True).astype(BF16)
    o_w = jnp.zeros((HEAD_DIM, rq), F32)
    for jt in range(SWA_SPAN // PAGE):
        o_w = o_w + _dot(vwt_ref[0, 0, start // PAGE + jt], p_w[jt * PAGE:(jt + 1) * PAGE, :])

    gt = gt_ref[0, 0, 0]
    o_t = o_c * gt[0:1, :] + o_s * gt[1:2, :] + o_w * gt[2:3, :]
    for c in range(rq // HEAD_DIM):
        blk = o_t[:, c * HEAD_DIM:(c + 1) * HEAD_DIM].T.astype(BF16)
        per_blk = HEAD_DIM // tq
        for rr in range(per_blk):
            r = c * per_blk + rr
            o_ref[:, r * HEAD_DIM:(r + 1) * HEAD_DIM] = blk[rr * tq:(rr + 1) * tq, :]


def nsa_attention(q, qr, gates_t, kc, vct, ks, vst, kw, vwt, *, nb, nq, tq, q_off, swa_base):
    rq = GROUP_Q * tq
    tk = ks.shape[2]
    assert SLC_TILE % tq == 0 and q_off % SLC_TILE == 0 and tk % SLC_TILE == 0
    qspec = pl.BlockSpec((tq, GROUP_W), lambda b, g, i: (b * nq + i, g))
    full = lambda a: pl.BlockSpec((1, 1) + a.shape[2:], lambda b, g, i: (b, g) + (0,) * (a.ndim - 2))
    return pl.pallas_call(
        functools.partial(_nsa_kernel, tq=tq, q_off=q_off, swa_base=swa_base),
        grid=(nb, N_KV, nq),
        in_specs=[qspec, qspec,
                  pl.BlockSpec((1, 1, 1, N_BRANCH, rq), lambda b, g, i: (b, g, i, 0, 0)),
                  full(kc), full(vct), full(ks), full(vst), full(kw), full(vwt)],
        out_specs=qspec,
        out_shape=jax.ShapeDtypeStruct((nb * nq * tq, Q_WIDTH), BF16),
        scratch_shapes=[pltpu.VMEM((tk // SLC_TILE, 2 * (SLC_TILE // SLC_BLOCK), rq), BF16),
                        pltpu.VMEM((2, SLC_TILE, rq), F32),
                        pltpu.VMEM((1, rq), F32), pltpu.VMEM((1, rq), F32), pltpu.VMEM((HEAD_DIM, rq), F32)],
        compiler_params=_cparams(("parallel", "parallel", "arbitrary")),
        name="nsa_attention",
    )(q, qr, gates_t, kc, vct, ks, vst, kw, vwt)


def _s5_layer(x, h0_re, h0_im, lp, nb, seq):
    m = x.shape[0]
    nc = -(-seq // S5_CHUNK)
    pad = nc * S5_CHUNK - seq
    n_blk = N_GROUPS // S5_GB
    xn = rms_norm(x, lp['norm_pre'])
    if pad == 0:
        nbk, nbp, xs = 1, nb, xn
    else:
        nbp = -(-nb // S5_CHUNK) * S5_CHUNK
        nbk = nbp
        xs = jnp.pad(xn.reshape(nb, seq, D_MODEL), ((0, nbp - nb), (pad, 0), (0, 0))).reshape(-1, D_MODEL)
    n_step = nbp // nbk
    a1, a2 = lp['a1'][pad], lp['a2'][pad]
    if h0_re is None:
        h0 = jnp.zeros((n_blk, n_step, nbk, S5_SW), F32)
    else:
        h0 = jnp.stack([h0_re, h0_im], axis=2).reshape(nb, n_blk, S5_SW)
        h0 = jnp.pad(h0, ((0, nbp - nb), (0, 0), (0, 0))).transpose(1, 0, 2).reshape(n_blk, n_step, nbk, S5_SW)
    y, hf = s5_block(xs, lp['w_lag'], lp['w_inj'], lp['w_car'][pad], a1, a2, -a2, h0, nbk, nc)
    y = y.reshape(nbp, nc * S5_CHUNK, D_MODEL)[:nb, pad:].reshape(m, D_MODEL)
    hf = hf.reshape(n_blk, nbp, S5_GB, 2, SSM_STATE)[:, :nb].transpose(3, 1, 0, 2, 4).reshape(2, nb, N_GROUPS, SSM_STATE)
    x = glu_tail(x, y, lp['norm_pre'], lp['d'], lp['w_glu'], lp['b_glu'], lp['norm_post'])
    return x, hf[0], hf[1]


def _gates_t(gates, nb, nq, tq):
    g = gates[:, :N_HEADS * N_BRANCH].reshape(nb, nq, tq, N_KV, GROUP_Q, N_BRANCH)
    return g.transpose(0, 3, 1, 5, 4, 2).reshape(nb, N_KV, nq, N_BRANCH, GROUP_Q * tq)


def _pad_rows(a, nb, seq, tq):
    if seq == tq or seq % tq == 0:
        return a
    a = a.reshape(nb, seq, -1)
    return jnp.pad(a, ((0, 0), (0, tq - seq), (0, 0))).reshape(nb * tq, -1)


def _nsa_layer(x, lp, kvs, tables, nb, seq, tq, q_off, swa_base):
    q, qr = q_proj(x, lp['norm_pre'], lp['w_q'], tables)
    gates = gate_proj(x, lp['norm_pre'], lp['w_g'])
    nq = -(-seq // tq)
    o = nsa_attention(_pad_rows(q, nb, seq, tq), _pad_rows(qr, nb, seq, tq),
                      _gates_t(_pad_rows(gates, nb, seq, tq), nb, nq, tq),
                      *kvs, nb=nb, nq=nq, tq=tq, q_off=q_off, swa_base=swa_base)
    if nq * tq != seq:
        o = o.reshape(nb, nq * tq, Q_WIDTH)[:, :seq].reshape(nb * seq, Q_WIDTH)
    return oproj(o, lp['w_o'], lp['norm_post'], x)


def _trunk(x, nb, seq, pos0, h0_re, h0_im, past, prm):
    m = nb * seq
    ssm_re, ssm_im = [], []
    n_a = len(prm['a'])
    for layer in range(n_a):
        x, hr, hi = _s5_layer(x, None if h0_re is None else h0_re[layer],
                              None if h0_im is None else h0_im[layer], prm['a'][layer], nb, seq)
        ssm_re.append(hr)
        ssm_im.append(hi)
        ml = prm['mlp'][layer]
        x = mlp(x, ml['norm_pre'], ml['w_up'], ml['w_down'], ml['norm_post'])

    pos = pos0 + jnp.tile(jnp.arange(seq), nb)
    tables = _rope_tables(pos)
    kv = kv_proj(x, prm['kv_norm'], prm['w_kv'], tables)
    rows_cmp = kv[:, 0:KV_SLAB].reshape(nb, seq, 2, N_KV, HEAD_DIM)
    rows_slc = kv[:, KV_SLAB:2 * KV_SLAB].reshape(nb, seq, 2, N_KV, HEAD_DIM)
    rows_swa = kv[:, 2 * KV_SLAB:3 * KV_SLAB]
    no_tail = None
    if past is None:
        npg = seq // PAGE
        table = jnp.arange(nb * npg, dtype=jnp.int32).reshape(nb, npg)
        pages = lambda j: kv[:, j * KV_SLAB:(j + 1) * KV_SLAB].reshape(nb * npg, PAGE_ROWS, HEAD_DIM)
        cproj = cmp_proj(pages(0), table, prm['cmp_w1k_cat'], prm['cmp_w1v_cat'])
        ks, vst = kv_pack(pages(1), table, no_tail, SLC_TILE)
        kw, vwt = kv_pack(pages(2), table, no_tail, PAGE)
        swa_base = 0
        tq = 128
        swa_buf = rows_swa.reshape(nb, seq, 2, N_KV, HEAD_DIM)[:, seq - WINDOW:]
    else:
        cache_cmp, cache_slc, state_swa, table = past
        n_pool = cache_cmp.shape[0]
        npg = table.shape[1]
        cproj = cmp_proj(cache_cmp.reshape(n_pool, PAGE_ROWS, HEAD_DIM), table, prm['cmp_w1k_cat'], prm['cmp_w1v_cat'])
        t_real = npg * PAGE + seq
        t_pad = -(-t_real // SLC_TILE) * SLC_TILE
        tail = jnp.pad(kv[:, KV_SLAB:2 * KV_SLAB].reshape(nb, seq, KV_SLAB),
                       ((0, 0), (0, t_pad - npg * PAGE - seq), (0, 0)))
        ks, vst = kv_pack(cache_slc.reshape(n_pool, PAGE_ROWS, HEAD_DIM), table, tail, SLC_TILE)
        w_keep = state_swa.shape[1]
        local = jnp.concatenate([state_swa.reshape(nb, w_keep, KV_SLAB), rows_swa.reshape(nb, seq, KV_SLAB)], axis=1)
        swa_buf = local[:, -w_keep:].reshape(nb, w_keep, 2, N_KV, HEAD_DIM)
        tw = -(-max(local.shape[1], SWA_SPAN) // PAGE) * PAGE
        local = jnp.pad(local, ((0, 0), (0, tw - local.shape[1]), (0, 0)))
        lt = jnp.arange(nb * (tw // PAGE), dtype=jnp.int32).reshape(nb, tw // PAGE)
        kw, vwt = kv_pack(local.reshape(nb * (tw // PAGE), PAGE_ROWS, HEAD_DIM), lt, no_tail, PAGE)
        swa_base = pos0 - w_keep
        tq = 32
    cmp_out = cmp_mlp(cproj, prm['cmp_w1'], prm['cmp_pe'], prm['cmp_w2'])
    kc = cmp_out[:, 0].astype(BF16)
    vct = cmp_out[:, 1].transpose(0, 1, 3, 2).astype(BF16)
    kvs = (kc, vct, ks, vst, kw, vwt)

    for j, lp in enumerate(prm['b']):
        x = _nsa_layer(x, lp, kvs, tables, nb, seq, tq, pos0, swa_base)
        ml = prm['mlp'][n_a + j]
        x = mlp(x, ml['norm_pre'], ml['w_up'], ml['w_down'], ml['norm_post'])
    return x, rows_cmp, rows_slc, swa_buf, jnp.stack(ssm_re), jnp.stack(ssm_im)


def kernel(x_prompt, x_sample, cache_kv_cmp, cache_kv_slc, state_kv_swa, state_ssm_re, state_ssm_im, page_table,
           a_norm_pre, a_lam_re, a_lam_im, a_log_dt, a_b_re, a_b_im, a_c_re, a_c_im, a_d, a_w_glu, a_b_glu,
           a_norm_post, kv_norm, w_kv, cmp_w1_k, cmp_pe_k, cmp_w2_k, cmp_w1_v, cmp_pe_v, cmp_w2_v, b_norm_pre,
           b_w_qg, b_w_o, b_norm_post, mlp_norm_pre, mlp_w_up, mlp_w_down, mlp_norm_post):
    n_a = a_norm_pre.shape[0]
    n_b = b_norm_pre.shape[0]
    prm = {'a': [], 'b': [], 'mlp': []}
    pads = tuple(sorted({-x_prompt.shape[1] % S5_CHUNK, -x_sample.shape[1] % S5_CHUNK}))
    for l in range(n_a):
        ops = s5_prep(a_lam_re[l], a_lam_im[l], a_log_dt[l], a_b_re[l], a_b_im[l], a_c_re[l], a_c_im[l], pads)
        prm['a'].append(dict(norm_pre=a_norm_pre[l], norm_post=a_norm_post[l], d=a_d[l],
                             w_glu=a_w_glu[l].astype(BF16), b_glu=a_b_glu[l], **ops))
    n_gate = N_HEADS * N_BRANCH
    for l in range(n_b):
        w_g = jnp.pad(b_w_qg[l][:, Q_WIDTH:], ((0, 0), (0, HEAD_DIM - n_gate))).astype(BF16)
        prm['b'].append(dict(norm_pre=b_norm_pre[l], norm_post=b_norm_post[l],
                             w_q=b_w_qg[l][:, :Q_WIDTH].astype(BF16), w_g=w_g, w_o=b_w_o[l].astype(BF16)))
    for l in range(n_a + n_b):
        prm['mlp'].append(dict(norm_pre=mlp_norm_pre[l], norm_post=mlp_norm_post[l],
                               w_up=mlp_w_up[l].astype(BF16), w_down=mlp_w_down[l].astype(BF16)))
    prm['kv_norm'] = kv_norm
    prm['w_kv'] = w_kv.astype(BF16)
    r = CMP_BLOCK // CMP_STRIDE
    cat = lambda w: w.reshape(r, CMP_FLAT, CMP_HIDDEN).transpose(1, 0, 2).reshape(CMP_FLAT, r * CMP_HIDDEN).astype(BF16)
    prm['cmp_w1k_cat'] = cat(cmp_w1_k)
    prm['cmp_w1v_cat'] = cat(cmp_w1_v)
    flat = CMP_BLOCK * HEAD_DIM
    prm['cmp_w1'] = jnp.stack([cmp_w1_k.reshape(flat, CMP_HIDDEN), cmp_w1_v.reshape(flat, CMP_HIDDEN)])
    prm['cmp_pe'] = jnp.stack([cmp_pe_k.reshape(flat, 1), cmp_pe_v.reshape(flat, 1)])
    prm['cmp_w2'] = jnp.stack([cmp_w2_k, cmp_w2_v]).astype(BF16)

    bp, sp, _ = x_prompt.shape
    y_p, cmp_p, slc_p, swa_p, re_p, im_p = _trunk(x_prompt.reshape(bp * sp, D_MODEL), bp, sp, 0,
                                                  None, None, None, prm)
    bs, ss, _ = x_sample.shape
    past_len = page_table.shape[1] * PAGE
    y_s, cmp_s, slc_s, swa_s, re_s, im_s = _trunk(x_sample.reshape(bs * ss, D_MODEL), bs, ss, past_len,
                                                  state_ssm_re, state_ssm_im,
                                                  (cache_kv_cmp, cache_kv_slc, state_kv_swa, page_table), prm)
    return (y_p.reshape(bp, sp, D_MODEL), y_s.reshape(bs, ss, D_MODEL), cmp_p, cmp_s, slc_p, slc_s,
            swa_p, swa_s, re_p, im_p, re_s, im_s)
```

```python
import functools
import math

import jax
import jax.numpy as jnp
from jax import lax
from jax.experimental import pallas as pl
from jax.experimental.pallas import tpu as pltpu

F32 = jnp.float32
BF16 = jnp.bfloat16

D_MODEL = 2048
N_HEADS = 16
HEAD_DIM = 128
N_KV = 4
GROUP_Q = N_HEADS // N_KV
N_BRANCH = 3
ROT_DIM = HEAD_DIM // 4
ROPE_THETA = 500000.0
SSM_GROUP = 16
N_GROUPS = D_MODEL // SSM_GROUP
SSM_STATE = 64
S5_CHUNK = 16
D_FF = 4 * D_MODEL
CMP_BLOCK = 32
CMP_STRIDE = 16
CMP_HIDDEN = 2 * HEAD_DIM
SLC_BLOCK = 64
N_SELECT = 16
WINDOW = 512
PAGE = 128
FORCE_SCORE = 1.0e4
EPS = 1e-6
Q_WIDTH = N_HEADS * HEAD_DIM
GROUP_W = GROUP_Q * HEAD_DIM
KV_SLAB = 2 * N_KV * HEAD_DIM
SLC_TILE = 512
SWA_SPAN = WINDOW + PAGE
NEG = -1.0e30
VMEM_LIMIT = 56 * 1024 * 1024


def _cparams(sem):
    return pltpu.CompilerParams(dimension_semantics=sem, vmem_limit_bytes=VMEM_LIMIT)


def _rms(x, g):
    var = jnp.mean(x * x, axis=-1, keepdims=True)
    return x * lax.rsqrt(var + EPS) * g


def _gelu(x):
    return 0.5 * x * (1.0 + jnp.tanh(math.sqrt(2.0 / math.pi) * (x + 0.044715 * (x * x * x))))


def _sigmoid(x):
    return 1.0 / (1.0 + jnp.exp(-x))


def _dot(a, b):
    return jnp.dot(a, b, preferred_element_type=F32)


def _dot_nt(a, b):
    return lax.dot_general(a, b, (((1,), (1,)), ((), ())), preferred_element_type=F32)


def _rope128(x, c, s1, s2):
    return x * c + pltpu.roll(x, HEAD_DIM - ROT_DIM // 2, 1) * s1 + pltpu.roll(x, ROT_DIM // 2, 1) * s2


def _rope_tables(pos):
    half = ROT_DIM // 2
    inv = ROPE_THETA ** (-jnp.arange(half, dtype=F32) / half)
    ang = pos.astype(F32)[:, None] * inv[None, :]
    cos, sin = jnp.cos(ang), jnp.sin(ang)
    n = pos.shape[0]
    rest = HEAD_DIM - ROT_DIM
    c = jnp.concatenate([cos, cos, jnp.ones((n, rest), F32)], axis=1)
    s1 = jnp.concatenate([-sin, jnp.zeros((n, HEAD_DIM - half), F32)], axis=1)
    s2 = jnp.concatenate([jnp.zeros((n, half), F32), sin, jnp.zeros((n, rest), F32)], axis=1)
    return c, s1, s2


def _row_tile(m):
    return 512 if m % 512 == 0 else m


def _kv_proj_kernel(x_ref, g_ref, w_ref, c_ref, s1_ref, s2_ref, o_ref, xn_ref):
    j = pl.program_id(1)

    @pl.when(j == 0)
    def _():
        xn_ref[...] = _rms(x_ref[...], g_ref[...]).astype(BF16)

    acc = _dot(xn_ref[...], w_ref[...])
    kw = N_KV * HEAD_DIM

    @pl.when(j >= 1)
    def _():
        c, s1, s2 = c_ref[...], s1_ref[...], s2_ref[...]
        for h in range(N_KV):
            sl = slice(h * HEAD_DIM, (h + 1) * HEAD_DIM)
            o_ref[:, sl] = _rope128(acc[:, sl], c, s1, s2)
        o_ref[:, kw:] = acc[:, kw:]

    @pl.when(j == 0)
    def _():
        o_ref[...] = acc


def kv_proj(x, g, w_bf, tables):
    m = x.shape[0]
    tm = _row_tile(m)
    n = w_bf.shape[1]
    tn = KV_SLAB
    c, s1, s2 = tables
    tab = pl.BlockSpec((tm, HEAD_DIM), lambda i, j: (i, 0))
    return pl.pallas_call(
        _kv_proj_kernel,
        grid=(m // tm, n // tn),
        in_specs=[pl.BlockSpec((tm, D_MODEL), lambda i, j: (i, 0)),
                  pl.BlockSpec((1, D_MODEL), lambda i, j: (0, 0)),
                  pl.BlockSpec((D_MODEL, tn), lambda i, j: (0, j)),
                  tab, tab, tab],
        out_specs=pl.BlockSpec((tm, tn), lambda i, j: (i, j)),
        out_shape=jax.ShapeDtypeStruct((m, n), F32),
        scratch_shapes=[pltpu.VMEM((tm, D_MODEL), BF16)],
        compiler_params=_cparams(("parallel", "arbitrary")),
        name="kv_proj",
    )(x, g.reshape(1, -1), w_bf, c, s1, s2)


def _q_proj_kernel(x_ref, g_ref, w_ref, c_ref, s1_ref, s2_ref, q_ref, qr_ref, xn_ref):
    j = pl.program_id(1)

    @pl.when(j == 0)
    def _():
        xn_ref[...] = _rms(x_ref[...], g_ref[...]).astype(BF16)

    acc = _dot(xn_ref[...], w_ref[...])
    scale = HEAD_DIM ** -0.5
    c, s1, s2 = c_ref[...], s1_ref[...], s2_ref[...]
    q_ref[...] = (acc * scale).astype(BF16)
    for h in range(acc.shape[1] // HEAD_DIM):
        sl = slice(h * HEAD_DIM, (h + 1) * HEAD_DIM)
        qr_ref[:, sl] = (_rope128(acc[:, sl], c, s1, s2) * scale).astype(BF16)


def q_proj(x, g, wq_bf, tables):
    m = x.shape[0]
    tm = _row_tile(m)
    tn = 2 * GROUP_W
    c, s1, s2 = tables
    tab = pl.BlockSpec((tm, HEAD_DIM), lambda i, j: (i, 0))
    out = jax.ShapeDtypeStruct((m, Q_WIDTH), BF16)
    ospec = pl.BlockSpec((tm, tn), lambda i, j: (i, j))
    return pl.pallas_call(
        _q_proj_kernel,
        grid=(m // tm, Q_WIDTH // tn),
        in_specs=[pl.BlockSpec((tm, D_MODEL), lambda i, j: (i, 0)),
                  pl.BlockSpec((1, D_MODEL), lambda i, j: (0, 0)),
                  pl.BlockSpec((D_MODEL, tn), lambda i, j: (0, j)),
                  tab, tab, tab],
        out_specs=[ospec, ospec],
        out_shape=[out, out],
        scratch_shapes=[pltpu.VMEM((tm, D_MODEL), BF16)],
        compiler_params=_cparams(("parallel", "arbitrary")),
        name="q_proj",
    )(x, g.reshape(1, -1), wq_bf, c, s1, s2)


def _gate_proj_kernel(x_ref, g_ref, w_ref, o_ref):
    xn = _rms(x_ref[...], g_ref[...]).astype(BF16)
    o_ref[...] = _sigmoid(_dot(xn, w_ref[...]))


def gate_proj(x, g, wg_bf):
    m = x.shape[0]
    tm = _row_tile(m)
    n = wg_bf.shape[1]
    return pl.pallas_call(
        _gate_proj_kernel,
        grid=(m // tm,),
        in_specs=[pl.BlockSpec((tm, D_MODEL), lambda i: (i, 0)),
                  pl.BlockSpec((1, D_MODEL), lambda i: (0, 0)),
                  pl.BlockSpec((D_MODEL, n), lambda i: (0, 0))],
        out_specs=pl.BlockSpec((tm, n), lambda i: (i, 0)),
        out_shape=jax.ShapeDtypeStruct((m, n), F32),
        compiler_params=_cparams(("parallel",)),
        name="gate_proj",
    )(x, g.reshape(1, -1), wg_bf)


def _rms_norm_kernel(x_ref, g_ref, o_ref):
    o_ref[...] = _rms(x_ref[...], g_ref[...])


def rms_norm(x, g):
    m = x.shape[0]
    tm = _row_tile(m)
    return pl.pallas_call(
        _rms_norm_kernel,
        grid=(m // tm,),
        in_specs=[pl.BlockSpec((tm, D_MODEL), lambda i: (i, 0)),
                  pl.BlockSpec((1, D_MODEL), lambda i: (0, 0))],
        out_specs=pl.BlockSpec((tm, D_MODEL), lambda i: (i, 0)),
        out_shape=jax.ShapeDtypeStruct((m, D_MODEL), F32),
        compiler_params=_cparams(("parallel",)),
        name="rms_norm",
    )(x, g.reshape(1, -1))


def _mlp_kernel(x_ref, gpre_ref, wup_ref, wdn_ref, gpost_ref, o_ref, xn_ref, acc_ref):
    j = pl.program_id(1)

    @pl.when(j == 0)
    def _():
        xn_ref[...] = _rms(x_ref[...], gpre_ref[...]).astype(BF16)
        acc_ref[...] = jnp.zeros_like(acc_ref)

    h = jnp.maximum(_dot(xn_ref[...], wup_ref[...]), 0.0)
    acc_ref[...] += _dot((h * h).astype(BF16), wdn_ref[...])

    @pl.when(j == pl.num_programs(1) - 1)
    def _():
        o_ref[...] = x_ref[...] + _rms(acc_ref[...], gpost_ref[...])


def mlp(x, gpre, wup_bf, wdn_bf, gpost):
    m = x.shape[0]
    tm = _row_tile(m)
    tf = 1024
    return pl.pallas_call(
        _mlp_kernel,
        grid=(m // tm, D_FF // tf),
        in_specs=[pl.BlockSpec((tm, D_MODEL), lambda i, j: (i, 0)),
                  pl.BlockSpec((1, D_MODEL), lambda i, j: (0, 0)),
                  pl.BlockSpec((D_MODEL, tf), lambda i, j: (0, j)),
                  pl.BlockSpec((tf, D_MODEL), lambda i, j: (j, 0)),
                  pl.BlockSpec((1, D_MODEL), lambda i, j: (0, 0))],
        out_specs=pl.BlockSpec((tm, D_MODEL), lambda i, j: (i, 0)),
        out_shape=jax.ShapeDtypeStruct((m, D_MODEL), F32),
        scratch_shapes=[pltpu.VMEM((tm, D_MODEL), BF16), pltpu.VMEM((tm, D_MODEL), F32)],
        compiler_params=_cparams(("parallel", "arbitrary")),
        name="mlp",
    )(x, gpre.reshape(1, -1), wup_bf, wdn_bf, gpost.reshape(1, -1))


def _oproj_kernel(o_ref, w_ref, g_ref, res_ref, out_ref):
    out_ref[...] = res_ref[...] + _rms(_dot(o_ref[...], w_ref[...]), g_ref[...])


def oproj(o_bf, w_bf, g, res):
    m = o_bf.shape[0]
    tm = _row_tile(m)
    return pl.pallas_call(
        _oproj_kernel,
        grid=(m // tm,),
        in_specs=[pl.BlockSpec((tm, Q_WIDTH), lambda i: (i, 0)),
                  pl.BlockSpec((Q_WIDTH, D_MODEL), lambda i: (0, 0)),
                  pl.BlockSpec((1, D_MODEL), lambda i: (0, 0)),
                  pl.BlockSpec((tm, D_MODEL), lambda i: (i, 0))],
        out_specs=pl.BlockSpec((tm, D_MODEL), lambda i: (i, 0)),
        out_shape=jax.ShapeDtypeStruct((m, D_MODEL), F32),
        compiler_params=_cparams(("parallel",)),
        name="oproj",
    )(o_bf, w_bf, g.reshape(1, -1), res)


def _glu_kernel(x_ref, y_ref, gpre_ref, d_ref, w_ref, b_ref, gpost_ref, o_ref):
    x = x_ref[...]
    xn = _rms(x, gpre_ref[...])
    y = _gelu(y_ref[...] + d_ref[...] * xn)
    z = _dot(y.astype(BF16), w_ref[...]) + b_ref[...]
    o_ref[...] = x + _rms(y * _sigmoid(z), gpost_ref[...])


def glu_tail(x, y_ssm, gpre, d_skip, w_bf, b, gpost):
    m = x.shape[0]
    tm = _row_tile(m)
    vec = pl.BlockSpec((1, D_MODEL), lambda i: (0, 0))
    row = pl.BlockSpec((tm, D_MODEL), lambda i: (i, 0))
    return pl.pallas_call(
        _glu_kernel,
        grid=(m // tm,),
        in_specs=[row, row, vec, vec, pl.BlockSpec((D_MODEL, D_MODEL), lambda i: (0, 0)), vec, vec],
        out_specs=row,
        out_shape=jax.ShapeDtypeStruct((m, D_MODEL), F32),
        compiler_params=_cparams(("parallel",)),
        name="glu_tail",
    )(x, y_ssm, gpre.reshape(1, -1), d_skip.reshape(1, -1), w_bf, b.reshape(1, -1), gpost.reshape(1, -1))


S5_GB = 8
S5_SW = S5_GB * 2 * SSM_STATE


def _s5_prep_kernel(lrl_ref, lil_ref, ldt_ref, lrc_ref, lic_ref, btr_ref, bti_ref, ccat_ref, ctr_ref, cti_ref,
                    wlag_ref, winj_ref, dec_ref, *wcar_refs, pads):
    t = S5_CHUNK
    dt = jnp.exp(ldt_ref[...])
    lane = lax.broadcasted_iota(jnp.int32, (1, 1, HEAD_DIM), 2)
    re_half = lane < SSM_STATE
    lr, li = lrl_ref[...], lil_ref[...]
    ldr, ldi = lr * dt, li * dt
    mag = jnp.exp(ldr)
    a_re, a_im = mag * jnp.cos(ldi), mag * jnp.sin(ldi)
    den = lr * lr + li * li
    nr = a_re - 1.0
    f_re = (nr * lr + a_im * li) / den
    f_im = (a_im * lr - nr * li) / den
    bt_re = jnp.concatenate([btr_ref[...]] * t, axis=1)
    bt_im = jnp.concatenate([bti_ref[...]] * t, axis=1)
    bb_re = f_re * bt_re - f_im * bt_im
    bb_im = f_re * bt_im + f_im * bt_re
    k = (lax.broadcasted_iota(jnp.int32, (1, t * SSM_GROUP, 1), 1) // SSM_GROUP).astype(F32)
    mk = jnp.exp(ldr * k)
    ak_re, ak_im = mk * jnp.cos(ldi * k), mk * jnp.sin(ldi * k)
    xc = jnp.where(re_half, ak_re * bb_re - ak_im * bb_im, ak_re * bb_im + ak_im * bb_re)
    kt = jnp.einsum('gnp,gpl->gnl', xc, ccat_ref[...], precision=lax.Precision.HIGHEST,
                    preferred_element_type=F32)
    own = lane // SSM_GROUP
    winj_ref[...] = jnp.zeros_like(winj_ref)
    for g in range(S5_GB):
        kg = jnp.where(own[0] == g, kt[g], 0.0).astype(BF16)
        xg = xc[g].astype(BF16)
        for s in range(t):
            src = slice((t - 1 - s) * SSM_GROUP, (t - s) * SSM_GROUP)
            dst = slice(s * HEAD_DIM + g * SSM_GROUP, s * HEAD_DIM + (g + 1) * SSM_GROUP)
            wlag_ref[0, dst, :] = kg[src]
            winj_ref[0, dst, g * HEAD_DIM:(g + 1) * HEAD_DIM] = xg[src]
    for n, pad in enumerate(pads):
        nn = float(t - pad)
        mn = jnp.exp(ldr * nn)
        an_re, an_im = mn * jnp.cos(ldi * nn), mn * jnp.sin(ldi * nn)
        dec_ref[:, 2 * n:2 * n + 1, :] = an_re
        dec_ref[:, 2 * n + 1:2 * n + 2, :] = jnp.where(re_half, -an_im, an_im)
    lrc, lic = lrc_ref[...], lic_ref[...]
    magc = jnp.exp(lrc * dt)
    ac_re, ac_im = magc * jnp.cos(lic * dt), magc * jnp.sin(lic * dt)
    top = lax.broadcasted_iota(jnp.int32, (1, HEAD_DIM, 1), 1) < SSM_STATE
    ctr, cti = ctr_ref[...], cti_ref[...]
    for ref, pad in zip(wcar_refs, pads):
        if pad:
            ref[...] = jnp.zeros_like(ref)
    p_re, p_im = ac_re, ac_im
    for e in range(t):
        w = jnp.where(top, ctr * p_re - cti * p_im, -(ctr * p_im + cti * p_re))
        for g in range(S5_GB):
            wg = jnp.where(own[0] == g, w[g], 0.0).astype(BF16)
            for ref, pad in zip(wcar_refs, pads):
                if e + pad < t:
                    ref[0, e + pad, g * HEAD_DIM:(g + 1) * HEAD_DIM, :] = wg
        p_re, p_im = p_re * ac_re - p_im * ac_im, p_re * ac_im + p_im * ac_re


def s5_prep(lam_re, lam_im, log_dt, b_re, b_im, c_re, c_im, pads):
    g, p = lam_re.shape
    gb, t = S5_GB, S5_CHUNK
    n_blk = g // gb
    twice = lambda a, axis: jnp.concatenate([a, a], axis=axis)
    lanes = lambda a: twice(a, 1).reshape(g, 1, 2 * p)
    cols = lambda a: twice(a, 1).reshape(g, 2 * p, 1)
    bt = lambda a: twice(a.transpose(0, 2, 1), 2)
    crt, cit = c_re.transpose(0, 2, 1), c_im.transpose(0, 2, 1)
    rep = (jnp.arange(SSM_GROUP)[:, None] == jnp.arange(HEAD_DIM)[None, :] % SSM_GROUP).astype(F32)
    over_h = lambda a: jnp.einsum('gpi,il->gpl', a, rep, precision=lax.Precision.HIGHEST)
    ccat = over_h(jnp.concatenate([crt, -cit], axis=1))
    blk3 = lambda s: pl.BlockSpec((gb,) + s, lambda i: (i, 0, 0))
    sq = blk3((2 * p, HEAD_DIM))
    car_spec = pl.BlockSpec((1, t, gb * HEAD_DIM, HEAD_DIM), lambda i: (i, 0, 0, 0))
    car_shape = jax.ShapeDtypeStruct((n_blk, t, gb * HEAD_DIM, HEAD_DIM), BF16)
    outs = pl.pallas_call(
        functools.partial(_s5_prep_kernel, pads=pads),
        grid=(n_blk,),
        in_specs=[blk3((1, 2 * p)), blk3((1, 2 * p)), blk3((1, 1)), blk3((2 * p, 1)), blk3((2 * p, 1)),
                  blk3((SSM_GROUP, 2 * p)), blk3((SSM_GROUP, 2 * p)), sq, sq, sq],
        out_specs=[pl.BlockSpec((1, t * HEAD_DIM, HEAD_DIM), lambda i: (i, 0, 0)),
                   pl.BlockSpec((1, t * HEAD_DIM, S5_SW), lambda i: (i, 0, 0)),
                   blk3((2 * len(pads), 2 * p))] + [car_spec] * len(pads),
        out_shape=[jax.ShapeDtypeStruct((n_blk, t * HEAD_DIM, HEAD_DIM), BF16),
                   jax.ShapeDtypeStruct((n_blk, t * HEAD_DIM, S5_SW), BF16),
                   jax.ShapeDtypeStruct((g, 2 * len(pads), 2 * p), F32)] + [car_shape] * len(pads),
        compiler_params=_cparams(("parallel",)),
        name="s5_prep",
    )(lanes(lam_re), lanes(lam_im), log_dt.reshape(g, 1, 1), cols(lam_re), cols(lam_im), bt(b_re), bt(b_im),
      ccat, over_h(twice(crt, 1)), over_h(twice(cit, 1)))
    w_lag, w_inj, dec = outs[:3]
    dec = dec.reshape(n_blk, gb, 2 * len(pads), 2 * p).transpose(2, 0, 1, 3).reshape(2 * len(pads), n_blk, 1, S5_SW)
    return dict(w_lag=w_lag, w_inj=w_inj,
                w_car={pad: outs[3 + n] for n, pad in enumerate(pads)},
                a1={pad: dec[2 * n] for n, pad in enumerate(pads)},
                a2={pad: dec[2 * n + 1] for n, pad in enumerate(pads)})


def _swap_halves(x):
    lane = lax.broadcasted_iota(jnp.int32, (1, x.shape[-1]), 1)
    return jnp.where(lane % (2 * SSM_STATE) < SSM_STATE,
                     pltpu.roll(x, x.shape[-1] - SSM_STATE, x.ndim - 1), pltpu.roll(x, SSM_STATE, x.ndim - 1))


def _s5_block_kernel(x_ref, wlag_ref, winj_ref, wcar_ref, a1_ref, a2_ref, a2s_ref, h0_ref, y_ref, hf_ref,
                     xc_ref, s_ref, ss_ref, hp_ref, *, nbk, nc):
    mc = nbk * nc
    t = S5_CHUNK
    for s in range(t):
        xc_ref[:, s * HEAD_DIM:(s + 1) * HEAD_DIM] = x_ref[pl.ds(s, mc, stride=t), :].astype(BF16)
    inc = _dot(xc_ref[...], winj_ref[0])
    s_ref[...] = inc
    ss_ref[...] = _swap_halves(inc)
    a1, a2, a2s = a1_ref[0], a2_ref[0], a2s_ref[0]
    h0 = tuple((h0_ref[0, 0, b:b + 1, :], _swap_halves(h0_ref[0, 0, b:b + 1, :])) for b in range(nbk))

    def step(c, hs):
        out = []
        for b in range(nbk):
            h, hx = hs[b]
            row = b * nc + c
            hp_ref[pl.ds(row, 1), :] = h
            out.append((a1 * h + a2 * hx + s_ref[pl.ds(row, 1), :],
                        a1 * hx + a2s * h + ss_ref[pl.ds(row, 1), :]))
        return tuple(out)

    hs = lax.fori_loop(0, nc, step, h0)
    for b in range(nbk):
        hf_ref[0, 0, b:b + 1, :] = hs[b][0]
    hp = hp_ref[...].astype(BF16)
    for tt in range(t):
        y = (_dot(xc_ref[:, 0:(tt + 1) * HEAD_DIM], wlag_ref[0, (t - 1 - tt) * HEAD_DIM:, :])
             + _dot(hp, wcar_ref[0, tt]))
        y_ref[pl.ds(tt, mc, stride=t), :] = y


def s5_block(xn, w_lag, w_inj, w_car, a1, a2, a2s, h0, nbk, nc):
    m = xn.shape[0]
    rows = nbk * nc * S5_CHUNK
    n_step = m // rows
    n_blk = N_GROUPS // S5_GB
    kdim = S5_CHUNK * HEAD_DIM
    wspec = lambda a: pl.BlockSpec((1,) + a.shape[1:], lambda g, b: (g,) + (0,) * (a.ndim - 1))
    hspec = pl.BlockSpec((1, 1, nbk, S5_SW), lambda g, b: (g, b, 0, 0))
    xspec = pl.BlockSpec((rows, HEAD_DIM), lambda g, b: (b, g))
    return pl.pallas_call(
        functools.partial(_s5_block_kernel, nbk=nbk, nc=nc),
        grid=(n_blk, n_step),
        in_specs=[xspec, wspec(w_lag), wspec(w_inj), wspec(w_car), wspec(a1), wspec(a2), wspec(a2s), hspec],
        out_specs=[xspec, hspec],
        out_shape=[jax.ShapeDtypeStruct((m, D_MODEL), F32), jax.ShapeDtypeStruct(h0.shape, F32)],
        scratch_shapes=[pltpu.VMEM((nbk * nc, kdim), BF16)] + [pltpu.VMEM((nbk * nc, S5_SW), F32)] * 3,
        compiler_params=_cparams(("parallel", "arbitrary")),
        name="s5_block",
    )(xn, w_lag, w_inj, w_car, a1, a2, a2s, h0)


CMP_PAGES = 16
CMP_ROWS = CMP_PAGES * (PAGE // CMP_STRIDE)
CMP_FLAT = CMP_STRIDE * HEAD_DIM


PAGE_CHUNKS = KV_SLAB // HEAD_DIM
PAGE_ROWS = PAGE * PAGE_CHUNKS


def _page_specs(n, first):
    def spec(k):
        return pl.BlockSpec((1, PAGE_ROWS, HEAD_DIM), lambda b, p, pt: (pt[b, first(p) + k], 0, 0))
    return [spec(k) for k in range(n)]


def _cmp_proj_kernel(pt_ref, *refs):
    x_refs = refs[:CMP_PAGES]
    wk_ref, wv_ref, o_ref, lhs_ref = refs[CMP_PAGES:]
    per_page = PAGE // CMP_STRIDE
    for pg, x_ref in enumerate(x_refs):
        for kv in range(2):
            for g in range(N_KV):
                c = kv * N_KV + g
                for s in range(CMP_STRIDE):
                    piece = x_ref[0, pl.ds(s * PAGE_CHUNKS + c, per_page, stride=CMP_STRIDE * PAGE_CHUNKS), :]
                    lhs_ref[kv, g, pg * per_page:(pg + 1) * per_page, s * HEAD_DIM:(s + 1) * HEAD_DIM] = piece
    for kv, w_ref in ((0, wk_ref), (1, wv_ref)):
        for g in range(N_KV):
            o_ref[0, kv, g] = _dot(lhs_ref[kv, g].astype(BF16), w_ref[...])


def cmp_proj(pages, page_table, wk_cat, wv_cat):
    nb, npg = page_table.shape
    assert npg % CMP_PAGES == 0, "compression consumes whole groups of pages"
    n_ch = npg * (PAGE // CMP_STRIDE)
    wspec = pl.BlockSpec((CMP_FLAT, 2 * CMP_HIDDEN), lambda b, p, pt: (0, 0))
    grid_spec = pltpu.PrefetchScalarGridSpec(
        num_scalar_prefetch=1,
        grid=(nb, npg // CMP_PAGES),
        in_specs=_page_specs(CMP_PAGES, lambda p: p * CMP_PAGES) + [wspec, wspec],
        out_specs=pl.BlockSpec((1, 2, N_KV, CMP_ROWS, 2 * CMP_HIDDEN), lambda b, p, pt: (b, 0, 0, p, 0)),
        scratch_shapes=[pltpu.VMEM((2, N_KV, CMP_ROWS, CMP_FLAT), F32)],
    )
    return pl.pallas_call(
        _cmp_proj_kernel,
        grid_spec=grid_spec,
        out_shape=jax.ShapeDtypeStruct((nb, 2, N_KV, n_ch, 2 * CMP_HIDDEN), F32),
        compiler_params=_cparams(("parallel", "arbitrary")),
        name="cmp_proj",
    )(page_table, *([pages] * CMP_PAGES), wk_cat, wv_cat)


def _cmp_mlp_kernel(p_ref, w1_ref, pe_ref, w2_ref, o_ref):
    proj = p_ref[0, 0, 0]
    n_ch = proj.shape[0]
    pre0 = jnp.sum(pe_ref[0] * w1_ref[0], axis=0, keepdims=True)
    first = proj[:, 0:CMP_HIDDEN]
    second = pltpu.roll(proj[:, CMP_HIDDEN:2 * CMP_HIDDEN], n_ch - 1, 0)
    pre = (pre0 + first) + second
    o_ref[0, 0, 0] = _dot(_gelu(pre).astype(BF16), w2_ref[0])


def cmp_mlp(proj, w1, pe, w2_bf):
    nb, _, _, n_ch, _ = proj.shape
    flat = CMP_BLOCK * HEAD_DIM
    return pl.pallas_call(
        _cmp_mlp_kernel,
        grid=(nb, 2, N_KV),
        in_specs=[pl.BlockSpec((1, 1, 1, n_ch, 2 * CMP_HIDDEN), lambda b, k, g: (b, k, g, 0, 0)),
                  pl.BlockSpec((1, flat, CMP_HIDDEN), lambda b, k, g: (k, 0, 0)),
                  pl.BlockSpec((1, flat, 1), lambda b, k, g: (k, 0, 0)),
                  pl.BlockSpec((1, CMP_HIDDEN, HEAD_DIM), lambda b, k, g: (k, 0, 0))],
        out_specs=pl.BlockSpec((1, 1, 1, n_ch, HEAD_DIM), lambda b, k, g: (b, k, g, 0, 0)),
        out_shape=jax.ShapeDtypeStruct((nb, 2, N_KV, n_ch, HEAD_DIM), F32),
        compiler_params=_cparams(("parallel", "parallel", "parallel")),
        name="cmp_mlp",
    )(proj, w1, pe, w2_bf)


def _kv_pack_kernel(pt_ref, *refs, per, n_groups):
    x_refs = refs[:per]
    t_ref, k_ref, vt_ref = refs[per:]
    j = pl.program_id(1)

    def emit(head):
        for g in range(N_KV):
            for pg in range(per):
                sl = slice(pg * PAGE, (pg + 1) * PAGE)
                k_ref[0, g, sl, :] = head(pg, g).astype(BF16)
                vt_ref[0, g, 0, :, sl] = head(pg, N_KV + g).T.astype(BF16)

    @pl.when(j < n_groups)
    def _():
        emit(lambda pg, c: x_refs[pg][0, pl.ds(c, PAGE, stride=PAGE_CHUNKS), :])

    @pl.when(j >= n_groups)
    def _():
        emit(lambda pg, c: t_ref[0, pl.ds(pg * PAGE_ROWS + c, PAGE, stride=PAGE_CHUNKS), :])


def kv_pack(pages, page_table, tail, vt_tile):
    nb, n_pages = page_table.shape
    per = vt_tile // PAGE
    assert n_pages % per == 0
    n_groups = n_pages // per
    if tail is None:
        n_tail = 0
        tail = jnp.zeros((nb, per * PAGE_ROWS, HEAD_DIM), F32)
    else:
        n_tail = tail.shape[1] // vt_tile
        tail = tail.reshape(nb, n_tail * per * PAGE_ROWS, HEAD_DIM)
    n_tot = n_groups + n_tail
    grid_spec = pltpu.PrefetchScalarGridSpec(
        num_scalar_prefetch=1,
        grid=(nb, n_tot),
        in_specs=_page_specs(per, lambda p: jnp.minimum(p, n_groups - 1) * per)
        + [pl.BlockSpec((1, per * PAGE_ROWS, HEAD_DIM), lambda b, p, pt: (b, jnp.maximum(p - n_groups, 0), 0))],
        out_specs=[pl.BlockSpec((1, N_KV, vt_tile, HEAD_DIM), lambda b, p, pt: (b, 0, p, 0)),
                   pl.BlockSpec((1, N_KV, 1, HEAD_DIM, vt_tile), lambda b, p, pt: (b, 0, p, 0, 0))],
    )
    return pl.pallas_call(
        functools.partial(_kv_pack_kernel, per=per, n_groups=n_groups),
        grid_spec=grid_spec,
        out_shape=[jax.ShapeDtypeStruct((nb, N_KV, n_tot * vt_tile, HEAD_DIM), BF16),
                   jax.ShapeDtypeStruct((nb, N_KV, n_tot, HEAD_DIM, vt_tile), BF16)],
        compiler_params=_cparams(("parallel", "arbitrary")),
        name="kv_pack",
    )(page_table, *([pages] * per), tail)


def _split3(x):
    hi = x.astype(BF16)
    r1 = x - hi.astype(F32)
    mid = r1.astype(BF16)
    lo = (r1 - mid.astype(F32)).astype(BF16)
    return hi, mid, lo


def _softmax_cols(s, col_ok):
    m = jnp.max(s, axis=0, keepdims=True)
    e = jnp.exp(s - m)
    den = jnp.sum(e, axis=0, keepdims=True)
    return e * jnp.where(col_ok, 1.0 / den, 0.0)


def _nsa_kernel(q_ref, qr_ref, gt_ref, kc_ref, vct_ref, ks_ref, vst_ref, kw_ref, vwt_ref, o_ref,
                bias_ref, s_ref, m_ref, l_ref, acc_ref, *, tq, q_off, swa_base):
    rq = GROUP_Q * tq
    w = max(tq, min(rq, HEAD_DIM))
    span = WINDOW + max(tq, PAGE)
    reps = rq // w
    n_cp = kc_ref.shape[2]
    n_tiles = bias_ref.shape[0]
    per = SLC_TILE // SLC_BLOCK
    n_blk = n_tiles * per
    tw = kw_ref.shape[2]
    i = pl.program_id(2)
    t0 = q_off + i * tq
    t_w = t0 + lax.broadcasted_iota(jnp.int32, (1, w), 1) % tq

    def tile(x):
        return jnp.concatenate([x] * reps, axis=x.ndim - 1) if reps > 1 else x

    def rows(ref):
        x = ref[...]
        return jnp.concatenate([x[:, r * HEAD_DIM:(r + 1) * HEAD_DIM] for r in range(GROUP_Q)], axis=0)

    q2 = rows(q_ref)
    qr2 = rows(qr_ref)
    t_rq = tile(t_w)

    n_io = lax.broadcasted_iota(jnp.int32, (n_cp, 1), 0)
    bias_c = jnp.where((n_io * CMP_STRIDE + (CMP_BLOCK - 1)) <= t_w, 0.0, NEG)
    p_c = _softmax_cols(_dot_nt(kc_ref[0, 0], q2) + tile(bias_c), t_rq >= CMP_BLOCK - 1)
    o_c = _dot(vct_ref[0, 0], p_c.astype(BF16))
    if reps > 1:
        p_sum = p_c[:, 0:w]
        for r in range(1, reps):
            p_sum = p_sum + p_c[:, r * w:(r + 1) * w]
    else:
        p_sum = p_c
        for r in range(1, GROUP_Q):
            p_sum = p_sum + pltpu.roll(p_c, r * tq, 1)
    s_col = lax.broadcasted_iota(jnp.int32, (n_blk, 1), 0)
    n_row = lax.broadcasted_iota(jnp.int32, (1, n_cp), 1)
    ov = jnp.logical_and(n_row * CMP_STRIDE < (s_col + 1) * SLC_BLOCK,
                         n_row * CMP_STRIDE + CMP_BLOCK > s_col * SLC_BLOCK)
    ov = jnp.where(ov, 1.0, 0.0).astype(BF16)
    hi, mid, lo = _split3(p_sum)
    imp = (_dot(ov, hi) + _dot(ov, mid)) + _dot(ov, lo)
    cur = t_w // SLC_BLOCK
    forced = jnp.logical_or(s_col == 0, jnp.logical_or(s_col == cur, s_col == cur - 1))
    elig = s_col * SLC_BLOCK <= t_w
    imp = jnp.where(forced, FORCE_SCORE, imp)
    imp = jnp.where(elig, imp, -jnp.inf)

    s_colf = s_col.astype(F32)

    def pick(_, carry):
        work, sel = carry
        best = jnp.max(work, axis=0, keepdims=True)
        first = jnp.min(jnp.where(work == best, s_colf, float(n_blk)), axis=0, keepdims=True)
        hit = s_colf == first
        return jnp.where(hit, -jnp.inf, work), jnp.where(hit, 1.0, sel)

    _, sel = lax.fori_loop(0, N_SELECT, pick, (imp, jnp.zeros((n_blk, w), F32)))
    bias_s = jnp.where(jnp.logical_and(elig, sel > 0.5), 0.0, NEG).reshape(n_tiles, per, w)
    bias_s = jnp.concatenate([bias_s, jnp.zeros_like(bias_s)], axis=1).astype(BF16)
    bias_ref[...] = tile(bias_s)

    m_ref[...] = jnp.full_like(m_ref, NEG)
    l_ref[...] = jnp.zeros_like(l_ref)
    acc_ref[...] = jnp.zeros_like(acc_ref)
    qr_t = qr2.astype(F32).T.astype(BF16)
    k_io = lax.broadcasted_iota(jnp.int32, (SLC_TILE, 1), 0)
    c_io = lax.broadcasted_iota(jnp.int32, (1, HEAD_DIM), 1)
    onehot = jnp.where(k_io // SLC_BLOCK == c_io, 1.0, 0.0).astype(BF16)
    zpad = jnp.zeros((HEAD_DIM - 2 * per, rq), BF16)

    def scores(kt):
        base = pl.multiple_of(kt * SLC_TILE, SLC_TILE)
        lhs = jnp.concatenate([ks_ref[0, 0, pl.ds(base, SLC_TILE), :], onehot], axis=1)
        rhs = jnp.concatenate([qr_t, bias_ref[kt], zpad], axis=0)
        return _dot(lhs, rhs)

    def update(kt, ss):
        m_old = m_ref[...]
        m_new = m_old
        for s in ss:
            m_new = jnp.maximum(m_new, jnp.max(s, axis=0, keepdims=True))
        alpha = jnp.exp(m_old - m_new)
        l_new = alpha * l_ref[...]
        acc = alpha * acc_ref[...]
        for j, s in enumerate(ss):
            e = jnp.exp(s - m_new)
            l_new = l_new + jnp.sum(e, axis=0, keepdims=True)
            acc = acc + _dot(vst_ref[0, 0, kt + j], e.astype(BF16))
        l_ref[...] = l_new
        acc_ref[...] = acc
        m_ref[...] = m_new

    k_last = t0 // SLC_TILE
    causal = tile(jnp.where((k_last * SLC_TILE + k_io) <= t_w, 0.0, NEG))
    if rq >= SLC_TILE:
        sa_ref, sb_ref = s_ref.at[0], s_ref.at[1]
        sa_ref[...] = scores(0)

        def body(i, carry):
            kt = 2 * i
            sb_ref[...] = scores(kt + 1)
            update(kt, [sa_ref[...]])
            sa_ref[...] = scores(kt + 2)
            update(kt + 1, [sb_ref[...]])
            return carry

        n_pair = k_last // 2
        lax.fori_loop(0, n_pair, body, 0)

        @pl.when(k_last % 2 == 0)
        def _():
            update(k_last, [sa_ref[...] + causal])

        @pl.when(k_last % 2 == 1)
        def _():
            sb_ref[...] = scores(k_last)
            update(k_last - 1, [sa_ref[...]])
            update(k_last, [sb_ref[...] + causal])
    else:
        joint = SLC_TILE // rq
        n_step = k_last // joint

        def body(i, carry):
            update(i * joint, [scores(i * joint + j) for j in range(joint)])
            return carry

        lax.fori_loop(0, n_step, body, 0)

        def single(kt, carry):
            update(kt, [scores(kt)])
            return carry

        lax.fori_loop(n_step * joint, k_last, single, 0)
        update(k_last, [scores(k_last) + causal])
    o_s = acc_ref[...] * (1.0 / l_ref[...])

    start = jnp.clip(t0 - WINDOW - swa_base, 0, tw - span)
    start = pl.multiple_of(start, PAGE)
    key_pos = swa_base + start + lax.broadcasted_iota(jnp.int32, (span, 1), 0)
    dist = t_w - key_pos
    ok_w = jnp.logical_and(jnp.logical_and(dist >= 0, dist < WINDOW), key_pos >= swa_base)
    sw = _dot_nt(kw_ref[0, 0, pl.ds(start, span), :], qr2) + tile(jnp.where(ok_w, 0.0, NEG))
    p_w = _softmax_cols(sw, True).astype(BF16)
    o_w = jnp.zeros((HEAD_DIM, rq), F32)
    for jt in range(span // PAGE):
        o_w = o_w + _dot(vwt_ref[0, 0, start // PAGE + jt], p_w[jt * PAGE:(jt + 1) * PAGE, :])

    gt = gt_ref[0, 0, 0]
    o_t = o_c * gt[0:1, :] + o_s * gt[1:2, :] + o_w * gt[2:3, :]
    for c in range(rq // HEAD_DIM):
        blk = o_t[:, c * HEAD_DIM:(c + 1) * HEAD_DIM].T.astype(BF16)
        if tq >= HEAD_DIM:
            r, q0 = divmod(c * HEAD_DIM, tq)
            o_ref[q0:q0 + HEAD_DIM, r * HEAD_DIM:(r + 1) * HEAD_DIM] = blk
        else:
            per_blk = HEAD_DIM // tq
            for rr in range(per_blk):
                r = c * per_blk + rr
                o_ref[:, r * HEAD_DIM:(r + 1) * HEAD_DIM] = blk[rr * tq:(rr + 1) * tq, :]


def nsa_attention(q, qr, gates_t, kc, vct, ks, vst, kw, vwt, *, nb, nq, tq, q_off, swa_base):
    rq = GROUP_Q * tq
    tk = ks.shape[2]
    assert SLC_TILE % tq == 0 and q_off % SLC_TILE == 0 and tk % SLC_TILE == 0
    qspec = pl.BlockSpec((tq, GROUP_W), lambda b, g, i: (b * nq + i, g))
    full = lambda a: pl.BlockSpec((1, 1) + a.shape[2:], lambda b, g, i: (b, g) + (0,) * (a.ndim - 2))
    return pl.pallas_call(
        functools.partial(_nsa_kernel, tq=tq, q_off=q_off, swa_base=swa_base),
        grid=(nb, N_KV, nq),
        in_specs=[qspec, qspec,
                  pl.BlockSpec((1, 1, 1, N_BRANCH, rq), lambda b, g, i: (b, g, i, 0, 0)),
                  full(kc), full(vct), full(ks), full(vst), full(kw), full(vwt)],
        out_specs=qspec,
        out_shape=jax.ShapeDtypeStruct((nb * nq * tq, Q_WIDTH), BF16),
        scratch_shapes=[pltpu.VMEM((tk // SLC_TILE, 2 * (SLC_TILE // SLC_BLOCK), rq), BF16),
                        pltpu.VMEM((2, SLC_TILE, rq), F32),
                        pltpu.VMEM((1, rq), F32), pltpu.VMEM((1, rq), F32), pltpu.VMEM((HEAD_DIM, rq), F32)],
        compiler_params=_cparams(("parallel", "parallel", "arbitrary")),
        name="nsa_attention",
    )(q, qr, gates_t, kc, vct, ks, vst, kw, vwt)


def _s5_layer(x, h0_re, h0_im, lp, nb, seq):
    m = x.shape[0]
    nc = -(-seq // S5_CHUNK)
    pad = nc * S5_CHUNK - seq
    n_blk = N_GROUPS // S5_GB
    xn = rms_norm(x, lp['norm_pre'])
    if pad == 0:
        nbk, nbp, xs = 1, nb, xn
    else:
        nbp = -(-nb // S5_CHUNK) * S5_CHUNK
        nbk = nbp
        xs = jnp.pad(xn.reshape(nb, seq, D_MODEL), ((0, nbp - nb), (pad, 0), (0, 0))).reshape(-1, D_MODEL)
    n_step = nbp // nbk
    a1, a2 = lp['a1'][pad], lp['a2'][pad]
    if h0_re is None:
        h0 = jnp.zeros((n_blk, n_step, nbk, S5_SW), F32)
    else:
        h0 = jnp.stack([h0_re, h0_im], axis=2).reshape(nb, n_blk, S5_SW)
        h0 = jnp.pad(h0, ((0, nbp - nb), (0, 0), (0, 0))).transpose(1, 0, 2).reshape(n_blk, n_step, nbk, S5_SW)
    y, hf = s5_block(xs, lp['w_lag'], lp['w_inj'], lp['w_car'][pad], a1, a2, -a2, h0, nbk, nc)
    y = y.reshape(nbp, nc * S5_CHUNK, D_MODEL)[:nb, pad:].reshape(m, D_MODEL)
    hf = hf.reshape(n_blk, nbp, S5_GB, 2, SSM_STATE)[:, :nb].transpose(3, 1, 0, 2, 4).reshape(2, nb, N_GROUPS, SSM_STATE)
    x = glu_tail(x, y, lp['norm_pre'], lp['d'], lp['w_glu'], lp['b_glu'], lp['norm_post'])
    return x, hf[0], hf[1]


def _gates_t(gates, nb, nq, tq):
    g = gates[:, :N_HEADS * N_BRANCH].reshape(nb, nq, tq, N_KV, GROUP_Q, N_BRANCH)
    return g.transpose(0, 3, 1, 5, 4, 2).reshape(nb, N_KV, nq, N_BRANCH, GROUP_Q * tq)


def _pad_rows(a, nb, seq, tq):
    if seq == tq or seq % tq == 0:
        return a
    a = a.reshape(nb, seq, -1)
    return jnp.pad(a, ((0, 0), (0, tq - seq), (0, 0))).reshape(nb * tq, -1)


def _nsa_layer(x, lp, kvs, tables, nb, seq, tq, q_off, swa_base):
    q, qr = q_proj(x, lp['norm_pre'], lp['w_q'], tables)
    gates = gate_proj(x, lp['norm_pre'], lp['w_g'])
    nq = -(-seq // tq)
    o = nsa_attention(_pad_rows(q, nb, seq, tq), _pad_rows(qr, nb, seq, tq),
                      _gates_t(_pad_rows(gates, nb, seq, tq), nb, nq, tq),
                      *kvs, nb=nb, nq=nq, tq=tq, q_off=q_off, swa_base=swa_base)
    if nq * tq != seq:
        o = o.reshape(nb, nq * tq, Q_WIDTH)[:, :seq].reshape(nb * seq, Q_WIDTH)
    return oproj(o, lp['w_o'], lp['norm_post'], x)


def _trunk(x, nb, seq, pos0, h0_re, h0_im, past, prm):
    m = nb * seq
    ssm_re, ssm_im = [], []
    n_a = len(prm['a'])
    for layer in range(n_a):
        x, hr, hi = _s5_layer(x, None if h0_re is None else h0_re[layer],
                              None if h0_im is None else h0_im[layer], prm['a'][layer], nb, seq)
        ssm_re.append(hr)
        ssm_im.append(hi)
        ml = prm['mlp'][layer]
        x = mlp(x, ml['norm_pre'], ml['w_up'], ml['w_down'], ml['norm_post'])

    pos = pos0 + jnp.tile(jnp.arange(seq), nb)
    tables = _rope_tables(pos)
    kv = kv_proj(x, prm['kv_norm'], prm['w_kv'], tables)
    rows_cmp = kv[:, 0:KV_SLAB].reshape(nb, seq, 2, N_KV, HEAD_DIM)
    rows_slc = kv[:, KV_SLAB:2 * KV_SLAB].reshape(nb, seq, 2, N_KV, HEAD_DIM)
    rows_swa = kv[:, 2 * KV_SLAB:3 * KV_SLAB]
    no_tail = None
    if past is None:
        npg = seq // PAGE
        table = jnp.arange(nb * npg, dtype=jnp.int32).reshape(nb, npg)
        pages = lambda j: kv[:, j * KV_SLAB:(j + 1) * KV_SLAB].reshape(nb * npg, PAGE_ROWS, HEAD_DIM)
        cproj = cmp_proj(pages(0), table, prm['cmp_w1k_cat'], prm['cmp_w1v_cat'])
        ks, vst = kv_pack(pages(1), table, no_tail, SLC_TILE)
        kw, vwt = kv_pack(pages(2), table, no_tail, PAGE)
        swa_base = 0
        tq = 256
        swa_buf = rows_swa.reshape(nb, seq, 2, N_KV, HEAD_DIM)[:, seq - WINDOW:]
    else:
        cache_cmp, cache_slc, state_swa, table = past
        n_pool = cache_cmp.shape[0]
        npg = table.shape[1]
        cproj = cmp_proj(cache_cmp.reshape(n_pool, PAGE_ROWS, HEAD_DIM), table, prm['cmp_w1k_cat'], prm['cmp_w1v_cat'])
        t_real = npg * PAGE + seq
        t_pad = -(-t_real // SLC_TILE) * SLC_TILE
        tail = jnp.pad(kv[:, KV_SLAB:2 * KV_SLAB].reshape(nb, seq, KV_SLAB),
                       ((0, 0), (0, t_pad - npg * PAGE - seq), (0, 0)))
        ks, vst = kv_pack(cache_slc.reshape(n_pool, PAGE_ROWS, HEAD_DIM), table, tail, SLC_TILE)
        w_keep = state_swa.shape[1]
        local = jnp.concatenate([state_swa.reshape(nb, w_keep, KV_SLAB), rows_swa.reshape(nb, seq, KV_SLAB)], axis=1)
        swa_buf = local[:, -w_keep:].reshape(nb, w_keep, 2, N_KV, HEAD_DIM)
        tw = -(-max(local.shape[1], SWA_SPAN) // PAGE) * PAGE
        local = jnp.pad(local, ((0, 0), (0, tw - local.shape[1]), (0, 0)))
        lt = jnp.arange(nb * (tw // PAGE), dtype=jnp.int32).reshape(nb, tw // PAGE)
        kw, vwt = kv_pack(local.reshape(nb * (tw // PAGE), PAGE_ROWS, HEAD_DIM), lt, no_tail, PAGE)
        swa_base = pos0 - w_keep
        tq = 32
    cmp_out = cmp_mlp(cproj, prm['cmp_w1'], prm['cmp_pe'], prm['cmp_w2'])
    kc = cmp_out[:, 0].astype(BF16)
    vct = cmp_out[:, 1].transpose(0, 1, 3, 2).astype(BF16)
    kvs = (kc, vct, ks, vst, kw, vwt)

    for j, lp in enumerate(prm['b']):
        x = _nsa_layer(x, lp, kvs, tables, nb, seq, tq, pos0, swa_base)
        ml = prm['mlp'][n_a + j]
        x = mlp(x, ml['norm_pre'], ml['w_up'], ml['w_down'], ml['norm_post'])
    return x, rows_cmp, rows_slc, swa_buf, jnp.stack(ssm_re), jnp.stack(ssm_im)


def kernel(x_prompt, x_sample, cache_kv_cmp, cache_kv_slc, state_kv_swa, state_ssm_re, state_ssm_im, page_table,
           a_norm_pre, a_lam_re, a_lam_im, a_log_dt, a_b_re, a_b_im, a_c_re, a_c_im, a_d, a_w_glu, a_b_glu,
           a_norm_post, kv_norm, w_kv, cmp_w1_k, cmp_pe_k, cmp_w2_k, cmp_w1_v, cmp_pe_v, cmp_w2_v, b_norm_pre,
           b_w_qg, b_w_o, b_norm_post, mlp_norm_pre, mlp_w_up, mlp_w_down, mlp_norm_post):
    n_a = a_norm_pre.shape[0]
    n_b = b_norm_pre.shape[0]
    prm = {'a': [], 'b': [], 'mlp': []}
    pads = tuple(sorted({-x_prompt.shape[1] % S5_CHUNK, -x_sample.shape[1] % S5_CHUNK}))
    for l in range(n_a):
        ops = s5_prep(a_lam_re[l], a_lam_im[l], a_log_dt[l], a_b_re[l], a_b_im[l], a_c_re[l], a_c_im[l], pads)
        prm['a'].append(dict(norm_pre=a_norm_pre[l], norm_post=a_norm_post[l], d=a_d[l],
                             w_glu=a_w_glu[l].astype(BF16), b_glu=a_b_glu[l], **ops))
    n_gate = N_HEADS * N_BRANCH
    for l in range(n_b):
        w_g = jnp.pad(b_w_qg[l][:, Q_WIDTH:], ((0, 0), (0, HEAD_DIM - n_gate))).astype(BF16)
        prm['b'].append(dict(norm_pre=b_norm_pre[l], norm_post=b_norm_post[l],
                             w_q=b_w_qg[l][:, :Q_WIDTH].astype(BF16), w_g=w_g, w_o=b_w_o[l].astype(BF16)))
    for l in range(n_a + n_b):
        prm['mlp'].append(dict(norm_pre=mlp_norm_pre[l], norm_post=mlp_norm_post[l],
                               w_up=mlp_w_up[l].astype(BF16), w_down=mlp_w_down[l].astype(BF16)))
    prm['kv_norm'] = kv_norm
    prm['w_kv'] = w_kv.astype(BF16)
    r = CMP_BLOCK // CMP_STRIDE
    cat = lambda w: w.reshape(r, CMP_FLAT, CMP_HIDDEN).transpose(1, 0, 2).reshape(CMP_FLAT, r * CMP_HIDDEN).astype(BF16)
    prm['cmp_w1k_cat'] = cat(cmp_w1_k)
    prm['cmp_w1v_cat'] = cat(cmp_w1_v)
    flat = CMP_BLOCK * HEAD_DIM
    prm['cmp_w1'] = jnp.stack([cmp_w1_k.reshape(flat, CMP_HIDDEN), cmp_w1_v.reshape(flat, CMP_HIDDEN)])
    prm['cmp_pe'] = jnp.stack([cmp_pe_k.reshape(flat, 1), cmp_pe_v.reshape(flat, 1)])
    prm['cmp_w2'] = jnp.stack([cmp_w2_k, cmp_w2_v]).astype(BF16)

    bp, sp, _ = x_prompt.shape
    y_p, cmp_p, slc_p, swa_p, re_p, im_p = _trunk(x_prompt.reshape(bp * sp, D_MODEL), bp, sp, 0,
                                                  None, None, None, prm)
    bs, ss, _ = x_sample.shape
    past_len = page_table.shape[1] * PAGE
    y_s, cmp_s, slc_s, swa_s, re_s, im_s = _trunk(x_sample.reshape(bs * ss, D_MODEL), bs, ss, past_len,
                                                  state_ssm_re, state_ssm_im,
                                                  (cache_kv_cmp, cache_kv_slc, state_kv_swa, page_table), prm)
    return (y_p.reshape(bp, sp, D_MODEL), y_s.reshape(bs, ss, D_MODEL), cmp_p, cmp_s, slc_p, slc_s,
            swa_p, swa_s, re_p, im_p, re_s, im_s)
```

```python
import functools
import math

import jax
import jax.numpy as jnp
from jax import lax
from jax.experimental import pallas as pl
from jax.experimental.pallas import tpu as pltpu

F32 = jnp.float32
BF16 = jnp.bfloat16

D_MODEL = 2048
N_HEADS = 16
HEAD_DIM = 128
N_KV = 4
GROUP_Q = N_HEADS // N_KV
N_BRANCH = 3
ROT_DIM = HEAD_DIM // 4
ROPE_THETA = 500000.0
SSM_GROUP = 16
N_GROUPS = D_MODEL // SSM_GROUP
SSM_STATE = 64
S5_CHUNK = 16
D_FF = 4 * D_MODEL
CMP_BLOCK = 32
CMP_STRIDE = 16
CMP_HIDDEN = 2 * HEAD_DIM
SLC_BLOCK = 64
N_SELECT = 16
WINDOW = 512
PAGE = 128
FORCE_SCORE = 1.0e4
EPS = 1e-6
Q_WIDTH = N_HEADS * HEAD_DIM
GROUP_W = GROUP_Q * HEAD_DIM
KV_SLAB = 2 * N_KV * HEAD_DIM
PAGE_CHUNKS = KV_SLAB // HEAD_DIM
PAGE_ROWS = PAGE * PAGE_CHUNKS
SLC_TILE = 512
SWA_SPAN = WINDOW + PAGE
NEG = -1.0e30
VMEM_LIMIT = 56 * 1024 * 1024


def _cparams(sem):
    return pltpu.CompilerParams(dimension_semantics=sem, vmem_limit_bytes=VMEM_LIMIT)


def _rms(x, g):
    var = jnp.mean(x * x, axis=-1, keepdims=True)
    return x * lax.rsqrt(var + EPS) * g


def _gelu(x):
    return 0.5 * x * (1.0 + jnp.tanh(math.sqrt(2.0 / math.pi) * (x + 0.044715 * (x * x * x))))


def _sigmoid(x):
    return 1.0 / (1.0 + jnp.exp(-x))


def _dot(a, b):
    return jnp.dot(a, b, preferred_element_type=F32)


def _dot_nt(a, b):
    return lax.dot_general(a, b, (((1,), (1,)), ((), ())), preferred_element_type=F32)


def _rope128(x, c, s1, s2):
    return x * c + pltpu.roll(x, HEAD_DIM - ROT_DIM // 2, 1) * s1 + pltpu.roll(x, ROT_DIM // 2, 1) * s2


def _rope_tables(pos):
    half = ROT_DIM // 2
    inv = ROPE_THETA ** (-jnp.arange(half, dtype=F32) / half)
    ang = pos.astype(F32)[:, None] * inv[None, :]
    cos, sin = jnp.cos(ang), jnp.sin(ang)
    n = pos.shape[0]
    rest = HEAD_DIM - ROT_DIM
    c = jnp.concatenate([cos, cos, jnp.ones((n, rest), F32)], axis=1)
    s1 = jnp.concatenate([-sin, jnp.zeros((n, HEAD_DIM - half), F32)], axis=1)
    s2 = jnp.concatenate([jnp.zeros((n, half), F32), sin, jnp.zeros((n, rest), F32)], axis=1)
    return c, s1, s2


def _row_tile(m):
    return 512 if m % 512 == 0 else m


def _kv_proj_kernel(x_ref, g_ref, w_ref, c_ref, s1_ref, s2_ref, o_ref, xn_ref):
    j = pl.program_id(1)
    tm = x_ref.shape[0]

    @pl.when(j == 0)
    def _():
        xn_ref[...] = _rms(x_ref[...], g_ref[...]).astype(BF16)

    acc = _dot(xn_ref[...], w_ref[...])
    def emit(rope):
        c, s1, s2 = c_ref[...], s1_ref[...], s2_ref[...]
        for ch in range(PAGE_CHUNKS):
            val = acc[:, ch * HEAD_DIM:(ch + 1) * HEAD_DIM]
            if rope and ch < N_KV:
                val = _rope128(val, c, s1, s2)
            o_ref[0, pl.ds(ch, tm, stride=PAGE_CHUNKS), :] = val

    @pl.when(j >= 1)
    def _():
        emit(True)

    @pl.when(j == 0)
    def _():
        emit(False)


def kv_proj(x, g, w_bf, tables):
    m = x.shape[0]
    tm = _row_tile(m)
    n = w_bf.shape[1]
    tn = KV_SLAB
    c, s1, s2 = tables
    tab = pl.BlockSpec((tm, HEAD_DIM), lambda i, j: (i, 0))
    return pl.pallas_call(
        _kv_proj_kernel,
        grid=(m // tm, n // tn),
        in_specs=[pl.BlockSpec((tm, D_MODEL), lambda i, j: (i, 0)),
                  pl.BlockSpec((1, D_MODEL), lambda i, j: (0, 0)),
                  pl.BlockSpec((D_MODEL, tn), lambda i, j: (0, j)),
                  tab, tab, tab],
        out_specs=pl.BlockSpec((1, tm * PAGE_CHUNKS, HEAD_DIM), lambda i, j: (j, i, 0)),
        out_shape=jax.ShapeDtypeStruct((n // tn, m * PAGE_CHUNKS, HEAD_DIM), F32),
        scratch_shapes=[pltpu.VMEM((tm, D_MODEL), BF16)],
        compiler_params=_cparams(("parallel", "arbitrary")),
        name="kv_proj",
    )(x, g.reshape(1, -1), w_bf, c, s1, s2)


def _q_proj_kernel(x_ref, g_ref, w_ref, c_ref, s1_ref, s2_ref, q_ref, qr_ref, xn_ref):
    j = pl.program_id(1)

    @pl.when(j == 0)
    def _():
        xn_ref[...] = _rms(x_ref[...], g_ref[...]).astype(BF16)

    acc = _dot(xn_ref[...], w_ref[...])
    scale = HEAD_DIM ** -0.5
    c, s1, s2 = c_ref[...], s1_ref[...], s2_ref[...]
    q_ref[...] = (acc * scale).astype(BF16)
    for h in range(acc.shape[1] // HEAD_DIM):
        sl = slice(h * HEAD_DIM, (h + 1) * HEAD_DIM)
        qr_ref[:, sl] = (_rope128(acc[:, sl], c, s1, s2) * scale).astype(BF16)


def q_proj(x, g, wq_bf, tables):
    m = x.shape[0]
    tm = _row_tile(m)
    tn = 2 * GROUP_W
    c, s1, s2 = tables
    tab = pl.BlockSpec((tm, HEAD_DIM), lambda i, j: (i, 0))
    out = jax.ShapeDtypeStruct((m, Q_WIDTH), BF16)
    ospec = pl.BlockSpec((tm, tn), lambda i, j: (i, j))
    return pl.pallas_call(
        _q_proj_kernel,
        grid=(m // tm, Q_WIDTH // tn),
        in_specs=[pl.BlockSpec((tm, D_MODEL), lambda i, j: (i, 0)),
                  pl.BlockSpec((1, D_MODEL), lambda i, j: (0, 0)),
                  pl.BlockSpec((D_MODEL, tn), lambda i, j: (0, j)),
                  tab, tab, tab],
        out_specs=[ospec, ospec],
        out_shape=[out, out],
        scratch_shapes=[pltpu.VMEM((tm, D_MODEL), BF16)],
        compiler_params=_cparams(("parallel", "arbitrary")),
        name="q_proj",
    )(x, g.reshape(1, -1), wq_bf, c, s1, s2)


def _gate_proj_kernel(x_ref, g_ref, w_ref, o_ref):
    xn = _rms(x_ref[...], g_ref[...]).astype(BF16)
    o_ref[...] = _sigmoid(_dot(xn, w_ref[...]))


def gate_proj(x, g, wg_bf):
    m = x.shape[0]
    tm = _row_tile(m)
    n = wg_bf.shape[1]
    return pl.pallas_call(
        _gate_proj_kernel,
        grid=(m // tm,),
        in_specs=[pl.BlockSpec((tm, D_MODEL), lambda i: (i, 0)),
                  pl.BlockSpec((1, D_MODEL), lambda i: (0, 0)),
                  pl.BlockSpec((D_MODEL, n), lambda i: (0, 0))],
        out_specs=pl.BlockSpec((tm, n), lambda i: (i, 0)),
        out_shape=jax.ShapeDtypeStruct((m, n), F32),
        compiler_params=_cparams(("parallel",)),
        name="gate_proj",
    )(x, g.reshape(1, -1), wg_bf)


def _rms_norm_kernel(x_ref, g_ref, o_ref):
    o_ref[...] = _rms(x_ref[...], g_ref[...])


def rms_norm(x, g):
    m = x.shape[0]
    tm = _row_tile(m)
    return pl.pallas_call(
        _rms_norm_kernel,
        grid=(m // tm,),
        in_specs=[pl.BlockSpec((tm, D_MODEL), lambda i: (i, 0)),
                  pl.BlockSpec((1, D_MODEL), lambda i: (0, 0))],
        out_specs=pl.BlockSpec((tm, D_MODEL), lambda i: (i, 0)),
        out_shape=jax.ShapeDtypeStruct((m, D_MODEL), F32),
        compiler_params=_cparams(("parallel",)),
        name="rms_norm",
    )(x, g.reshape(1, -1))


def _mlp_kernel(x_ref, gpre_ref, wup_ref, wdn_ref, gpost_ref, o_ref, xn_ref, acc_ref):
    j = pl.program_id(1)

    @pl.when(j == 0)
    def _():
        xn_ref[...] = _rms(x_ref[...], gpre_ref[...]).astype(BF16)
        acc_ref[...] = jnp.zeros_like(acc_ref)

    h = jnp.maximum(_dot(xn_ref[...], wup_ref[...]), 0.0)
    acc_ref[...] += _dot((h * h).astype(BF16), wdn_ref[...])

    @pl.when(j == pl.num_programs(1) - 1)
    def _():
        o_ref[...] = x_ref[...] + _rms(acc_ref[...], gpost_ref[...])


def mlp(x, gpre, wup_bf, wdn_bf, gpost):
    m = x.shape[0]
    tm = _row_tile(m)
    tf = 1024
    return pl.pallas_call(
        _mlp_kernel,
        grid=(m // tm, D_FF // tf),
        in_specs=[pl.BlockSpec((tm, D_MODEL), lambda i, j: (i, 0)),
                  pl.BlockSpec((1, D_MODEL), lambda i, j: (0, 0)),
                  pl.BlockSpec((D_MODEL, tf), lambda i, j: (0, j)),
                  pl.BlockSpec((tf, D_MODEL), lambda i, j: (j, 0)),
                  pl.BlockSpec((1, D_MODEL), lambda i, j: (0, 0))],
        out_specs=pl.BlockSpec((tm, D_MODEL), lambda i, j: (i, 0)),
        out_shape=jax.ShapeDtypeStruct((m, D_MODEL), F32),
        scratch_shapes=[pltpu.VMEM((tm, D_MODEL), BF16), pltpu.VMEM((tm, D_MODEL), F32)],
        compiler_params=_cparams(("parallel", "arbitrary")),
        name="mlp",
    )(x, gpre.reshape(1, -1), wup_bf, wdn_bf, gpost.reshape(1, -1))


def _oproj_kernel(o_ref, w_ref, g_ref, res_ref, out_ref):
    out_ref[...] = res_ref[...] + _rms(_dot(o_ref[...], w_ref[...]), g_ref[...])


def oproj(o_bf, w_bf, g, res):
    m = o_bf.shape[0]
    tm = _row_tile(m)
    return pl.pallas_call(
        _oproj_kernel,
        grid=(m // tm,),
        in_specs=[pl.BlockSpec((tm, Q_WIDTH), lambda i: (i, 0)),
                  pl.BlockSpec((Q_WIDTH, D_MODEL), lambda i: (0, 0)),
                  pl.BlockSpec((1, D_MODEL), lambda i: (0, 0)),
                  pl.BlockSpec((tm, D_MODEL), lambda i: (i, 0))],
        out_specs=pl.BlockSpec((tm, D_MODEL), lambda i: (i, 0)),
        out_shape=jax.ShapeDtypeStruct((m, D_MODEL), F32),
        compiler_params=_cparams(("parallel",)),
        name="oproj",
    )(o_bf, w_bf, g.reshape(1, -1), res)


def _glu_kernel(x_ref, y_ref, gpre_ref, d_ref, w_ref, b_ref, gpost_ref, o_ref):
    x = x_ref[...]
    xn = _rms(x, gpre_ref[...])
    y = _gelu(y_ref[...] + d_ref[...] * xn)
    z = _dot(y.astype(BF16), w_ref[...]) + b_ref[...]
    o_ref[...] = x + _rms(y * _sigmoid(z), gpost_ref[...])


def glu_tail(x, y_ssm, gpre, d_skip, w_bf, b, gpost):
    m = x.shape[0]
    tm = _row_tile(m)
    vec = pl.BlockSpec((1, D_MODEL), lambda i: (0, 0))
    row = pl.BlockSpec((tm, D_MODEL), lambda i: (i, 0))
    return pl.pallas_call(
        _glu_kernel,
        grid=(m // tm,),
        in_specs=[row, row, vec, vec, pl.BlockSpec((D_MODEL, D_MODEL), lambda i: (0, 0)), vec, vec],
        out_specs=row,
        out_shape=jax.ShapeDtypeStruct((m, D_MODEL), F32),
        compiler_params=_cparams(("parallel",)),
        name="glu_tail",
    )(x, y_ssm, gpre.reshape(1, -1), d_skip.reshape(1, -1), w_bf, b.reshape(1, -1), gpost.reshape(1, -1))


S5_GB = 8
S5_SW = S5_GB * 2 * SSM_STATE


def _s5_prep_kernel(lrl_ref, lil_ref, ldt_ref, lrc_ref, lic_ref, btr_ref, bti_ref, ccat_ref, ctr_ref, cti_ref,
                    wlag_ref, winj_ref, dec_ref, *wcar_refs, pads):
    t = S5_CHUNK
    dt = jnp.exp(ldt_ref[...])
    lane = lax.broadcasted_iota(jnp.int32, (1, 1, HEAD_DIM), 2)
    re_half = lane < SSM_STATE
    lr, li = lrl_ref[...], lil_ref[...]
    ldr, ldi = lr * dt, li * dt
    mag = jnp.exp(ldr)
    a_re, a_im = mag * jnp.cos(ldi), mag * jnp.sin(ldi)
    den = lr * lr + li * li
    nr = a_re - 1.0
    f_re = (nr * lr + a_im * li) / den
    f_im = (a_im * lr - nr * li) / den
    bt_re = jnp.concatenate([btr_ref[...]] * t, axis=1)
    bt_im = jnp.concatenate([bti_ref[...]] * t, axis=1)
    bb_re = f_re * bt_re - f_im * bt_im
    bb_im = f_re * bt_im + f_im * bt_re
    k = lax.broadcasted_iota(jnp.int32, (1, t, 1), 1).astype(F32)
    mk = jnp.exp(ldr * k)
    per_j = lambda a: jnp.broadcast_to(a[:, :, None, :], (S5_GB, t, SSM_GROUP, HEAD_DIM)).reshape(S5_GB, t * SSM_GROUP, HEAD_DIM)
    ak_re, ak_im = per_j(mk * jnp.cos(ldi * k)), per_j(mk * jnp.sin(ldi * k))
    xc = jnp.where(re_half, ak_re * bb_re - ak_im * bb_im, ak_re * bb_im + ak_im * bb_re)
    kt = jnp.einsum('gnp,gpl->gnl', xc, ccat_ref[...], precision=lax.Precision.HIGHEST,
                    preferred_element_type=F32)
    own = lane // SSM_GROUP
    winj_ref[...] = jnp.zeros_like(winj_ref)
    for g in range(S5_GB):
        kg = jnp.where(own[0] == g, kt[g], 0.0).astype(BF16)
        xg = xc[g].astype(BF16)
        for s in range(t):
            src = slice((t - 1 - s) * SSM_GROUP, (t - s) * SSM_GROUP)
            dst = slice(s * HEAD_DIM + g * SSM_GROUP, s * HEAD_DIM + (g + 1) * SSM_GROUP)
            wlag_ref[0, dst, :] = kg[src]
            winj_ref[0, dst, g * HEAD_DIM:(g + 1) * HEAD_DIM] = xg[src]
    for n, pad in enumerate(pads):
        nn = float(t - pad)
        mn = jnp.exp(ldr * nn)
        an_re, an_im = mn * jnp.cos(ldi * nn), mn * jnp.sin(ldi * nn)
        dec_ref[:, 2 * n:2 * n + 1, :] = an_re
        dec_ref[:, 2 * n + 1:2 * n + 2, :] = jnp.where(re_half, -an_im, an_im)
    lrc, lic = lrc_ref[...], lic_ref[...]
    magc = jnp.exp(lrc * dt)
    ac_re, ac_im = magc * jnp.cos(lic * dt), magc * jnp.sin(lic * dt)
    top = lax.broadcasted_iota(jnp.int32, (1, HEAD_DIM, 1), 1) < SSM_STATE
    ctr, cti = ctr_ref[...], cti_ref[...]
    for ref, pad in zip(wcar_refs, pads):
        if pad:
            ref[...] = jnp.zeros_like(ref)
    p_re, p_im = ac_re, ac_im
    for e in range(t):
        w = jnp.where(top, ctr * p_re - cti * p_im, -(ctr * p_im + cti * p_re))
        for g in range(S5_GB):
            wg = jnp.where(own[0] == g, w[g], 0.0).astype(BF16)
            for ref, pad in zip(wcar_refs, pads):
                if e + pad < t:
                    ref[0, e + pad, g * HEAD_DIM:(g + 1) * HEAD_DIM, :] = wg
        p_re, p_im = p_re * ac_re - p_im * ac_im, p_re * ac_im + p_im * ac_re


def s5_prep(lam_re, lam_im, log_dt, b_re, b_im, c_re, c_im, pads):
    g, p = lam_re.shape
    gb, t = S5_GB, S5_CHUNK
    n_blk = g // gb
    twice = lambda a, axis: jnp.concatenate([a, a], axis=axis)
    lanes = lambda a: twice(a, 1).reshape(g, 1, 2 * p)
    cols = lambda a: twice(a, 1).reshape(g, 2 * p, 1)
    bt = lambda a: twice(a.transpose(0, 2, 1), 2)
    crt, cit = c_re.transpose(0, 2, 1), c_im.transpose(0, 2, 1)
    rep = (jnp.arange(SSM_GROUP)[:, None] == jnp.arange(HEAD_DIM)[None, :] % SSM_GROUP).astype(F32)
    over_h = lambda a: jnp.einsum('gpi,il->gpl', a, rep, precision=lax.Precision.HIGHEST)
    ccat = over_h(jnp.concatenate([crt, -cit], axis=1))
    blk3 = lambda s: pl.BlockSpec((gb,) + s, lambda i: (i, 0, 0))
    sq = blk3((2 * p, HEAD_DIM))
    car_spec = pl.BlockSpec((1, t, gb * HEAD_DIM, HEAD_DIM), lambda i: (i, 0, 0, 0))
    car_shape = jax.ShapeDtypeStruct((n_blk, t, gb * HEAD_DIM, HEAD_DIM), BF16)
    outs = pl.pallas_call(
        functools.partial(_s5_prep_kernel, pads=pads),
        grid=(n_blk,),
        in_specs=[blk3((1, 2 * p)), blk3((1, 2 * p)), blk3((1, 1)), blk3((2 * p, 1)), blk3((2 * p, 1)),
                  blk3((SSM_GROUP, 2 * p)), blk3((SSM_GROUP, 2 * p)), sq, sq, sq],
        out_specs=[pl.BlockSpec((1, t * HEAD_DIM, HEAD_DIM), lambda i: (i, 0, 0)),
                   pl.BlockSpec((1, t * HEAD_DIM, S5_SW), lambda i: (i, 0, 0)),
                   blk3((2 * len(pads), 2 * p))] + [car_spec] * len(pads),
        out_shape=[jax.ShapeDtypeStruct((n_blk, t * HEAD_DIM, HEAD_DIM), BF16),
                   jax.ShapeDtypeStruct((n_blk, t * HEAD_DIM, S5_SW), BF16),
                   jax.ShapeDtypeStruct((g, 2 * len(pads), 2 * p), F32)] + [car_shape] * len(pads),
        compiler_params=_cparams(("parallel",)),
        name="s5_prep",
    )(lanes(lam_re), lanes(lam_im), log_dt.reshape(g, 1, 1), cols(lam_re), cols(lam_im), bt(b_re), bt(b_im),
      ccat, over_h(twice(crt, 1)), over_h(twice(cit, 1)))
    w_lag, w_inj, dec = outs[:3]
    dec = dec.reshape(n_blk, gb, 2 * len(pads), 2 * p).transpose(2, 0, 1, 3).reshape(2 * len(pads), n_blk, 1, S5_SW)
    return dict(w_lag=w_lag, w_inj=w_inj,
                w_car={pad: outs[3 + n] for n, pad in enumerate(pads)},
                a1={pad: dec[2 * n] for n, pad in enumerate(pads)},
                a2={pad: dec[2 * n + 1] for n, pad in enumerate(pads)})


def _swap_halves(x):
    lane = lax.broadcasted_iota(jnp.int32, (1, x.shape[-1]), 1)
    return jnp.where(lane % (2 * SSM_STATE) < SSM_STATE,
                     pltpu.roll(x, x.shape[-1] - SSM_STATE, x.ndim - 1), pltpu.roll(x, SSM_STATE, x.ndim - 1))


def _s5_block_kernel(x_ref, wlag_ref, winj_ref, wcar_ref, a1_ref, a2_ref, a2s_ref, h0_ref, y_ref, hf_ref,
                     xc_ref, s_ref, ss_ref, hp_ref, *, nbk, nc):
    mc = nbk * nc
    t = S5_CHUNK
    for s in range(t):
        xc_ref[:, s * HEAD_DIM:(s + 1) * HEAD_DIM] = x_ref[pl.ds(s, mc, stride=t), :].astype(BF16)
    inc = _dot(xc_ref[...], winj_ref[0])
    s_ref[...] = inc
    ss_ref[...] = _swap_halves(inc)
    a1, a2, a2s = a1_ref[0], a2_ref[0], a2s_ref[0]
    h0 = tuple((h0_ref[0, 0, b:b + 1, :], _swap_halves(h0_ref[0, 0, b:b + 1, :])) for b in range(nbk))

    def step(c, hs):
        out = []
        for b in range(nbk):
            h, hx = hs[b]
            row = b * nc + c
            hp_ref[pl.ds(row, 1), :] = h
            out.append((a1 * h + a2 * hx + s_ref[pl.ds(row, 1), :],
                        a1 * hx + a2s * h + ss_ref[pl.ds(row, 1), :]))
        return tuple(out)

    hs = lax.fori_loop(0, nc, step, h0)
    for b in range(nbk):
        hf_ref[0, 0, b:b + 1, :] = hs[b][0]
    hp = hp_ref[...].astype(BF16)
    for tt in range(t):
        y = (_dot(xc_ref[:, 0:(tt + 1) * HEAD_DIM], wlag_ref[0, (t - 1 - tt) * HEAD_DIM:, :])
             + _dot(hp, wcar_ref[0, tt]))
        y_ref[pl.ds(tt, mc, stride=t), :] = y


def s5_block(xn, w_lag, w_inj, w_car, a1, a2, a2s, h0, nbk, nc):
    m = xn.shape[0]
    rows = nbk * nc * S5_CHUNK
    n_step = m // rows
    n_blk = N_GROUPS // S5_GB
    kdim = S5_CHUNK * HEAD_DIM
    wspec = lambda a: pl.BlockSpec((1,) + a.shape[1:], lambda g, b: (g,) + (0,) * (a.ndim - 1))
    hspec = pl.BlockSpec((1, 1, nbk, S5_SW), lambda g, b: (g, b, 0, 0))
    xspec = pl.BlockSpec((rows, HEAD_DIM), lambda g, b: (b, g))
    return pl.pallas_call(
        functools.partial(_s5_block_kernel, nbk=nbk, nc=nc),
        grid=(n_blk, n_step),
        in_specs=[xspec, wspec(w_lag), wspec(w_inj), wspec(w_car), wspec(a1), wspec(a2), wspec(a2s), hspec],
        out_specs=[xspec, hspec],
        out_shape=[jax.ShapeDtypeStruct((m, D_MODEL), F32), jax.ShapeDtypeStruct(h0.shape, F32)],
        scratch_shapes=[pltpu.VMEM((nbk * nc, kdim), BF16)] + [pltpu.VMEM((nbk * nc, S5_SW), F32)] * 3,
        compiler_params=_cparams(("parallel", "arbitrary")),
        name="s5_block",
    )(xn, w_lag, w_inj, w_car, a1, a2, a2s, h0)


CMP_PAGES = 16
CMP_ROWS = CMP_PAGES * (PAGE // CMP_STRIDE)
CMP_FLAT = CMP_STRIDE * HEAD_DIM


def _page_specs(n, first):
    def spec(k):
        return pl.BlockSpec((1, PAGE_ROWS, HEAD_DIM), lambda b, p, pt: (pt[b, first(p) + k], 0, 0))
    return [spec(k) for k in range(n)]


def _cmp_proj_kernel(pt_ref, *refs):
    x_refs = refs[:CMP_PAGES]
    wk_ref, wv_ref, o_ref, lhs_ref = refs[CMP_PAGES:]
    per_page = PAGE // CMP_STRIDE
    for pg, x_ref in enumerate(x_refs):
        for kv in range(2):
            for g in range(N_KV):
                c = kv * N_KV + g
                for s in range(CMP_STRIDE):
                    piece = x_ref[0, pl.ds(s * PAGE_CHUNKS + c, per_page, stride=CMP_STRIDE * PAGE_CHUNKS), :]
                    lhs_ref[kv, g, pg * per_page:(pg + 1) * per_page, s * HEAD_DIM:(s + 1) * HEAD_DIM] = piece
    for kv, w_ref in ((0, wk_ref), (1, wv_ref)):
        for g in range(N_KV):
            o_ref[0, kv, g] = _dot(lhs_ref[kv, g].astype(BF16), w_ref[...])


def cmp_proj(pages, page_table, wk_cat, wv_cat):
    nb, npg = page_table.shape
    assert npg % CMP_PAGES == 0, "compression consumes whole groups of pages"
    n_ch = npg * (PAGE // CMP_STRIDE)
    wspec = pl.BlockSpec((CMP_FLAT, 2 * CMP_HIDDEN), lambda b, p, pt: (0, 0))
    grid_spec = pltpu.PrefetchScalarGridSpec(
        num_scalar_prefetch=1,
        grid=(nb, npg // CMP_PAGES),
        in_specs=_page_specs(CMP_PAGES, lambda p: p * CMP_PAGES) + [wspec, wspec],
        out_specs=pl.BlockSpec((1, 2, N_KV, CMP_ROWS, 2 * CMP_HIDDEN), lambda b, p, pt: (b, 0, 0, p, 0)),
        scratch_shapes=[pltpu.VMEM((2, N_KV, CMP_ROWS, CMP_FLAT), F32)],
    )
    return pl.pallas_call(
        _cmp_proj_kernel,
        grid_spec=grid_spec,
        out_shape=jax.ShapeDtypeStruct((nb, 2, N_KV, n_ch, 2 * CMP_HIDDEN), F32),
        compiler_params=_cparams(("parallel", "arbitrary")),
        name="cmp_proj",
    )(page_table, *([pages] * CMP_PAGES), wk_cat, wv_cat)


def _cmp_mlp_kernel(p_ref, w1_ref, pe_ref, w2_ref, o_ref):
    proj = p_ref[0, 0, 0]
    n_ch = proj.shape[0]
    pre0 = jnp.sum(pe_ref[0] * w1_ref[0], axis=0, keepdims=True)
    first = proj[:, 0:CMP_HIDDEN]
    second = pltpu.roll(proj[:, CMP_HIDDEN:2 * CMP_HIDDEN], n_ch - 1, 0)
    pre = (pre0 + first) + second
    o_ref[0, 0, 0] = _dot(_gelu(pre).astype(BF16), w2_ref[0])


def cmp_mlp(proj, w1, pe, w2_bf):
    nb, _, _, n_ch, _ = proj.shape
    flat = CMP_BLOCK * HEAD_DIM
    return pl.pallas_call(
        _cmp_mlp_kernel,
        grid=(nb, 2, N_KV),
        in_specs=[pl.BlockSpec((1, 1, 1, n_ch, 2 * CMP_HIDDEN), lambda b, k, g: (b, k, g, 0, 0)),
                  pl.BlockSpec((1, flat, CMP_HIDDEN), lambda b, k, g: (k, 0, 0)),
                  pl.BlockSpec((1, flat, 1), lambda b, k, g: (k, 0, 0)),
                  pl.BlockSpec((1, CMP_HIDDEN, HEAD_DIM), lambda b, k, g: (k, 0, 0))],
        out_specs=pl.BlockSpec((1, 1, 1, n_ch, HEAD_DIM), lambda b, k, g: (b, k, g, 0, 0)),
        out_shape=jax.ShapeDtypeStruct((nb, 2, N_KV, n_ch, HEAD_DIM), F32),
        compiler_params=_cparams(("parallel", "parallel", "parallel")),
        name="cmp_mlp",
    )(proj, w1, pe, w2_bf)


def _kv_pack_kernel(pt_ref, *refs, per, n_groups):
    x_refs = refs[:per]
    t_ref, k_ref, vt_ref = refs[per:]
    j = pl.program_id(1)

    def emit(head):
        for g in range(N_KV):
            for pg in range(per):
                sl = slice(pg * PAGE, (pg + 1) * PAGE)
                k_ref[0, g, sl, :] = head(pg, g).astype(BF16)
                vt_ref[0, g, 0, :, sl] = head(pg, N_KV + g).T.astype(BF16)

    @pl.when(j < n_groups)
    def _():
        emit(lambda pg, c: x_refs[pg][0, pl.ds(c, PAGE, stride=PAGE_CHUNKS), :])

    @pl.when(j >= n_groups)
    def _():
        emit(lambda pg, c: t_ref[0, pl.ds(pg * PAGE_ROWS + c, PAGE, stride=PAGE_CHUNKS), :])


def kv_pack(pages, page_table, tail, vt_tile):
    nb, n_pages = page_table.shape
    per = vt_tile // PAGE
    assert n_pages % per == 0
    n_groups = n_pages // per
    if tail is None:
        n_tail = 0
        tail = jnp.zeros((nb, per * PAGE_ROWS, HEAD_DIM), F32)
    else:
        n_tail = tail.shape[1] // vt_tile
        tail = tail.reshape(nb, n_tail * per * PAGE_ROWS, HEAD_DIM)
    n_tot = n_groups + n_tail
    grid_spec = pltpu.PrefetchScalarGridSpec(
        num_scalar_prefetch=1,
        grid=(nb, n_tot),
        in_specs=_page_specs(per, lambda p: jnp.minimum(p, n_groups - 1) * per)
        + [pl.BlockSpec((1, per * PAGE_ROWS, HEAD_DIM), lambda b, p, pt: (b, jnp.maximum(p - n_groups, 0), 0))],
        out_specs=[pl.BlockSpec((1, N_KV, vt_tile, HEAD_DIM), lambda b, p, pt: (b, 0, p, 0)),
                   pl.BlockSpec((1, N_KV, 1, HEAD_DIM, vt_tile), lambda b, p, pt: (b, 0, p, 0, 0))],
    )
    return pl.pallas_call(
        functools.partial(_kv_pack_kernel, per=per, n_groups=n_groups),
        grid_spec=grid_spec,
        out_shape=[jax.ShapeDtypeStruct((nb, N_KV, n_tot * vt_tile, HEAD_DIM), BF16),
                   jax.ShapeDtypeStruct((nb, N_KV, n_tot, HEAD_DIM, vt_tile), BF16)],
        compiler_params=_cparams(("parallel", "arbitrary")),
        name="kv_pack",
    )(page_table, *([pages] * per), tail)


def _split3(x):
    hi = x.astype(BF16)
    r1 = x - hi.astype(F32)
    mid = r1.astype(BF16)
    lo = (r1 - mid.astype(F32)).astype(BF16)
    return hi, mid, lo


def _softmax_cols(s, col_ok):
    m = jnp.max(s, axis=0, keepdims=True)
    e = jnp.exp(s - m)
    den = jnp.sum(e, axis=0, keepdims=True)
    return e * jnp.where(col_ok, 1.0 / den, 0.0)


def _nsa_kernel(q_ref, qr_ref, gt_ref, kc_ref, vct_ref, ks_ref, vst_ref, kw_ref, vwt_ref, o_ref,
                bias_ref, s_ref, m_ref, l_ref, acc_ref, *, tq, q_off, swa_base):
    rq = GROUP_Q * tq
    w = max(tq, min(rq, HEAD_DIM))
    span = WINDOW + max(tq, PAGE)
    reps = rq // w
    n_cp = kc_ref.shape[2]
    n_tiles = bias_ref.shape[0]
    per = SLC_TILE // SLC_BLOCK
    n_blk = n_tiles * per
    tw = kw_ref.shape[2]
    i = pl.program_id(2)
    t0 = q_off + i * tq
    t_w = t0 + lax.broadcasted_iota(jnp.int32, (1, w), 1) % tq

    def tile(x):
        return jnp.concatenate([x] * reps, axis=x.ndim - 1) if reps > 1 else x

    def rows(ref):
        x = ref[...]
        return jnp.concatenate([x[:, r * HEAD_DIM:(r + 1) * HEAD_DIM] for r in range(GROUP_Q)], axis=0)

    q2 = rows(q_ref)
    qr2 = rows(qr_ref)
    t_rq = tile(t_w)

    n_io = lax.broadcasted_iota(jnp.int32, (n_cp, 1), 0)
    bias_c = jnp.where((n_io * CMP_STRIDE + (CMP_BLOCK - 1)) <= t_w, 0.0, NEG)
    p_c = _softmax_cols(_dot_nt(kc_ref[0, 0], q2) + tile(bias_c), t_rq >= CMP_BLOCK - 1)
    o_c = _dot(vct_ref[0, 0], p_c.astype(BF16))
    if reps > 1:
        p_sum = p_c[:, 0:w]
        for r in range(1, reps):
            p_sum = p_sum + p_c[:, r * w:(r + 1) * w]
    else:
        p_sum = p_c
        for r in range(1, GROUP_Q):
            p_sum = p_sum + pltpu.roll(p_c, r * tq, 1)
    s_col = lax.broadcasted_iota(jnp.int32, (n_blk, 1), 0)
    n_row = lax.broadcasted_iota(jnp.int32, (1, n_cp), 1)
    ov = jnp.logical_and(n_row * CMP_STRIDE < (s_col + 1) * SLC_BLOCK,
                         n_row * CMP_STRIDE + CMP_BLOCK > s_col * SLC_BLOCK)
    ov = jnp.where(ov, 1.0, 0.0).astype(BF16)
    hi, mid, lo = _split3(p_sum)
    imp = (_dot(ov, hi) + _dot(ov, mid)) + _dot(ov, lo)
    cur = t_w // SLC_BLOCK
    forced = jnp.logical_or(s_col == 0, jnp.logical_or(s_col == cur, s_col == cur - 1))
    elig = s_col * SLC_BLOCK <= t_w
    imp = jnp.where(forced, FORCE_SCORE, imp)
    imp = jnp.where(elig, imp, -jnp.inf)

    s_colf = s_col.astype(F32)

    def pick(_, carry):
        work, sel = carry
        best = jnp.max(work, axis=0, keepdims=True)
        first = jnp.min(jnp.where(work == best, s_colf, float(n_blk)), axis=0, keepdims=True)
        hit = s_colf == first
        return jnp.where(hit, -jnp.inf, work), jnp.where(hit, 1.0, sel)

    _, sel = lax.fori_loop(0, N_SELECT, pick, (imp, jnp.zeros((n_blk, w), F32)))
    bias_s = jnp.where(jnp.logical_and(elig, sel > 0.5), 0.0, NEG).reshape(n_tiles, per, w)
    bias_s = jnp.concatenate([bias_s, jnp.zeros_like(bias_s)], axis=1).astype(BF16)
    bias_ref[...] = tile(bias_s)

    m_ref[...] = jnp.full_like(m_ref, NEG)
    l_ref[...] = jnp.zeros_like(l_ref)
    acc_ref[...] = jnp.zeros_like(acc_ref)
    qr_t = qr2.astype(F32).T.astype(BF16)
    k_io = lax.broadcasted_iota(jnp.int32, (SLC_TILE, 1), 0)
    c_io = lax.broadcasted_iota(jnp.int32, (1, HEAD_DIM), 1)
    onehot = jnp.where(k_io // SLC_BLOCK == c_io, 1.0, 0.0).astype(BF16)
    zpad = jnp.zeros((HEAD_DIM - 2 * per, rq), BF16)

    def scores(kt):
        base = pl.multiple_of(kt * SLC_TILE, SLC_TILE)
        lhs = jnp.concatenate([ks_ref[0, 0, pl.ds(base, SLC_TILE), :], onehot], axis=1)
        rhs = jnp.concatenate([qr_t, bias_ref[kt], zpad], axis=0)
        return _dot(lhs, rhs)

    def update(kt, ss):
        m_old = m_ref[...]
        m_new = m_old
        for s in ss:
            m_new = jnp.maximum(m_new, jnp.max(s, axis=0, keepdims=True))
        alpha = jnp.exp(m_old - m_new)
        l_new = alpha * l_ref[...]
        acc = alpha * acc_ref[...]
        for j, s in enumerate(ss):
            e = jnp.exp(s - m_new)
            l_new = l_new + jnp.sum(e, axis=0, keepdims=True)
            acc = acc + _dot(vst_ref[0, 0, kt + j], e.astype(BF16))
        l_ref[...] = l_new
        acc_ref[...] = acc
        m_ref[...] = m_new

    k_last = t0 // SLC_TILE
    causal = tile(jnp.where((k_last * SLC_TILE + k_io) <= t_w, 0.0, NEG))
    if rq >= SLC_TILE:
        sa_ref, sb_ref = s_ref.at[0], s_ref.at[1]
        sa_ref[...] = scores(0)

        def body(i, carry):
            kt = 2 * i
            sb_ref[...] = scores(kt + 1)
            update(kt, [sa_ref[...]])
            sa_ref[...] = scores(kt + 2)
            update(kt + 1, [sb_ref[...]])
            return carry

        n_pair = k_last // 2
        lax.fori_loop(0, n_pair, body, 0)

        @pl.when(k_last % 2 == 0)
        def _():
            update(k_last, [sa_ref[...] + causal])

        @pl.when(k_last % 2 == 1)
        def _():
            sb_ref[...] = scores(k_last)
            update(k_last - 1, [sa_ref[...]])
            update(k_last, [sb_ref[...] + causal])
    else:
        joint = SLC_TILE // rq
        n_step = k_last // joint

        def body(i, carry):
            update(i * joint, [scores(i * joint + j) for j in range(joint)])
            return carry

        lax.fori_loop(0, n_step, body, 0)

        def single(kt, carry):
            update(kt, [scores(kt)])
            return carry

        lax.fori_loop(n_step * joint, k_last, single, 0)
        update(k_last, [scores(k_last) + causal])
    o_s = acc_ref[...] * (1.0 / l_ref[...])

    start = jnp.clip(t0 - WINDOW - swa_base, 0, tw - span)
    start = pl.multiple_of(start, PAGE)
    key_pos = swa_base + start + lax.broadcasted_iota(jnp.int32, (span, 1), 0)
    dist = t_w - key_pos
    ok_w = jnp.logical_and(jnp.logical_and(dist >= 0, dist < WINDOW), key_pos >= swa_base)
    sw = _dot_nt(kw_ref[0, 0, pl.ds(start, span), :], qr2) + tile(jnp.where(ok_w, 0.0, NEG))
    p_w = _softmax_cols(sw, True).astype(BF16)
    o_w = jnp.zeros((HEAD_DIM, rq), F32)
    for jt in range(span // PAGE):
        o_w = o_w + _dot(vwt_ref[0, 0, start // PAGE + jt], p_w[jt * PAGE:(jt + 1) * PAGE, :])

    gt = gt_ref[0, 0, 0]
    o_t = o_c * gt[0:1, :] + o_s * gt[1:2, :] + o_w * gt[2:3, :]
    for c in range(rq // HEAD_DIM):
        blk = o_t[:, c * HEAD_DIM:(c + 1) * HEAD_DIM].T.astype(BF16)
        if tq >= HEAD_DIM:
            r, q0 = divmod(c * HEAD_DIM, tq)
            o_ref[q0:q0 + HEAD_DIM, r * HEAD_DIM:(r + 1) * HEAD_DIM] = blk
        else:
            per_blk = HEAD_DIM // tq
            for rr in range(per_blk):
                r = c * per_blk + rr
                o_ref[:, r * HEAD_DIM:(r + 1) * HEAD_DIM] = blk[rr * tq:(rr + 1) * tq, :]


def nsa_attention(q, qr, gates_t, kc, vct, ks, vst, kw, vwt, *, nb, nq, tq, q_off, swa_base):
    rq = GROUP_Q * tq
    tk = ks.shape[2]
    assert SLC_TILE % tq == 0 and q_off % SLC_TILE == 0 and tk % SLC_TILE == 0
    qspec = pl.BlockSpec((tq, GROUP_W), lambda b, g, i: (b * nq + i, g))
    full = lambda a: pl.BlockSpec((1, 1) + a.shape[2:], lambda b, g, i: (b, g) + (0,) * (a.ndim - 2))
    return pl.pallas_call(
        functools.partial(_nsa_kernel, tq=tq, q_off=q_off, swa_base=swa_base),
        grid=(nb, N_KV, nq),
        in_specs=[qspec, qspec,
                  pl.BlockSpec((1, 1, 1, N_BRANCH, rq), lambda b, g, i: (b, g, i, 0, 0)),
                  full(kc), full(vct), full(ks), full(vst), full(kw), full(vwt)],
        out_specs=qspec,
        out_shape=jax.ShapeDtypeStruct((nb * nq * tq, Q_WIDTH), BF16),
        scratch_shapes=[pltpu.VMEM((tk // SLC_TILE, 2 * (SLC_TILE // SLC_BLOCK), rq), BF16),
                        pltpu.VMEM((2, SLC_TILE, rq), F32),
                        pltpu.VMEM((1, rq), F32), pltpu.VMEM((1, rq), F32), pltpu.VMEM((HEAD_DIM, rq), F32)],
        compiler_params=_cparams(("parallel", "parallel", "arbitrary")),
        name="nsa_attention",
    )(q, qr, gates_t, kc, vct, ks, vst, kw, vwt)


def _s5_layer(x, h0_re, h0_im, lp, nb, seq):
    m = x.shape[0]
    nc = -(-seq // S5_CHUNK)
    pad = nc * S5_CHUNK - seq
    n_blk = N_GROUPS // S5_GB
    xn = rms_norm(x, lp['norm_pre'])
    if pad == 0:
        nbk, nbp, xs = 1, nb, xn
    else:
        nbp = -(-nb // S5_CHUNK) * S5_CHUNK
        nbk = nbp
        xs = jnp.pad(xn.reshape(nb, seq, D_MODEL), ((0, nbp - nb), (pad, 0), (0, 0))).reshape(-1, D_MODEL)
    n_step = nbp // nbk
    a1, a2 = lp['a1'][pad], lp['a2'][pad]
    if h0_re is None:
        h0 = jnp.zeros((n_blk, n_step, nbk, S5_SW), F32)
    else:
        h0 = jnp.stack([h0_re, h0_im], axis=2).reshape(nb, n_blk, S5_SW)
        h0 = jnp.pad(h0, ((0, nbp - nb), (0, 0), (0, 0))).transpose(1, 0, 2).reshape(n_blk, n_step, nbk, S5_SW)
    y, hf = s5_block(xs, lp['w_lag'], lp['w_inj'], lp['w_car'][pad], a1, a2, -a2, h0, nbk, nc)
    y = y.reshape(nbp, nc * S5_CHUNK, D_MODEL)[:nb, pad:].reshape(m, D_MODEL)
    hf = hf.reshape(n_blk, nbp, S5_GB, 2, SSM_STATE)[:, :nb].transpose(3, 1, 0, 2, 4).reshape(2, nb, N_GROUPS, SSM_STATE)
    x = glu_tail(x, y, lp['norm_pre'], lp['d'], lp['w_glu'], lp['b_glu'], lp['norm_post'])
    return x, hf[0], hf[1]


def _gates_t(gates, nb, nq, tq):
    g = gates[:, :N_HEADS * N_BRANCH].reshape(nb, nq, tq, N_KV, GROUP_Q, N_BRANCH)
    return g.transpose(0, 3, 1, 5, 4, 2).reshape(nb, N_KV, nq, N_BRANCH, GROUP_Q * tq)


def _pad_rows(a, nb, seq, tq):
    if seq == tq or seq % tq == 0:
        return a
    a = a.reshape(nb, seq, -1)
    return jnp.pad(a, ((0, 0), (0, tq - seq), (0, 0))).reshape(nb * tq, -1)


def _nsa_layer(x, lp, kvs, tables, nb, seq, tq, q_off, swa_base):
    q, qr = q_proj(x, lp['norm_pre'], lp['w_q'], tables)
    gates = gate_proj(x, lp['norm_pre'], lp['w_g'])
    nq = -(-seq // tq)
    o = nsa_attention(_pad_rows(q, nb, seq, tq), _pad_rows(qr, nb, seq, tq),
                      _gates_t(_pad_rows(gates, nb, seq, tq), nb, nq, tq),
                      *kvs, nb=nb, nq=nq, tq=tq, q_off=q_off, swa_base=swa_base)
    if nq * tq != seq:
        o = o.reshape(nb, nq * tq, Q_WIDTH)[:, :seq].reshape(nb * seq, Q_WIDTH)
    return oproj(o, lp['w_o'], lp['norm_post'], x)


def _trunk(x, nb, seq, pos0, h0_re, h0_im, past, prm):
    m = nb * seq
    ssm_re, ssm_im = [], []
    n_a = len(prm['a'])
    for layer in range(n_a):
        x, hr, hi = _s5_layer(x, None if h0_re is None else h0_re[layer],
                              None if h0_im is None else h0_im[layer], prm['a'][layer], nb, seq)
        ssm_re.append(hr)
        ssm_im.append(hi)
        ml = prm['mlp'][layer]
        x = mlp(x, ml['norm_pre'], ml['w_up'], ml['w_down'], ml['norm_post'])

    pos = pos0 + jnp.tile(jnp.arange(seq), nb)
    tables = _rope_tables(pos)
    kv = kv_proj(x, prm['kv_norm'], prm['w_kv'], tables)
    rows_cmp = kv[0].reshape(nb, seq, 2, N_KV, HEAD_DIM)
    rows_slc = kv[1].reshape(nb, seq, 2, N_KV, HEAD_DIM)
    rows_swa = kv[2].reshape(nb, seq, 2, N_KV, HEAD_DIM)
    no_tail = None
    if past is None:
        npg = seq // PAGE
        table = jnp.arange(nb * npg, dtype=jnp.int32).reshape(nb, npg)
        pages = lambda j: kv[j].reshape(nb * npg, PAGE_ROWS, HEAD_DIM)
        cproj = cmp_proj(pages(0), table, prm['cmp_w1k_cat'], prm['cmp_w1v_cat'])
        ks, vst = kv_pack(pages(1), table, no_tail, SLC_TILE)
        kw, vwt = kv_pack(pages(2), table, no_tail, PAGE)
        swa_base = 0
        tq = 256
        swa_buf = rows_swa[:, seq - WINDOW:]
    else:
        cache_cmp, cache_slc, state_swa, table = past
        n_pool = cache_cmp.shape[0]
        npg = table.shape[1]
        cproj = cmp_proj(cache_cmp.reshape(n_pool, PAGE_ROWS, HEAD_DIM), table, prm['cmp_w1k_cat'], prm['cmp_w1v_cat'])
        t_real = npg * PAGE + seq
        t_pad = -(-t_real // SLC_TILE) * SLC_TILE
        tail = jnp.pad(kv[1].reshape(nb, seq, KV_SLAB),
                       ((0, 0), (0, t_pad - npg * PAGE - seq), (0, 0)))
        ks, vst = kv_pack(cache_slc.reshape(n_pool, PAGE_ROWS, HEAD_DIM), table, tail, SLC_TILE)
        w_keep = state_swa.shape[1]
        local = jnp.concatenate([state_swa.reshape(nb, w_keep, KV_SLAB), rows_swa.reshape(nb, seq, KV_SLAB)], axis=1)
        swa_buf = local[:, -w_keep:].reshape(nb, w_keep, 2, N_KV, HEAD_DIM)
        tw = -(-max(local.shape[1], SWA_SPAN) // PAGE) * PAGE
        local = jnp.pad(local, ((0, 0), (0, tw - local.shape[1]), (0, 0)))
        lt = jnp.arange(nb * (tw // PAGE), dtype=jnp.int32).reshape(nb, tw // PAGE)
        kw, vwt = kv_pack(local.reshape(nb * (tw // PAGE), PAGE_ROWS, HEAD_DIM), lt, no_tail, PAGE)
        swa_base = pos0 - w_keep
        tq = 32
    cmp_out = cmp_mlp(cproj, prm['cmp_w1'], prm['cmp_pe'], prm['cmp_w2'])
    kc = cmp_out[:, 0].astype(BF16)
    vct = cmp_out[:, 1].transpose(0, 1, 3, 2).astype(BF16)
    kvs = (kc, vct, ks, vst, kw, vwt)

    for j, lp in enumerate(prm['b']):
        x = _nsa_layer(x, lp, kvs, tables, nb, seq, tq, pos0, swa_base)
        ml = prm['mlp'][n_a + j]
        x = mlp(x, ml['norm_pre'], ml['w_up'], ml['w_down'], ml['norm_post'])
    return x, rows_cmp, rows_slc, swa_buf, jnp.stack(ssm_re), jnp.stack(ssm_im)


def kernel(x_prompt, x_sample, cache_kv_cmp, cache_kv_slc, state_kv_swa, state_ssm_re, state_ssm_im, page_table,
           a_norm_pre, a_lam_re, a_lam_im, a_log_dt, a_b_re, a_b_im, a_c_re, a_c_im, a_d, a_w_glu, a_b_glu,
           a_norm_post, kv_norm, w_kv, cmp_w1_k, cmp_pe_k, cmp_w2_k, cmp_w1_v, cmp_pe_v, cmp_w2_v, b_norm_pre,
           b_w_qg, b_w_o, b_norm_post, mlp_norm_pre, mlp_w_up, mlp_w_down, mlp_norm_post):
    n_a = a_norm_pre.shape[0]
    n_b = b_norm_pre.shape[0]
    prm = {'a': [], 'b': [], 'mlp': []}
    pads = tuple(sorted({-x_prompt.shape[1] % S5_CHUNK, -x_sample.shape[1] % S5_CHUNK}))
    for l in range(n_a):
        ops = s5_prep(a_lam_re[l], a_lam_im[l], a_log_dt[l], a_b_re[l], a_b_im[l], a_c_re[l], a_c_im[l], pads)
        prm['a'].append(dict(norm_pre=a_norm_pre[l], norm_post=a_norm_post[l], d=a_d[l],
                             w_glu=a_w_glu[l].astype(BF16), b_glu=a_b_glu[l], **ops))
    n_gate = N_HEADS * N_BRANCH
    for l in range(n_b):
        w_g = jnp.pad(b_w_qg[l][:, Q_WIDTH:], ((0, 0), (0, HEAD_DIM - n_gate))).astype(BF16)
        prm['b'].append(dict(norm_pre=b_norm_pre[l], norm_post=b_norm_post[l],
                             w_q=b_w_qg[l][:, :Q_WIDTH].astype(BF16), w_g=w_g, w_o=b_w_o[l].astype(BF16)))
    for l in range(n_a + n_b):
        prm['mlp'].append(dict(norm_pre=mlp_norm_pre[l], norm_post=mlp_norm_post[l],
                               w_up=mlp_w_up[l].astype(BF16), w_down=mlp_w_down[l].astype(BF16)))
    prm['kv_norm'] = kv_norm
    prm['w_kv'] = w_kv.astype(BF16)
    r = CMP_BLOCK // CMP_STRIDE
    cat = lambda w: w.reshape(r, CMP_FLAT, CMP_HIDDEN).transpose(1, 0, 2).reshape(CMP_FLAT, r * CMP_HIDDEN).astype(BF16)
    prm['cmp_w1k_cat'] = cat(cmp_w1_k)
    prm['cmp_w1v_cat'] = cat(cmp_w1_v)
    flat = CMP_BLOCK * HEAD_DIM
    prm['cmp_w1'] = jnp.stack([cmp_w1_k.reshape(flat, CMP_HIDDEN), cmp_w1_v.reshape(flat, CMP_HIDDEN)])
    prm['cmp_pe'] = jnp.stack([cmp_pe_k.reshape(flat, 1), cmp_pe_v.reshape(flat, 1)])
    prm['cmp_w2'] = jnp.stack([cmp_w2_k, cmp_w2_v]).astype(BF16)

    bp, sp, _ = x_prompt.shape
    y_p, cmp_p, slc_p, swa_p, re_p, im_p = _trunk(x_prompt.reshape(bp * sp, D_MODEL), bp, sp, 0,
                                                  None, None, None, prm)
    bs, ss, _ = x_sample.shape
    past_len = page_table.shape[1] * PAGE
    y_s, cmp_s, slc_s, swa_s, re_s, im_s = _trunk(x_sample.reshape(bs * ss, D_MODEL), bs, ss, past_len,
                                                  state_ssm_re, state_ssm_im,
                                                  (cache_kv_cmp, cache_kv_slc, state_kv_swa, page_table), prm)
    return (y_p.reshape(bp, sp, D_MODEL), y_s.reshape(bs, ss, D_MODEL), cmp_p, cmp_s, slc_p, slc_s,
            swa_p, swa_s, re_p, im_p, re_s, im_s)
```

```python
import functools
import math

import jax
import jax.numpy as jnp
from jax import lax
from jax.experimental import pallas as pl
from jax.experimental.pallas import tpu as pltpu

F32 = jnp.float32
BF16 = jnp.bfloat16

D_MODEL = 2048
N_HEADS = 16
HEAD_DIM = 128
N_KV = 4
GROUP_Q = N_HEADS // N_KV
N_BRANCH = 3
ROT_DIM = HEAD_DIM // 4
ROPE_THETA = 500000.0
SSM_GROUP = 16
N_GROUPS = D_MODEL // SSM_GROUP
SSM_STATE = 64
S5_CHUNK = 16
D_FF = 4 * D_MODEL
CMP_BLOCK = 32
CMP_STRIDE = 16
CMP_HIDDEN = 2 * HEAD_DIM
SLC_BLOCK = 64
N_SELECT = 16
WINDOW = 512
PAGE = 128
FORCE_SCORE = 1.0e4
EPS = 1e-6
Q_WIDTH = N_HEADS * HEAD_DIM
GROUP_W = GROUP_Q * HEAD_DIM
KV_SLAB = 2 * N_KV * HEAD_DIM
PAGE_CHUNKS = KV_SLAB // HEAD_DIM
PAGE_ROWS = PAGE * PAGE_CHUNKS
SLC_TILE = 512
SWA_SPAN = WINDOW + PAGE
NEG = -1.0e30
VMEM_LIMIT = 56 * 1024 * 1024


def _cparams(sem):
    return pltpu.CompilerParams(dimension_semantics=sem, vmem_limit_bytes=VMEM_LIMIT)


def _rms(x, g):
    var = jnp.mean(x * x, axis=-1, keepdims=True)
    return x * lax.rsqrt(var + EPS) * g


def _gelu(x):
    return 0.5 * x * (1.0 + jnp.tanh(math.sqrt(2.0 / math.pi) * (x + 0.044715 * (x * x * x))))


def _sigmoid(x):
    return 1.0 / (1.0 + jnp.exp(-x))


def _dot(a, b):
    return jnp.dot(a, b, preferred_element_type=F32)


def _dot_nt(a, b):
    return lax.dot_general(a, b, (((1,), (1,)), ((), ())), preferred_element_type=F32)


def _rope128(x, c, s1, s2):
    return x * c + pltpu.roll(x, HEAD_DIM - ROT_DIM // 2, 1) * s1 + pltpu.roll(x, ROT_DIM // 2, 1) * s2


def _rope_tables(pos):
    half = ROT_DIM // 2
    inv = ROPE_THETA ** (-jnp.arange(half, dtype=F32) / half)
    ang = pos.astype(F32)[:, None] * inv[None, :]
    cos, sin = jnp.cos(ang), jnp.sin(ang)
    n = pos.shape[0]
    rest = HEAD_DIM - ROT_DIM
    c = jnp.concatenate([cos, cos, jnp.ones((n, rest), F32)], axis=1)
    s1 = jnp.concatenate([-sin, jnp.zeros((n, HEAD_DIM - half), F32)], axis=1)
    s2 = jnp.concatenate([jnp.zeros((n, half), F32), sin, jnp.zeros((n, rest), F32)], axis=1)
    return c, s1, s2


def _row_tile(m):
    return 512 if m % 512 == 0 else m


def _kv_proj_kernel(x_ref, g_ref, w_ref, c_ref, s1_ref, s2_ref, o_ref, xn_ref):
    j = pl.program_id(1)
    tm = x_ref.shape[0]

    @pl.when(j == 0)
    def _():
        xn_ref[...] = _rms(x_ref[...], g_ref[...]).astype(BF16)

    acc = _dot(xn_ref[...], w_ref[...])
    def emit(rope):
        c, s1, s2 = c_ref[...], s1_ref[...], s2_ref[...]
        for ch in range(PAGE_CHUNKS):
            val = acc[:, ch * HEAD_DIM:(ch + 1) * HEAD_DIM]
            if rope and ch < N_KV:
                val = _rope128(val, c, s1, s2)
            o_ref[0, pl.ds(ch, tm, stride=PAGE_CHUNKS), :] = val

    @pl.when(j >= 1)
    def _():
        emit(True)

    @pl.when(j == 0)
    def _():
        emit(False)


def kv_proj(x, g, w_bf, tables):
    m = x.shape[0]
    tm = _row_tile(m)
    n = w_bf.shape[1]
    tn = KV_SLAB
    c, s1, s2 = tables
    tab = pl.BlockSpec((tm, HEAD_DIM), lambda i, j: (i, 0))
    return pl.pallas_call(
        _kv_proj_kernel,
        grid=(m // tm, n // tn),
        in_specs=[pl.BlockSpec((tm, D_MODEL), lambda i, j: (i, 0)),
                  pl.BlockSpec((1, D_MODEL), lambda i, j: (0, 0)),
                  pl.BlockSpec((D_MODEL, tn), lambda i, j: (0, j)),
                  tab, tab, tab],
        out_specs=pl.BlockSpec((1, tm * PAGE_CHUNKS, HEAD_DIM), lambda i, j: (j, i, 0)),
        out_shape=jax.ShapeDtypeStruct((n // tn, m * PAGE_CHUNKS, HEAD_DIM), F32),
        scratch_shapes=[pltpu.VMEM((tm, D_MODEL), BF16)],
        compiler_params=_cparams(("parallel", "arbitrary")),
        name="kv_proj",
    )(x, g.reshape(1, -1), w_bf, c, s1, s2)


def _q_proj_kernel(x_ref, g_ref, w_ref, c_ref, s1_ref, s2_ref, q_ref, qr_ref, xn_ref):
    j = pl.program_id(1)

    @pl.when(j == 0)
    def _():
        xn_ref[...] = _rms(x_ref[...], g_ref[...]).astype(BF16)

    acc = _dot(xn_ref[...], w_ref[...])
    scale = HEAD_DIM ** -0.5
    c, s1, s2 = c_ref[...], s1_ref[...], s2_ref[...]
    q_ref[...] = (acc * scale).astype(BF16)
    for h in range(acc.shape[1] // HEAD_DIM):
        sl = slice(h * HEAD_DIM, (h + 1) * HEAD_DIM)
        qr_ref[:, sl] = (_rope128(acc[:, sl], c, s1, s2) * scale).astype(BF16)


def q_proj(x, g, wq_bf, tables):
    m = x.shape[0]
    tm = _row_tile(m)
    tn = 2 * GROUP_W
    c, s1, s2 = tables
    tab = pl.BlockSpec((tm, HEAD_DIM), lambda i, j: (i, 0))
    out = jax.ShapeDtypeStruct((m, Q_WIDTH), BF16)
    ospec = pl.BlockSpec((tm, tn), lambda i, j: (i, j))
    return pl.pallas_call(
        _q_proj_kernel,
        grid=(m // tm, Q_WIDTH // tn),
        in_specs=[pl.BlockSpec((tm, D_MODEL), lambda i, j: (i, 0)),
                  pl.BlockSpec((1, D_MODEL), lambda i, j: (0, 0)),
                  pl.BlockSpec((D_MODEL, tn), lambda i, j: (0, j)),
                  tab, tab, tab],
        out_specs=[ospec, ospec],
        out_shape=[out, out],
        scratch_shapes=[pltpu.VMEM((tm, D_MODEL), BF16)],
        compiler_params=_cparams(("parallel", "arbitrary")),
        name="q_proj",
    )(x, g.reshape(1, -1), wq_bf, c, s1, s2)


def _gate_proj_kernel(x_ref, g_ref, w_ref, o_ref):
    xn = _rms(x_ref[...], g_ref[...]).astype(BF16)
    o_ref[...] = _sigmoid(_dot(xn, w_ref[...]))


def gate_proj(x, g, wg_bf):
    m = x.shape[0]
    tm = _row_tile(m)
    n = wg_bf.shape[1]
    return pl.pallas_call(
        _gate_proj_kernel,
        grid=(m // tm,),
        in_specs=[pl.BlockSpec((tm, D_MODEL), lambda i: (i, 0)),
                  pl.BlockSpec((1, D_MODEL), lambda i: (0, 0)),
                  pl.BlockSpec((D_MODEL, n), lambda i: (0, 0))],
        out_specs=pl.BlockSpec((tm, n), lambda i: (i, 0)),
        out_shape=jax.ShapeDtypeStruct((m, n), F32),
        compiler_params=_cparams(("parallel",)),
        name="gate_proj",
    )(x, g.reshape(1, -1), wg_bf)


def _rms_norm_kernel(x_ref, g_ref, o_ref):
    o_ref[...] = _rms(x_ref[...], g_ref[...])


def rms_norm(x, g):
    m = x.shape[0]
    tm = _row_tile(m)
    return pl.pallas_call(
        _rms_norm_kernel,
        grid=(m // tm,),
        in_specs=[pl.BlockSpec((tm, D_MODEL), lambda i: (i, 0)),
                  pl.BlockSpec((1, D_MODEL), lambda i: (0, 0))],
        out_specs=pl.BlockSpec((tm, D_MODEL), lambda i: (i, 0)),
        out_shape=jax.ShapeDtypeStruct((m, D_MODEL), F32),
        compiler_params=_cparams(("parallel",)),
        name="rms_norm",
    )(x, g.reshape(1, -1))


def _mlp_kernel(x_ref, gpre_ref, wup_ref, wdn_ref, gpost_ref, o_ref, xn_ref, acc_ref):
    j = pl.program_id(1)

    @pl.when(j == 0)
    def _():
        xn_ref[...] = _rms(x_ref[...], gpre_ref[...]).astype(BF16)
        acc_ref[...] = jnp.zeros_like(acc_ref)

    h = jnp.maximum(_dot(xn_ref[...], wup_ref[...]), 0.0)
    acc_ref[...] += _dot((h * h).astype(BF16), wdn_ref[...])

    @pl.when(j == pl.num_programs(1) - 1)
    def _():
        o_ref[...] = x_ref[...] + _rms(acc_ref[...], gpost_ref[...])


def mlp(x, gpre, wup_bf, wdn_bf, gpost):
    m = x.shape[0]
    tm = _row_tile(m)
    tf = 1024
    return pl.pallas_call(
        _mlp_kernel,
        grid=(m // tm, D_FF // tf),
        in_specs=[pl.BlockSpec((tm, D_MODEL), lambda i, j: (i, 0)),
                  pl.BlockSpec((1, D_MODEL), lambda i, j: (0, 0)),
                  pl.BlockSpec((D_MODEL, tf), lambda i, j: (0, j)),
                  pl.BlockSpec((tf, D_MODEL), lambda i, j: (j, 0)),
                  pl.BlockSpec((1, D_MODEL), lambda i, j: (0, 0))],
        out_specs=pl.BlockSpec((tm, D_MODEL), lambda i, j: (i, 0)),
        out_shape=jax.ShapeDtypeStruct((m, D_MODEL), F32),
        scratch_shapes=[pltpu.VMEM((tm, D_MODEL), BF16), pltpu.VMEM((tm, D_MODEL), F32)],
        compiler_params=_cparams(("parallel", "arbitrary")),
        name="mlp",
    )(x, gpre.reshape(1, -1), wup_bf, wdn_bf, gpost.reshape(1, -1))


def _oproj_kernel(o_ref, w_ref, g_ref, res_ref, out_ref):
    out_ref[...] = res_ref[...] + _rms(_dot(o_ref[...], w_ref[...]), g_ref[...])


def oproj(o_bf, w_bf, g, res):
    m = o_bf.shape[0]
    tm = _row_tile(m)
    return pl.pallas_call(
        _oproj_kernel,
        grid=(m // tm,),
        in_specs=[pl.BlockSpec((tm, Q_WIDTH), lambda i: (i, 0)),
                  pl.BlockSpec((Q_WIDTH, D_MODEL), lambda i: (0, 0)),
                  pl.BlockSpec((1, D_MODEL), lambda i: (0, 0)),
                  pl.BlockSpec((tm, D_MODEL), lambda i: (i, 0))],
        out_specs=pl.BlockSpec((tm, D_MODEL), lambda i: (i, 0)),
        out_shape=jax.ShapeDtypeStruct((m, D_MODEL), F32),
        compiler_params=_cparams(("parallel",)),
        name="oproj",
    )(o_bf, w_bf, g.reshape(1, -1), res)


def _glu_kernel(x_ref, y_ref, gpre_ref, d_ref, w_ref, b_ref, gpost_ref, o_ref):
    x = x_ref[...]
    xn = _rms(x, gpre_ref[...])
    y = _gelu(y_ref[...] + d_ref[...] * xn)
    z = _dot(y.astype(BF16), w_ref[...]) + b_ref[...]
    o_ref[...] = x + _rms(y * _sigmoid(z), gpost_ref[...])


def glu_tail(x, y_ssm, gpre, d_skip, w_bf, b, gpost):
    m = x.shape[0]
    tm = _row_tile(m)
    vec = pl.BlockSpec((1, D_MODEL), lambda i: (0, 0))
    row = pl.BlockSpec((tm, D_MODEL), lambda i: (i, 0))
    return pl.pallas_call(
        _glu_kernel,
        grid=(m // tm,),
        in_specs=[row, row, vec, vec, pl.BlockSpec((D_MODEL, D_MODEL), lambda i: (0, 0)), vec, vec],
        out_specs=row,
        out_shape=jax.ShapeDtypeStruct((m, D_MODEL), F32),
        compiler_params=_cparams(("parallel",)),
        name="glu_tail",
    )(x, y_ssm, gpre.reshape(1, -1), d_skip.reshape(1, -1), w_bf, b.reshape(1, -1), gpost.reshape(1, -1))


S5_GB = 8
S5_SW = S5_GB * 2 * SSM_STATE


def _s5_prep_kernel(lrl_ref, lil_ref, ldt_ref, lrc_ref, lic_ref, btr_ref, bti_ref, ccat_ref, ctr_ref, cti_ref,
                    wlag_ref, winj_ref, dec_ref, *wcar_refs, pads):
    t = S5_CHUNK
    dt = jnp.exp(ldt_ref[...])
    lane = lax.broadcasted_iota(jnp.int32, (1, 1, HEAD_DIM), 2)
    re_half = lane < SSM_STATE
    lr, li = lrl_ref[...], lil_ref[...]
    ldr, ldi = lr * dt, li * dt
    mag = jnp.exp(ldr)
    a_re, a_im = mag * jnp.cos(ldi), mag * jnp.sin(ldi)
    den = lr * lr + li * li
    nr = a_re - 1.0
    f_re = (nr * lr + a_im * li) / den
    f_im = (a_im * lr - nr * li) / den
    bt_re = jnp.concatenate([btr_ref[...]] * t, axis=1)
    bt_im = jnp.concatenate([bti_ref[...]] * t, axis=1)
    bb_re = f_re * bt_re - f_im * bt_im
    bb_im = f_re * bt_im + f_im * bt_re
    k = lax.broadcasted_iota(jnp.int32, (1, t, 1), 1).astype(F32)
    mk = jnp.exp(ldr * k)
    per_j = lambda a: jnp.broadcast_to(a[:, :, None, :], (S5_GB, t, SSM_GROUP, HEAD_DIM)).reshape(S5_GB, t * SSM_GROUP, HEAD_DIM)
    ak_re, ak_im = per_j(mk * jnp.cos(ldi * k)), per_j(mk * jnp.sin(ldi * k))
    xc = jnp.where(re_half, ak_re * bb_re - ak_im * bb_im, ak_re * bb_im + ak_im * bb_re)
    kt = jnp.einsum('gnp,gpl->gnl', xc, ccat_ref[...], precision=lax.Precision.HIGHEST,
                    preferred_element_type=F32)
    own = lane // SSM_GROUP
    winj_ref[...] = jnp.zeros_like(winj_ref)
    for g in range(S5_GB):
        kg = jnp.where(own[0] == g, kt[g], 0.0).astype(BF16)
        xg = xc[g].astype(BF16)
        for s in range(t):
            src = slice((t - 1 - s) * SSM_GROUP, (t - s) * SSM_GROUP)
            dst = slice(s * HEAD_DIM + g * SSM_GROUP, s * HEAD_DIM + (g + 1) * SSM_GROUP)
            wlag_ref[0, dst, :] = kg[src]
            winj_ref[0, dst, g * HEAD_DIM:(g + 1) * HEAD_DIM] = xg[src]
    for n, pad in enumerate(pads):
        nn = float(t - pad)
        mn = jnp.exp(ldr * nn)
        an_re, an_im = mn * jnp.cos(ldi * nn), mn * jnp.sin(ldi * nn)
        dec_ref[:, 2 * n:2 * n + 1, :] = an_re
        dec_ref[:, 2 * n + 1:2 * n + 2, :] = jnp.where(re_half, -an_im, an_im)
    lrc, lic = lrc_ref[...], lic_ref[...]
    magc = jnp.exp(lrc * dt)
    ac_re, ac_im = magc * jnp.cos(lic * dt), magc * jnp.sin(lic * dt)
    top = lax.broadcasted_iota(jnp.int32, (1, HEAD_DIM, 1), 1) < SSM_STATE
    ctr, cti = ctr_ref[...], cti_ref[...]
    for ref, pad in zip(wcar_refs, pads):
        if pad:
            ref[...] = jnp.zeros_like(ref)
    p_re, p_im = ac_re, ac_im
    for e in range(t):
        w = jnp.where(top, ctr * p_re - cti * p_im, -(ctr * p_im + cti * p_re))
        for g in range(S5_GB):
            wg = jnp.where(own[0] == g, w[g], 0.0).astype(BF16)
            for ref, pad in zip(wcar_refs, pads):
                if e + pad < t:
                    ref[0, e + pad, g * HEAD_DIM:(g + 1) * HEAD_DIM, :] = wg
        p_re, p_im = p_re * ac_re - p_im * ac_im, p_re * ac_im + p_im * ac_re


def s5_prep(lam_re, lam_im, log_dt, b_re, b_im, c_re, c_im, pads):
    g, p = lam_re.shape
    gb, t = S5_GB, S5_CHUNK
    n_blk = g // gb
    twice = lambda a, axis: jnp.concatenate([a, a], axis=axis)
    lanes = lambda a: twice(a, 1).reshape(g, 1, 2 * p)
    cols = lambda a: twice(a, 1).reshape(g, 2 * p, 1)
    bt = lambda a: twice(a.transpose(0, 2, 1), 2)
    crt, cit = c_re.transpose(0, 2, 1), c_im.transpose(0, 2, 1)
    rep = (jnp.arange(SSM_GROUP)[:, None] == jnp.arange(HEAD_DIM)[None, :] % SSM_GROUP).astype(F32)
    over_h = lambda a: jnp.einsum('gpi,il->gpl', a, rep, precision=lax.Precision.HIGHEST)
    ccat = over_h(jnp.concatenate([crt, -cit], axis=1))
    blk3 = lambda s: pl.BlockSpec((gb,) + s, lambda i: (i, 0, 0))
    sq = blk3((2 * p, HEAD_DIM))
    car_spec = pl.BlockSpec((1, t, gb * HEAD_DIM, HEAD_DIM), lambda i: (i, 0, 0, 0))
    car_shape = jax.ShapeDtypeStruct((n_blk, t, gb * HEAD_DIM, HEAD_DIM), BF16)
    outs = pl.pallas_call(
        functools.partial(_s5_prep_kernel, pads=pads),
        grid=(n_blk,),
        in_specs=[blk3((1, 2 * p)), blk3((1, 2 * p)), blk3((1, 1)), blk3((2 * p, 1)), blk3((2 * p, 1)),
                  blk3((SSM_GROUP, 2 * p)), blk3((SSM_GROUP, 2 * p)), sq, sq, sq],
        out_specs=[pl.BlockSpec((1, t * HEAD_DIM, HEAD_DIM), lambda i: (i, 0, 0)),
                   pl.BlockSpec((1, t * HEAD_DIM, S5_SW), lambda i: (i, 0, 0)),
                   blk3((2 * len(pads), 2 * p))] + [car_spec] * len(pads),
        out_shape=[jax.ShapeDtypeStruct((n_blk, t * HEAD_DIM, HEAD_DIM), BF16),
                   jax.ShapeDtypeStruct((n_blk, t * HEAD_DIM, S5_SW), BF16),
                   jax.ShapeDtypeStruct((g, 2 * len(pads), 2 * p), F32)] + [car_shape] * len(pads),
        compiler_params=_cparams(("parallel",)),
        name="s5_prep",
    )(lanes(lam_re), lanes(lam_im), log_dt.reshape(g, 1, 1), cols(lam_re), cols(lam_im), bt(b_re), bt(b_im),
      ccat, over_h(twice(crt, 1)), over_h(twice(cit, 1)))
    w_lag, w_inj, dec = outs[:3]
    dec = dec.reshape(n_blk, gb, 2 * len(pads), 2 * p).transpose(2, 0, 1, 3).reshape(2 * len(pads), n_blk, 1, S5_SW)
    return dict(w_lag=w_lag, w_inj=w_inj,
                w_car={pad: outs[3 + n] for n, pad in enumerate(pads)},
                a1={pad: dec[2 * n] for n, pad in enumerate(pads)},
                a2={pad: dec[2 * n + 1] for n, pad in enumerate(pads)})


def _swap_halves(x):
    lane = lax.broadcasted_iota(jnp.int32, (1, x.shape[-1]), 1)
    return jnp.where(lane % (2 * SSM_STATE) < SSM_STATE,
                     pltpu.roll(x, x.shape[-1] - SSM_STATE, x.ndim - 1), pltpu.roll(x, SSM_STATE, x.ndim - 1))


def _s5_block_kernel(x_ref, wlag_ref, winj_ref, wcar_ref, a1_ref, a2_ref, a2s_ref, h0_ref, y_ref, hf_ref,
                     xc_ref, s_ref, ss_ref, hp_ref, *, nbk, nc):
    mc = nbk * nc
    t = S5_CHUNK
    for s in range(t):
        xc_ref[:, s * HEAD_DIM:(s + 1) * HEAD_DIM] = x_ref[pl.ds(s, mc, stride=t), :].astype(BF16)
    inc = _dot(xc_ref[...], winj_ref[0])
    s_ref[...] = inc
    ss_ref[...] = _swap_halves(inc)
    a1, a2, a2s = a1_ref[0], a2_ref[0], a2s_ref[0]
    h0 = tuple((h0_ref[0, 0, b:b + 1, :], _swap_halves(h0_ref[0, 0, b:b + 1, :])) for b in range(nbk))

    def step(c, hs):
        out = []
        for b in range(nbk):
            h, hx = hs[b]
            row = b * nc + c
            hp_ref[pl.ds(row, 1), :] = h
            out.append((a1 * h + a2 * hx + s_ref[pl.ds(row, 1), :],
                        a1 * hx + a2s * h + ss_ref[pl.ds(row, 1), :]))
        return tuple(out)

    hs = lax.fori_loop(0, nc, step, h0)
    for b in range(nbk):
        hf_ref[0, 0, b:b + 1, :] = hs[b][0]
    hp = hp_ref[...].astype(BF16)
    for tt in range(t):
        y = (_dot(xc_ref[:, 0:(tt + 1) * HEAD_DIM], wlag_ref[0, (t - 1 - tt) * HEAD_DIM:, :])
             + _dot(hp, wcar_ref[0, tt]))
        y_ref[pl.ds(tt, mc, stride=t), :] = y


def s5_block(xn, w_lag, w_inj, w_car, a1, a2, a2s, h0, nbk, nc):
    m = xn.shape[0]
    rows = nbk * nc * S5_CHUNK
    n_step = m // rows
    n_blk = N_GROUPS // S5_GB
    kdim = S5_CHUNK * HEAD_DIM
    wspec = lambda a: pl.BlockSpec((1,) + a.shape[1:], lambda g, b: (g,) + (0,) * (a.ndim - 1))
    hspec = pl.BlockSpec((1, 1, nbk, S5_SW), lambda g, b: (g, b, 0, 0))
    xspec = pl.BlockSpec((rows, HEAD_DIM), lambda g, b: (b, g))
    return pl.pallas_call(
        functools.partial(_s5_block_kernel, nbk=nbk, nc=nc),
        grid=(n_blk, n_step),
        in_specs=[xspec, wspec(w_lag), wspec(w_inj), wspec(w_car), wspec(a1), wspec(a2), wspec(a2s), hspec],
        out_specs=[xspec, hspec],
        out_shape=[jax.ShapeDtypeStruct((m, D_MODEL), F32), jax.ShapeDtypeStruct(h0.shape, F32)],
        scratch_shapes=[pltpu.VMEM((nbk * nc, kdim), BF16)] + [pltpu.VMEM((nbk * nc, S5_SW), F32)] * 3,
        compiler_params=_cparams(("parallel", "arbitrary")),
        name="s5_block",
    )(xn, w_lag, w_inj, w_car, a1, a2, a2s, h0)


CMP_PAGES = 16
CMP_ROWS = CMP_PAGES * (PAGE // CMP_STRIDE)
CMP_FLAT = CMP_STRIDE * HEAD_DIM


def _page_specs(n, first):
    def spec(k):
        return pl.BlockSpec((1, PAGE_ROWS, HEAD_DIM), lambda b, p, pt: (pt[b, first(p) + k], 0, 0))
    return [spec(k) for k in range(n)]


def _cmp_proj_kernel(pt_ref, *refs):
    x_refs = refs[:CMP_PAGES]
    wk_ref, wv_ref, o_ref, lhs_ref = refs[CMP_PAGES:]
    per_page = PAGE // CMP_STRIDE
    for pg, x_ref in enumerate(x_refs):
        for kv in range(2):
            for g in range(N_KV):
                c = kv * N_KV + g
                for s in range(CMP_STRIDE):
                    piece = x_ref[0, pl.ds(s * PAGE_CHUNKS + c, per_page, stride=CMP_STRIDE * PAGE_CHUNKS), :]
                    lhs_ref[kv, g, pg * per_page:(pg + 1) * per_page, s * HEAD_DIM:(s + 1) * HEAD_DIM] = piece
    for kv, w_ref in ((0, wk_ref), (1, wv_ref)):
        for g in range(N_KV):
            o_ref[0, kv, g] = _dot(lhs_ref[kv, g].astype(BF16), w_ref[...])


def cmp_proj(pages, page_table, wk_cat, wv_cat):
    nb, npg = page_table.shape
    assert npg % CMP_PAGES == 0, "compression consumes whole groups of pages"
    n_ch = npg * (PAGE // CMP_STRIDE)
    wspec = pl.BlockSpec((CMP_FLAT, 2 * CMP_HIDDEN), lambda b, p, pt: (0, 0))
    grid_spec = pltpu.PrefetchScalarGridSpec(
        num_scalar_prefetch=1,
        grid=(nb, npg // CMP_PAGES),
        in_specs=_page_specs(CMP_PAGES, lambda p: p * CMP_PAGES) + [wspec, wspec],
        out_specs=pl.BlockSpec((1, 2, N_KV, CMP_ROWS, 2 * CMP_HIDDEN), lambda b, p, pt: (b, 0, 0, p, 0)),
        scratch_shapes=[pltpu.VMEM((2, N_KV, CMP_ROWS, CMP_FLAT), F32)],
    )
    return pl.pallas_call(
        _cmp_proj_kernel,
        grid_spec=grid_spec,
        out_shape=jax.ShapeDtypeStruct((nb, 2, N_KV, n_ch, 2 * CMP_HIDDEN), F32),
        compiler_params=_cparams(("parallel", "arbitrary")),
        name="cmp_proj",
    )(page_table, *([pages] * CMP_PAGES), wk_cat, wv_cat)


def _cmp_mlp_kernel(p_ref, w1_ref, pe_ref, w2_ref, o_ref):
    proj = p_ref[0, 0, 0]
    n_ch = proj.shape[0]
    pre0 = jnp.sum(pe_ref[0] * w1_ref[0], axis=0, keepdims=True)
    first = proj[:, 0:CMP_HIDDEN]
    second = pltpu.roll(proj[:, CMP_HIDDEN:2 * CMP_HIDDEN], n_ch - 1, 0)
    pre = (pre0 + first) + second
    o_ref[0, 0, 0] = _dot(_gelu(pre).astype(BF16), w2_ref[0])


def cmp_mlp(proj, w1, pe, w2_bf):
    nb, _, _, n_ch, _ = proj.shape
    flat = CMP_BLOCK * HEAD_DIM
    return pl.pallas_call(
        _cmp_mlp_kernel,
        grid=(nb, 2, N_KV),
        in_specs=[pl.BlockSpec((1, 1, 1, n_ch, 2 * CMP_HIDDEN), lambda b, k, g: (b, k, g, 0, 0)),
                  pl.BlockSpec((1, flat, CMP_HIDDEN), lambda b, k, g: (k, 0, 0)),
                  pl.BlockSpec((1, flat, 1), lambda b, k, g: (k, 0, 0)),
                  pl.BlockSpec((1, CMP_HIDDEN, HEAD_DIM), lambda b, k, g: (k, 0, 0))],
        out_specs=pl.BlockSpec((1, 1, 1, n_ch, HEAD_DIM), lambda b, k, g: (b, k, g, 0, 0)),
        out_shape=jax.ShapeDtypeStruct((nb, 2, N_KV, n_ch, HEAD_DIM), F32),
        compiler_params=_cparams(("parallel", "parallel", "parallel")),
        name="cmp_mlp",
    )(proj, w1, pe, w2_bf)


def _kv_pack_kernel(pt_ref, *refs, per, n_groups):
    x_refs = refs[:per]
    t_ref, k_ref, vt_ref = refs[per:]
    j = pl.program_id(1)

    def emit(head):
        for g in range(N_KV):
            for pg in range(per):
                sl = slice(pg * PAGE, (pg + 1) * PAGE)
                k_ref[0, g, sl, :] = head(pg, g).astype(BF16)
                vt_ref[0, g, 0, :, sl] = head(pg, N_KV + g).T.astype(BF16)

    @pl.when(j < n_groups)
    def _():
        emit(lambda pg, c: x_refs[pg][0, pl.ds(c, PAGE, stride=PAGE_CHUNKS), :])

    @pl.when(j >= n_groups)
    def _():
        emit(lambda pg, c: t_ref[0, pl.ds(pg * PAGE_ROWS + c, PAGE, stride=PAGE_CHUNKS), :])


def kv_pack(pages, page_table, tail, vt_tile):
    nb, n_pages = page_table.shape
    per = vt_tile // PAGE
    assert n_pages % per == 0
    n_groups = n_pages // per
    if tail is None:
        n_tail = 0
        tail = jnp.zeros((nb, per * PAGE_ROWS, HEAD_DIM), F32)
    else:
        n_tail = tail.shape[1] // vt_tile
        tail = tail.reshape(nb, n_tail * per * PAGE_ROWS, HEAD_DIM)
    n_tot = n_groups + n_tail
    grid_spec = pltpu.PrefetchScalarGridSpec(
        num_scalar_prefetch=1,
        grid=(nb, n_tot),
        in_specs=_page_specs(per, lambda p: jnp.minimum(p, n_groups - 1) * per)
        + [pl.BlockSpec((1, per * PAGE_ROWS, HEAD_DIM), lambda b, p, pt: (b, jnp.maximum(p - n_groups, 0), 0))],
        out_specs=[pl.BlockSpec((1, N_KV, vt_tile, HEAD_DIM), lambda b, p, pt: (b, 0, p, 0)),
                   pl.BlockSpec((1, N_KV, 1, HEAD_DIM, vt_tile), lambda b, p, pt: (b, 0, p, 0, 0))],
    )
    return pl.pallas_call(
        functools.partial(_kv_pack_kernel, per=per, n_groups=n_groups),
        grid_spec=grid_spec,
        out_shape=[jax.ShapeDtypeStruct((nb, N_KV, n_tot * vt_tile, HEAD_DIM), BF16),
                   jax.ShapeDtypeStruct((nb, N_KV, n_tot, HEAD_DIM, vt_tile), BF16)],
        compiler_params=_cparams(("parallel", "arbitrary")),
        name="kv_pack",
    )(page_table, *([pages] * per), tail)


def _split3(x):
    hi = x.astype(BF16)
    r1 = x - hi.astype(F32)
    mid = r1.astype(BF16)
    lo = (r1 - mid.astype(F32)).astype(BF16)
    return hi, mid, lo


def _softmax_cols(s, col_ok):
    m = jnp.max(s, axis=0, keepdims=True)
    e = jnp.exp(s - m)
    den = jnp.sum(e, axis=0, keepdims=True)
    return e * jnp.where(col_ok, 1.0 / den, 0.0)


def _nsa_kernel(q_ref, qr_ref, gt_ref, kc_ref, vct_ref, ks_ref, vst_ref, kw_ref, vwt_ref, o_ref,
                bias_ref, s_ref, m_ref, l_ref, acc_ref, *, tq, q_off, swa_base):
    rq = GROUP_Q * tq
    w = max(tq, min(rq, HEAD_DIM))
    span = WINDOW + max(tq, PAGE)
    reps = rq // w
    n_cp = kc_ref.shape[2]
    n_tiles = bias_ref.shape[0]
    per = SLC_TILE // SLC_BLOCK
    n_blk = n_tiles * per
    tw = kw_ref.shape[2]
    i = pl.program_id(2)
    t0 = q_off + i * tq
    t_w = t0 + lax.broadcasted_iota(jnp.int32, (1, w), 1) % tq

    def tile(x):
        return jnp.concatenate([x] * reps, axis=x.ndim - 1) if reps > 1 else x

    def rows(ref):
        x = ref[...]
        return jnp.concatenate([x[:, r * HEAD_DIM:(r + 1) * HEAD_DIM] for r in range(GROUP_Q)], axis=0)

    q2 = rows(q_ref)
    qr2 = rows(qr_ref)
    t_rq = tile(t_w)

    n_io = lax.broadcasted_iota(jnp.int32, (n_cp, 1), 0)
    bias_c = jnp.where((n_io * CMP_STRIDE + (CMP_BLOCK - 1)) <= t_w, 0.0, NEG)
    p_c = _softmax_cols(_dot_nt(kc_ref[0, 0], q2) + tile(bias_c), t_rq >= CMP_BLOCK - 1)
    o_c = _dot(vct_ref[0, 0], p_c.astype(BF16))
    if reps > 1:
        p_sum = p_c[:, 0:w]
        for r in range(1, reps):
            p_sum = p_sum + p_c[:, r * w:(r + 1) * w]
    else:
        p_sum = p_c
        for r in range(1, GROUP_Q):
            p_sum = p_sum + pltpu.roll(p_c, r * tq, 1)
    s_col = lax.broadcasted_iota(jnp.int32, (n_blk, 1), 0)
    n_row = lax.broadcasted_iota(jnp.int32, (1, n_cp), 1)
    ov = jnp.logical_and(n_row * CMP_STRIDE < (s_col + 1) * SLC_BLOCK,
                         n_row * CMP_STRIDE + CMP_BLOCK > s_col * SLC_BLOCK)
    ov = jnp.where(ov, 1.0, 0.0).astype(BF16)
    hi, mid, lo = _split3(p_sum)
    imp = (_dot(ov, hi) + _dot(ov, mid)) + _dot(ov, lo)
    cur = t_w // SLC_BLOCK
    forced = jnp.logical_or(s_col == 0, jnp.logical_or(s_col == cur, s_col == cur - 1))
    elig = s_col * SLC_BLOCK <= t_w
    imp = jnp.where(forced, FORCE_SCORE, imp)
    imp = jnp.where(elig, imp, -jnp.inf)

    s_colf = s_col.astype(F32)

    def pick(_, carry):
        work, sel = carry
        best = jnp.max(work, axis=0, keepdims=True)
        first = jnp.min(jnp.where(work == best, s_colf, float(n_blk)), axis=0, keepdims=True)
        hit = s_colf == first
        return jnp.where(hit, -jnp.inf, work), jnp.where(hit, 1.0, sel)

    _, sel = lax.fori_loop(0, N_SELECT, pick, (imp, jnp.zeros((n_blk, w), F32)))
    bias_s = jnp.where(jnp.logical_and(elig, sel > 0.5), 0.0, NEG).reshape(n_tiles, per, w)
    bias_s = jnp.concatenate([bias_s, jnp.zeros_like(bias_s)], axis=1).astype(BF16)
    bias_ref[...] = tile(bias_s)

    m_ref[...] = jnp.full_like(m_ref, NEG)
    l_ref[...] = jnp.zeros_like(l_ref)
    acc_ref[...] = jnp.zeros_like(acc_ref)
    qr_t = qr2.astype(F32).T.astype(BF16)
    k_io = lax.broadcasted_iota(jnp.int32, (SLC_TILE, 1), 0)
    c_io = lax.broadcasted_iota(jnp.int32, (1, HEAD_DIM), 1)
    onehot = jnp.where(k_io // SLC_BLOCK == c_io, 1.0, 0.0).astype(BF16)
    zpad = jnp.zeros((HEAD_DIM - 2 * per, rq), BF16)

    def scores(kt):
        base = pl.multiple_of(kt * SLC_TILE, SLC_TILE)
        lhs = jnp.concatenate([ks_ref[0, 0, pl.ds(base, SLC_TILE), :], onehot], axis=1)
        rhs = jnp.concatenate([qr_t, bias_ref[kt], zpad], axis=0)
        return _dot(lhs, rhs)

    def update(kt, ss):
        m_old = m_ref[...]
        m_new = m_old
        for s in ss:
            m_new = jnp.maximum(m_new, jnp.max(s, axis=0, keepdims=True))
        alpha = jnp.exp(m_old - m_new)
        l_new = alpha * l_ref[...]
        acc = alpha * acc_ref[...]
        for j, s in enumerate(ss):
            e = jnp.exp(s - m_new)
            l_new = l_new + jnp.sum(e, axis=0, keepdims=True)
            acc = acc + _dot(vst_ref[0, 0, kt + j], e.astype(BF16))
        l_ref[...] = l_new
        acc_ref[...] = acc
        m_ref[...] = m_new

    k_last = t0 // SLC_TILE
    causal = tile(jnp.where((k_last * SLC_TILE + k_io) <= t_w, 0.0, NEG))
    if rq >= SLC_TILE:
        sa_ref, sb_ref = s_ref.at[0], s_ref.at[1]
        sa_ref[...] = scores(0)

        def body(i, carry):
            kt = 2 * i
            sb_ref[...] = scores(kt + 1)
            update(kt, [sa_ref[...]])
            sa_ref[...] = scores(kt + 2)
            update(kt + 1, [sb_ref[...]])
            return carry

        n_pair = k_last // 2
        lax.fori_loop(0, n_pair, body, 0)

        @pl.when(k_last % 2 == 0)
        def _():
            update(k_last, [sa_ref[...] + causal])

        @pl.when(k_last % 2 == 1)
        def _():
            sb_ref[...] = scores(k_last)
            update(k_last - 1, [sa_ref[...]])
            update(k_last, [sb_ref[...] + causal])
    else:
        joint = SLC_TILE // rq
        n_step = k_last // joint
        sa_ref, sb_ref = s_ref.at[0], s_ref.at[1]

        def put(ref, i):
            for j in range(joint):
                ref[j * SLC_TILE:(j + 1) * SLC_TILE, :] = scores(i * joint + j)

        def get(ref):
            return [ref[j * SLC_TILE:(j + 1) * SLC_TILE, :] for j in range(joint)]

        put(sa_ref, 0)

        def body(i, carry):
            put(sb_ref, 2 * i + 1)
            update(2 * i * joint, get(sa_ref))
            put(sa_ref, 2 * i + 2)
            update((2 * i + 1) * joint, get(sb_ref))
            return carry

        n_pair = jnp.maximum(n_step - 1, 0) // 2
        lax.fori_loop(0, n_pair, body, 0)

        @pl.when(n_step % 2 == 1)
        def _():
            update((n_step - 1) * joint, get(sa_ref))

        @pl.when(jnp.logical_and(n_step % 2 == 0, n_step > 0))
        def _():
            put(sb_ref, n_step - 1)
            update((n_step - 2) * joint, get(sa_ref))
            update((n_step - 1) * joint, get(sb_ref))

        def single(kt, carry):
            update(kt, [scores(kt)])
            return carry

        lax.fori_loop(n_step * joint, k_last, single, 0)
        update(k_last, [scores(k_last) + causal])
    o_s = acc_ref[...] * (1.0 / l_ref[...])

    start = jnp.clip(t0 - WINDOW - swa_base, 0, tw - span)
    start = pl.multiple_of(start, PAGE)
    key_pos = swa_base + start + lax.broadcasted_iota(jnp.int32, (span, 1), 0)
    dist = t_w - key_pos
    ok_w = jnp.logical_and(jnp.logical_and(dist >= 0, dist < WINDOW), key_pos >= swa_base)
    sw = _dot_nt(kw_ref[0, 0, pl.ds(start, span), :], qr2) + tile(jnp.where(ok_w, 0.0, NEG))
    p_w = _softmax_cols(sw, True).astype(BF16)
    o_w = jnp.zeros((HEAD_DIM, rq), F32)
    for jt in range(span // PAGE):
        o_w = o_w + _dot(vwt_ref[0, 0, start // PAGE + jt], p_w[jt * PAGE:(jt + 1) * PAGE, :])

    gt = gt_ref[0, 0, 0]
    o_t = o_c * gt[0:1, :] + o_s * gt[1:2, :] + o_w * gt[2:3, :]
    for c in range(rq // HEAD_DIM):
        blk = o_t[:, c * HEAD_DIM:(c + 1) * HEAD_DIM].T.astype(BF16)
        if tq >= HEAD_DIM:
            r, q0 = divmod(c * HEAD_DIM, tq)
            o_ref[q0:q0 + HEAD_DIM, r * HEAD_DIM:(r + 1) * HEAD_DIM] = blk
        else:
            per_blk = HEAD_DIM // tq
            for rr in range(per_blk):
                r = c * per_blk + rr
                o_ref[:, r * HEAD_DIM:(r + 1) * HEAD_DIM] = blk[rr * tq:(rr + 1) * tq, :]


def nsa_attention(q, qr, gates_t, kc, vct, ks, vst, kw, vwt, *, nb, nq, tq, q_off, swa_base):
    rq = GROUP_Q * tq
    tk = ks.shape[2]
    assert SLC_TILE % tq == 0 and q_off % SLC_TILE == 0 and tk % SLC_TILE == 0
    qspec = pl.BlockSpec((tq, GROUP_W), lambda b, g, i: (b * nq + i, g))
    full = lambda a: pl.BlockSpec((1, 1) + a.shape[2:], lambda b, g, i: (b, g) + (0,) * (a.ndim - 2))
    return pl.pallas_call(
        functools.partial(_nsa_kernel, tq=tq, q_off=q_off, swa_base=swa_base),
        grid=(nb, N_KV, nq),
        in_specs=[qspec, qspec,
                  pl.BlockSpec((1, 1, 1, N_BRANCH, rq), lambda b, g, i: (b, g, i, 0, 0)),
                  full(kc), full(vct), full(ks), full(vst), full(kw), full(vwt)],
        out_specs=qspec,
        out_shape=jax.ShapeDtypeStruct((nb * nq * tq, Q_WIDTH), BF16),
        scratch_shapes=[pltpu.VMEM((tk // SLC_TILE, 2 * (SLC_TILE // SLC_BLOCK), rq), BF16),
                        pltpu.VMEM((2, SLC_TILE * max(1, SLC_TILE // rq), rq), F32),
                        pltpu.VMEM((1, rq), F32), pltpu.VMEM((1, rq), F32), pltpu.VMEM((HEAD_DIM, rq), F32)],
        compiler_params=_cparams(("parallel", "parallel", "arbitrary")),
        name="nsa_attention",
    )(q, qr, gates_t, kc, vct, ks, vst, kw, vwt)


def _s5_layer(x, h0_re, h0_im, lp, nb, seq):
    m = x.shape[0]
    nc = -(-seq // S5_CHUNK)
    pad = nc * S5_CHUNK - seq
    n_blk = N_GROUPS // S5_GB
    xn = rms_norm(x, lp['norm_pre'])
    if pad == 0:
        nbk, nbp, xs = 1, nb, xn
    else:
        nbp = -(-nb // S5_CHUNK) * S5_CHUNK
        nbk = nbp
        xs = jnp.pad(xn.reshape(nb, seq, D_MODEL), ((0, nbp - nb), (pad, 0), (0, 0))).reshape(-1, D_MODEL)
    n_step = nbp // nbk
    a1, a2 = lp['a1'][pad], lp['a2'][pad]
    if h0_re is None:
        h0 = jnp.zeros((n_blk, n_step, nbk, S5_SW), F32)
    else:
        h0 = jnp.stack([h0_re, h0_im], axis=2).reshape(nb, n_blk, S5_SW)
        h0 = jnp.pad(h0, ((0, nbp - nb), (0, 0), (0, 0))).transpose(1, 0, 2).reshape(n_blk, n_step, nbk, S5_SW)
    y, hf = s5_block(xs, lp['w_lag'], lp['w_inj'], lp['w_car'][pad], a1, a2, -a2, h0, nbk, nc)
    y = y.reshape(nbp, nc * S5_CHUNK, D_MODEL)[:nb, pad:].reshape(m, D_MODEL)
    hf = hf.reshape(n_blk, nbp, S5_GB, 2, SSM_STATE)[:, :nb].transpose(3, 1, 0, 2, 4).reshape(2, nb, N_GROUPS, SSM_STATE)
    x = glu_tail(x, y, lp['norm_pre'], lp['d'], lp['w_glu'], lp['b_glu'], lp['norm_post'])
    return x, hf[0], hf[1]


def _gates_t(gates, nb, nq, tq):
    g = gates[:, :N_HEADS * N_BRANCH].reshape(nb, nq, tq, N_KV, GROUP_Q, N_BRANCH)
    return g.transpose(0, 3, 1, 5, 4, 2).reshape(nb, N_KV, nq, N_BRANCH, GROUP_Q * tq)


def _pad_rows(a, nb, seq, tq):
    if seq == tq or seq % tq == 0:
        return a
    a = a.reshape(nb, seq, -1)
    return jnp.pad(a, ((0, 0), (0, tq - seq), (0, 0))).reshape(nb * tq, -1)


def _nsa_layer(x, lp, kvs, tables, nb, seq, tq, q_off, swa_base):
    q, qr = q_proj(x, lp['norm_pre'], lp['w_q'], tables)
    gates = gate_proj(x, lp['norm_pre'], lp['w_g'])
    nq = -(-seq // tq)
    o = nsa_attention(_pad_rows(q, nb, seq, tq), _pad_rows(qr, nb, seq, tq),
                      _gates_t(_pad_rows(gates, nb, seq, tq), nb, nq, tq),
                      *kvs, nb=nb, nq=nq, tq=tq, q_off=q_off, swa_base=swa_base)
    if nq * tq != seq:
        o = o.reshape(nb, nq * tq, Q_WIDTH)[:, :seq].reshape(nb * seq, Q_WIDTH)
    return oproj(o, lp['w_o'], lp['norm_post'], x)


def _trunk(x, nb, seq, pos0, h0_re, h0_im, past, prm):
    m = nb * seq
    ssm_re, ssm_im = [], []
    n_a = len(prm['a'])
    for layer in range(n_a):
        x, hr, hi = _s5_layer(x, None if h0_re is None else h0_re[layer],
                              None if h0_im is None else h0_im[layer], prm['a'][layer], nb, seq)
        ssm_re.append(hr)
        ssm_im.append(hi)
        ml = prm['mlp'][layer]
        x = mlp(x, ml['norm_pre'], ml['w_up'], ml['w_down'], ml['norm_post'])

    pos = pos0 + jnp.tile(jnp.arange(seq), nb)
    tables = _rope_tables(pos)
    kv = kv_proj(x, prm['kv_norm'], prm['w_kv'], tables)
    rows_cmp = kv[0].reshape(nb, seq, 2, N_KV, HEAD_DIM)
    rows_slc = kv[1].reshape(nb, seq, 2, N_KV, HEAD_DIM)
    rows_swa = kv[2].reshape(nb, seq, 2, N_KV, HEAD_DIM)
    no_tail = None
    if past is None:
        npg = seq // PAGE
        table = jnp.arange(nb * npg, dtype=jnp.int32).reshape(nb, npg)
        pages = lambda j: kv[j].reshape(nb * npg, PAGE_ROWS, HEAD_DIM)
        cproj = cmp_proj(pages(0), table, prm['cmp_w1k_cat'], prm['cmp_w1v_cat'])
        ks, vst = kv_pack(pages(1), table, no_tail, SLC_TILE)
        kw, vwt = kv_pack(pages(2), table, no_tail, PAGE)
        swa_base = 0
        tq = 256
        swa_buf = rows_swa[:, seq - WINDOW:]
    else:
        cache_cmp, cache_slc, state_swa, table = past
        n_pool = cache_cmp.shape[0]
        npg = table.shape[1]
        cproj = cmp_proj(cache_cmp.reshape(n_pool, PAGE_ROWS, HEAD_DIM), table, prm['cmp_w1k_cat'], prm['cmp_w1v_cat'])
        t_real = npg * PAGE + seq
        t_pad = -(-t_real // SLC_TILE) * SLC_TILE
        tail = jnp.pad(kv[1].reshape(nb, seq, KV_SLAB),
                       ((0, 0), (0, t_pad - npg * PAGE - seq), (0, 0)))
        ks, vst = kv_pack(cache_slc.reshape(n_pool, PAGE_ROWS, HEAD_DIM), table, tail, SLC_TILE)
        w_keep = state_swa.shape[1]
        local = jnp.concatenate([state_swa.reshape(nb, w_keep, KV_SLAB), rows_swa.reshape(nb, seq, KV_SLAB)], axis=1)
        swa_buf = local[:, -w_keep:].reshape(nb, w_keep, 2, N_KV, HEAD_DIM)
        tw = -(-max(local.shape[1], SWA_SPAN) // PAGE) * PAGE
        local = jnp.pad(local, ((0, 0), (0, tw - local.shape[1]), (0, 0)))
        lt = jnp.arange(nb * (tw // PAGE), dtype=jnp.int32).reshape(nb, tw // PAGE)
        kw, vwt = kv_pack(local.reshape(nb * (tw // PAGE), PAGE_ROWS, HEAD_DIM), lt, no_tail, PAGE)
        swa_base = pos0 - w_keep
        tq = 32
    cmp_out = cmp_mlp(cproj, prm['cmp_w1'], prm['cmp_pe'], prm['cmp_w2'])
    kc = cmp_out[:, 0].astype(BF16)
    vct = cmp_out[:, 1].transpose(0, 1, 3, 2).astype(BF16)
    kvs = (kc, vct, ks, vst, kw, vwt)

    for j, lp in enumerate(prm['b']):
        x = _nsa_layer(x, lp, kvs, tables, nb, seq, tq, pos0, swa_base)
        ml = prm['mlp'][n_a + j]
        x = mlp(x, ml['norm_pre'], ml['w_up'], ml['w_down'], ml['norm_post'])
    return x, rows_cmp, rows_slc, swa_buf, jnp.stack(ssm_re), jnp.stack(ssm_im)


def kernel(x_prompt, x_sample, cache_kv_cmp, cache_kv_slc, state_kv_swa, state_ssm_re, state_ssm_im, page_table,
           a_norm_pre, a_lam_re, a_lam_im, a_log_dt, a_b_re, a_b_im, a_c_re, a_c_im, a_d, a_w_glu, a_b_glu,
           a_norm_post, kv_norm, w_kv, cmp_w1_k, cmp_pe_k, cmp_w2_k, cmp_w1_v, cmp_pe_v, cmp_w2_v, b_norm_pre,
           b_w_qg, b_w_o, b_norm_post, mlp_norm_pre, mlp_w_up, mlp_w_down, mlp_norm_post):
    n_a = a_norm_pre.shape[0]
    n_b = b_norm_pre.shape[0]
    prm = {'a': [], 'b': [], 'mlp': []}
    pads = tuple(sorted({-x_prompt.shape[1] % S5_CHUNK, -x_sample.shape[1] % S5_CHUNK}))
    for l in range(n_a):
        ops = s5_prep(a_lam_re[l], a_lam_im[l], a_log_dt[l], a_b_re[l], a_b_im[l], a_c_re[l], a_c_im[l], pads)
        prm['a'].append(dict(norm_pre=a_norm_pre[l], norm_post=a_norm_post[l], d=a_d[l],
                             w_glu=a_w_glu[l].astype(BF16), b_glu=a_b_glu[l], **ops))
    n_gate = N_HEADS * N_BRANCH
    for l in range(n_b):
        w_g = jnp.pad(b_w_qg[l][:, Q_WIDTH:], ((0, 0), (0, HEAD_DIM - n_gate))).astype(BF16)
        prm['b'].append(dict(norm_pre=b_norm_pre[l], norm_post=b_norm_post[l],
                             w_q=b_w_qg[l][:, :Q_WIDTH].astype(BF16), w_g=w_g, w_o=b_w_o[l].astype(BF16)))
    for l in range(n_a + n_b):
        prm['mlp'].append(dict(norm_pre=mlp_norm_pre[l], norm_post=mlp_norm_post[l],
                               w_up=mlp_w_up[l].astype(BF16), w_down=mlp_w_down[l].astype(BF16)))
    prm['kv_norm'] = kv_norm
    prm['w_kv'] = w_kv.astype(BF16)
    r = CMP_BLOCK // CMP_STRIDE
    cat = lambda w: w.reshape(r, CMP_FLAT, CMP_HIDDEN).transpose(1, 0, 2).reshape(CMP_FLAT, r * CMP_HIDDEN).astype(BF16)
    prm['cmp_w1k_cat'] = cat(cmp_w1_k)
    prm['cmp_w1v_cat'] = cat(cmp_w1_v)
    flat = CMP_BLOCK * HEAD_DIM
    prm['cmp_w1'] = jnp.stack([cmp_w1_k.reshape(flat, CMP_HIDDEN), cmp_w1_v.reshape(flat, CMP_HIDDEN)])
    prm['cmp_pe'] = jnp.stack([cmp_pe_k.reshape(flat, 1), cmp_pe_v.reshape(flat, 1)])
    prm['cmp_w2'] = jnp.stack([cmp_w2_k, cmp_w2_v]).astype(BF16)

    bp, sp, _ = x_prompt.shape
    y_p, cmp_p, slc_p, swa_p, re_p, im_p = _trunk(x_prompt.reshape(bp * sp, D_MODEL), bp, sp, 0,
                                                  None, None, None, prm)
    bs, ss, _ = x_sample.shape
    past_len = page_table.shape[1] * PAGE
    y_s, cmp_s, slc_s, swa_s, re_s, im_s = _trunk(x_sample.reshape(bs * ss, D_MODEL), bs, ss, past_len,
                                                  state_ssm_re, state_ssm_im,
                                                  (cache_kv_cmp, cache_kv_slc, state_kv_swa, page_table), prm)
    return (y_p.reshape(bp, sp, D_MODEL), y_s.reshape(bs, ss, D_MODEL), cmp_p, cmp_s, slc_p, slc_s,
            swa_p, swa_s, re_p, im_p, re_s, im_s)
```

```python
import functools
import math

import jax
import jax.numpy as jnp
from jax import lax
from jax.experimental import pallas as pl
from jax.experimental.pallas import tpu as pltpu

F32 = jnp.float32
BF16 = jnp.bfloat16

D_MODEL = 2048
N_HEADS = 16
HEAD_DIM = 128
N_KV = 4
GROUP_Q = N_HEADS // N_KV
N_BRANCH = 3
ROT_DIM = HEAD_DIM // 4
ROPE_THETA = 500000.0
SSM_GROUP = 16
N_GROUPS = D_MODEL // SSM_GROUP
SSM_STATE = 64
S5_CHUNK = 16
D_FF = 4 * D_MODEL
CMP_BLOCK = 32
CMP_STRIDE = 16
CMP_HIDDEN = 2 * HEAD_DIM
SLC_BLOCK = 64
N_SELECT = 16
WINDOW = 512
PAGE = 128
FORCE_SCORE = 1.0e4
EPS = 1e-6
Q_WIDTH = N_HEADS * HEAD_DIM
GROUP_W = GROUP_Q * HEAD_DIM
KV_SLAB = 2 * N_KV * HEAD_DIM
PAGE_CHUNKS = KV_SLAB // HEAD_DIM
PAGE_ROWS = PAGE * PAGE_CHUNKS
SLC_TILE = 512
SWA_SPAN = WINDOW + PAGE
NEG = -1.0e30
VMEM_LIMIT = 56 * 1024 * 1024


def _cparams(sem):
    return pltpu.CompilerParams(dimension_semantics=sem, vmem_limit_bytes=VMEM_LIMIT)


def _rms(x, g):
    var = jnp.mean(x * x, axis=-1, keepdims=True)
    return x * lax.rsqrt(var + EPS) * g


def _gelu(x):
    return 0.5 * x * (1.0 + jnp.tanh(math.sqrt(2.0 / math.pi) * (x + 0.044715 * (x * x * x))))


def _sigmoid(x):
    return 1.0 / (1.0 + jnp.exp(-x))


def _dot(a, b):
    return jnp.dot(a, b, preferred_element_type=F32)


def _dot_nt(a, b):
    return lax.dot_general(a, b, (((1,), (1,)), ((), ())), preferred_element_type=F32)


def _rope128(x, c, s1, s2):
    return x * c + pltpu.roll(x, HEAD_DIM - ROT_DIM // 2, 1) * s1 + pltpu.roll(x, ROT_DIM // 2, 1) * s2


def _rope_tables(pos):
    half = ROT_DIM // 2
    inv = ROPE_THETA ** (-jnp.arange(half, dtype=F32) / half)
    ang = pos.astype(F32)[:, None] * inv[None, :]
    cos, sin = jnp.cos(ang), jnp.sin(ang)
    n = pos.shape[0]
    rest = HEAD_DIM - ROT_DIM
    c = jnp.concatenate([cos, cos, jnp.ones((n, rest), F32)], axis=1)
    s1 = jnp.concatenate([-sin, jnp.zeros((n, HEAD_DIM - half), F32)], axis=1)
    s2 = jnp.concatenate([jnp.zeros((n, half), F32), sin, jnp.zeros((n, rest), F32)], axis=1)
    return c, s1, s2


def _row_tile(m):
    return 512 if m % 512 == 0 else m


def _kv_proj_kernel(x_ref, g_ref, w_ref, c_ref, s1_ref, s2_ref, o_ref, xn_ref):
    j = pl.program_id(1)
    tm = x_ref.shape[0]

    @pl.when(j == 0)
    def _():
        xn_ref[...] = _rms(x_ref[...], g_ref[...]).astype(BF16)

    acc = _dot(xn_ref[...], w_ref[...])
    def emit(rope):
        c, s1, s2 = c_ref[...], s1_ref[...], s2_ref[...]
        for ch in range(PAGE_CHUNKS):
            val = acc[:, ch * HEAD_DIM:(ch + 1) * HEAD_DIM]
            if rope and ch < N_KV:
                val = _rope128(val, c, s1, s2)
            o_ref[0, pl.ds(ch, tm, stride=PAGE_CHUNKS), :] = val

    @pl.when(j >= 1)
    def _():
        emit(True)

    @pl.when(j == 0)
    def _():
        emit(False)


def kv_proj(x, g, w_bf, tables):
    m = x.shape[0]
    tm = _row_tile(m)
    n = w_bf.shape[1]
    tn = KV_SLAB
    c, s1, s2 = tables
    tab = pl.BlockSpec((tm, HEAD_DIM), lambda i, j: (i, 0))
    return pl.pallas_call(
        _kv_proj_kernel,
        grid=(m // tm, n // tn),
        in_specs=[pl.BlockSpec((tm, D_MODEL), lambda i, j: (i, 0)),
                  pl.BlockSpec((1, D_MODEL), lambda i, j: (0, 0)),
                  pl.BlockSpec((D_MODEL, tn), lambda i, j: (0, j)),
                  tab, tab, tab],
        out_specs=pl.BlockSpec((1, tm * PAGE_CHUNKS, HEAD_DIM), lambda i, j: (j, i, 0)),
        out_shape=jax.ShapeDtypeStruct((n // tn, m * PAGE_CHUNKS, HEAD_DIM), F32),
        scratch_shapes=[pltpu.VMEM((tm, D_MODEL), BF16)],
        compiler_params=_cparams(("parallel", "arbitrary")),
        name="kv_proj",
    )(x, g.reshape(1, -1), w_bf, c, s1, s2)


def _q_proj_kernel(x_ref, g_ref, w_ref, c_ref, s1_ref, s2_ref, q_ref, qr_ref, xn_ref):
    j = pl.program_id(1)

    @pl.when(j == 0)
    def _():
        xn_ref[...] = _rms(x_ref[...], g_ref[...]).astype(BF16)

    acc = _dot(xn_ref[...], w_ref[...])
    scale = HEAD_DIM ** -0.5
    c, s1, s2 = c_ref[...], s1_ref[...], s2_ref[...]
    q_ref[...] = (acc * scale).astype(BF16)
    for h in range(acc.shape[1] // HEAD_DIM):
        sl = slice(h * HEAD_DIM, (h + 1) * HEAD_DIM)
        qr_ref[:, sl] = (_rope128(acc[:, sl], c, s1, s2) * scale).astype(BF16)


def q_proj(x, g, wq_bf, tables):
    m = x.shape[0]
    tm = _row_tile(m)
    tn = Q_WIDTH
    c, s1, s2 = tables
    tab = pl.BlockSpec((tm, HEAD_DIM), lambda i, j: (i, 0))
    out = jax.ShapeDtypeStruct((m, Q_WIDTH), BF16)
    ospec = pl.BlockSpec((tm, tn), lambda i, j: (i, j))
    return pl.pallas_call(
        _q_proj_kernel,
        grid=(m // tm, Q_WIDTH // tn),
        in_specs=[pl.BlockSpec((tm, D_MODEL), lambda i, j: (i, 0)),
                  pl.BlockSpec((1, D_MODEL), lambda i, j: (0, 0)),
                  pl.BlockSpec((D_MODEL, tn), lambda i, j: (0, j)),
                  tab, tab, tab],
        out_specs=[ospec, ospec],
        out_shape=[out, out],
        scratch_shapes=[pltpu.VMEM((tm, D_MODEL), BF16)],
        compiler_params=_cparams(("parallel", "arbitrary")),
        name="q_proj",
    )(x, g.reshape(1, -1), wq_bf, c, s1, s2)


def _gate_proj_kernel(x_ref, g_ref, w_ref, o_ref):
    xn = _rms(x_ref[...], g_ref[...]).astype(BF16)
    o_ref[...] = _sigmoid(_dot(xn, w_ref[...]))


def gate_proj(x, g, wg_bf):
    m = x.shape[0]
    tm = _row_tile(m)
    n = wg_bf.shape[1]
    return pl.pallas_call(
        _gate_proj_kernel,
        grid=(m // tm,),
        in_specs=[pl.BlockSpec((tm, D_MODEL), lambda i: (i, 0)),
                  pl.BlockSpec((1, D_MODEL), lambda i: (0, 0)),
                  pl.BlockSpec((D_MODEL, n), lambda i: (0, 0))],
        out_specs=pl.BlockSpec((tm, n), lambda i: (i, 0)),
        out_shape=jax.ShapeDtypeStruct((m, n), F32),
        compiler_params=_cparams(("parallel",)),
        name="gate_proj",
    )(x, g.reshape(1, -1), wg_bf)


def _rms_norm_kernel(x_ref, g_ref, o_ref):
    o_ref[...] = _rms(x_ref[...], g_ref[...])


def rms_norm(x, g):
    m = x.shape[0]
    tm = _row_tile(m)
    return pl.pallas_call(
        _rms_norm_kernel,
        grid=(m // tm,),
        in_specs=[pl.BlockSpec((tm, D_MODEL), lambda i: (i, 0)),
                  pl.BlockSpec((1, D_MODEL), lambda i: (0, 0))],
        out_specs=pl.BlockSpec((tm, D_MODEL), lambda i: (i, 0)),
        out_shape=jax.ShapeDtypeStruct((m, D_MODEL), F32),
        compiler_params=_cparams(("parallel",)),
        name="rms_norm",
    )(x, g.reshape(1, -1))


def _mlp_kernel(x_ref, gpre_ref, wup_ref, wdn_ref, gpost_ref, o_ref, xn_ref, acc_ref):
    j = pl.program_id(1)

    @pl.when(j == 0)
    def _():
        xn_ref[...] = _rms(x_ref[...], gpre_ref[...]).astype(BF16)
        acc_ref[...] = jnp.zeros_like(acc_ref)

    h = jnp.maximum(_dot(xn_ref[...], wup_ref[...]), 0.0)
    acc_ref[...] += _dot((h * h).astype(BF16), wdn_ref[...])

    @pl.when(j == pl.num_programs(1) - 1)
    def _():
        o_ref[...] = x_ref[...] + _rms(acc_ref[...], gpost_ref[...])


def mlp(x, gpre, wup_bf, wdn_bf, gpost):
    m = x.shape[0]
    tm = _row_tile(m)
    tf = 1024
    return pl.pallas_call(
        _mlp_kernel,
        grid=(m // tm, D_FF // tf),
        in_specs=[pl.BlockSpec((tm, D_MODEL), lambda i, j: (i, 0)),
                  pl.BlockSpec((1, D_MODEL), lambda i, j: (0, 0)),
                  pl.BlockSpec((D_MODEL, tf), lambda i, j: (0, j)),
                  pl.BlockSpec((tf, D_MODEL), lambda i, j: (j, 0)),
                  pl.BlockSpec((1, D_MODEL), lambda i, j: (0, 0))],
        out_specs=pl.BlockSpec((tm, D_MODEL), lambda i, j: (i, 0)),
        out_shape=jax.ShapeDtypeStruct((m, D_MODEL), F32),
        scratch_shapes=[pltpu.VMEM((tm, D_MODEL), BF16), pltpu.VMEM((tm, D_MODEL), F32)],
        compiler_params=_cparams(("parallel", "arbitrary")),
        name="mlp",
    )(x, gpre.reshape(1, -1), wup_bf, wdn_bf, gpost.reshape(1, -1))


def _oproj_kernel(o_ref, w_ref, g_ref, res_ref, out_ref):
    out_ref[...] = res_ref[...] + _rms(_dot(o_ref[...], w_ref[...]), g_ref[...])


def oproj(o_bf, w_bf, g, res):
    m = o_bf.shape[0]
    tm = _row_tile(m)
    return pl.pallas_call(
        _oproj_kernel,
        grid=(m // tm,),
        in_specs=[pl.BlockSpec((tm, Q_WIDTH), lambda i: (i, 0)),
                  pl.BlockSpec((Q_WIDTH, D_MODEL), lambda i: (0, 0)),
                  pl.BlockSpec((1, D_MODEL), lambda i: (0, 0)),
                  pl.BlockSpec((tm, D_MODEL), lambda i: (i, 0))],
        out_specs=pl.BlockSpec((tm, D_MODEL), lambda i: (i, 0)),
        out_shape=jax.ShapeDtypeStruct((m, D_MODEL), F32),
        compiler_params=_cparams(("parallel",)),
        name="oproj",
    )(o_bf, w_bf, g.reshape(1, -1), res)


def _glu_kernel(x_ref, y_ref, gpre_ref, d_ref, w_ref, b_ref, gpost_ref, o_ref):
    x = x_ref[...]
    xn = _rms(x, gpre_ref[...])
    y = _gelu(y_ref[...] + d_ref[...] * xn)
    z = _dot(y.astype(BF16), w_ref[...]) + b_ref[...]
    o_ref[...] = x + _rms(y * _sigmoid(z), gpost_ref[...])


def glu_tail(x, y_ssm, gpre, d_skip, w_bf, b, gpost):
    m = x.shape[0]
    tm = _row_tile(m)
    vec = pl.BlockSpec((1, D_MODEL), lambda i: (0, 0))
    row = pl.BlockSpec((tm, D_MODEL), lambda i: (i, 0))
    return pl.pallas_call(
        _glu_kernel,
        grid=(m // tm,),
        in_specs=[row, row, vec, vec, pl.BlockSpec((D_MODEL, D_MODEL), lambda i: (0, 0)), vec, vec],
        out_specs=row,
        out_shape=jax.ShapeDtypeStruct((m, D_MODEL), F32),
        compiler_params=_cparams(("parallel",)),
        name="glu_tail",
    )(x, y_ssm, gpre.reshape(1, -1), d_skip.reshape(1, -1), w_bf, b.reshape(1, -1), gpost.reshape(1, -1))


S5_GB = 8
S5_SW = S5_GB * 2 * SSM_STATE


def _s5_prep_kernel(lrl_ref, lil_ref, ldt_ref, lrc_ref, lic_ref, btr_ref, bti_ref, ccat_ref, ctr_ref, cti_ref,
                    wlag_ref, winj_ref, dec_ref, *wcar_refs, pads):
    t = S5_CHUNK
    dt = jnp.exp(ldt_ref[...])
    lane = lax.broadcasted_iota(jnp.int32, (1, 1, HEAD_DIM), 2)
    re_half = lane < SSM_STATE
    lr, li = lrl_ref[...], lil_ref[...]
    ldr, ldi = lr * dt, li * dt
    mag = jnp.exp(ldr)
    a_re, a_im = mag * jnp.cos(ldi), mag * jnp.sin(ldi)
    den = lr * lr + li * li
    nr = a_re - 1.0
    f_re = (nr * lr + a_im * li) / den
    f_im = (a_im * lr - nr * li) / den
    bt_re = jnp.concatenate([btr_ref[...]] * t, axis=1)
    bt_im = jnp.concatenate([bti_ref[...]] * t, axis=1)
    bb_re = f_re * bt_re - f_im * bt_im
    bb_im = f_re * bt_im + f_im * bt_re
    k = lax.broadcasted_iota(jnp.int32, (1, t, 1), 1).astype(F32)
    mk = jnp.exp(ldr * k)
    per_j = lambda a: jnp.broadcast_to(a[:, :, None, :], (S5_GB, t, SSM_GROUP, HEAD_DIM)).reshape(S5_GB, t * SSM_GROUP, HEAD_DIM)
    ak_re, ak_im = per_j(mk * jnp.cos(ldi * k)), per_j(mk * jnp.sin(ldi * k))
    xc = jnp.where(re_half, ak_re * bb_re - ak_im * bb_im, ak_re * bb_im + ak_im * bb_re)
    kt = jnp.einsum('gnp,gpl->gnl', xc, ccat_ref[...], precision=lax.Precision.HIGHEST,
                    preferred_element_type=F32)
    own = lane // SSM_GROUP
    winj_ref[...] = jnp.zeros_like(winj_ref)
    for g in range(S5_GB):
        kg = jnp.where(own[0] == g, kt[g], 0.0).astype(BF16)
        xg = xc[g].astype(BF16)
        for s in range(t):
            src = slice((t - 1 - s) * SSM_GROUP, (t - s) * SSM_GROUP)
            dst = slice(s * HEAD_DIM + g * SSM_GROUP, s * HEAD_DIM + (g + 1) * SSM_GROUP)
            wlag_ref[0, dst, :] = kg[src]
            winj_ref[0, dst, g * HEAD_DIM:(g + 1) * HEAD_DIM] = xg[src]
    for n, pad in enumerate(pads):
        nn = float(t - pad)
        mn = jnp.exp(ldr * nn)
        an_re, an_im = mn * jnp.cos(ldi * nn), mn * jnp.sin(ldi * nn)
        dec_ref[:, 2 * n:2 * n + 1, :] = an_re
        dec_ref[:, 2 * n + 1:2 * n + 2, :] = jnp.where(re_half, -an_im, an_im)
    lrc, lic = lrc_ref[...], lic_ref[...]
    magc = jnp.exp(lrc * dt)
    ac_re, ac_im = magc * jnp.cos(lic * dt), magc * jnp.sin(lic * dt)
    top = lax.broadcasted_iota(jnp.int32, (1, HEAD_DIM, 1), 1) < SSM_STATE
    ctr, cti = ctr_ref[...], cti_ref[...]
    for ref, pad in zip(wcar_refs, pads):
        if pad:
            ref[...] = jnp.zeros_like(ref)
    p_re, p_im = ac_re, ac_im
    for e in range(t):
        w = jnp.where(top, ctr * p_re - cti * p_im, -(ctr * p_im + cti * p_re))
        for g in range(S5_GB):
            wg = jnp.where(own[0] == g, w[g], 0.0).astype(BF16)
            for ref, pad in zip(wcar_refs, pads):
                if e + pad < t:
                    ref[0, e + pad, g * HEAD_DIM:(g + 1) * HEAD_DIM, :] = wg
        p_re, p_im = p_re * ac_re - p_im * ac_im, p_re * ac_im + p_im * ac_re


def s5_prep(lam_re, lam_im, log_dt, b_re, b_im, c_re, c_im, pads):
    g, p = lam_re.shape
    gb, t = S5_GB, S5_CHUNK
    n_blk = g // gb
    twice = lambda a, axis: jnp.concatenate([a, a], axis=axis)
    lanes = lambda a: twice(a, 1).reshape(g, 1, 2 * p)
    cols = lambda a: twice(a, 1).reshape(g, 2 * p, 1)
    bt = lambda a: twice(a.transpose(0, 2, 1), 2)
    crt, cit = c_re.transpose(0, 2, 1), c_im.transpose(0, 2, 1)
    rep = (jnp.arange(SSM_GROUP)[:, None] == jnp.arange(HEAD_DIM)[None, :] % SSM_GROUP).astype(F32)
    over_h = lambda a: jnp.einsum('gpi,il->gpl', a, rep, precision=lax.Precision.HIGHEST)
    ccat = over_h(jnp.concatenate([crt, -cit], axis=1))
    blk3 = lambda s: pl.BlockSpec((gb,) + s, lambda i: (i, 0, 0))
    sq = blk3((2 * p, HEAD_DIM))
    car_spec = pl.BlockSpec((1, t, gb * HEAD_DIM, HEAD_DIM), lambda i: (i, 0, 0, 0))
    car_shape = jax.ShapeDtypeStruct((n_blk, t, gb * HEAD_DIM, HEAD_DIM), BF16)
    outs = pl.pallas_call(
        functools.partial(_s5_prep_kernel, pads=pads),
        grid=(n_blk,),
        in_specs=[blk3((1, 2 * p)), blk3((1, 2 * p)), blk3((1, 1)), blk3((2 * p, 1)), blk3((2 * p, 1)),
                  blk3((SSM_GROUP, 2 * p)), blk3((SSM_GROUP, 2 * p)), sq, sq, sq],
        out_specs=[pl.BlockSpec((1, t * HEAD_DIM, HEAD_DIM), lambda i: (i, 0, 0)),
                   pl.BlockSpec((1, t * HEAD_DIM, S5_SW), lambda i: (i, 0, 0)),
                   blk3((2 * len(pads), 2 * p))] + [car_spec] * len(pads),
        out_shape=[jax.ShapeDtypeStruct((n_blk, t * HEAD_DIM, HEAD_DIM), BF16),
                   jax.ShapeDtypeStruct((n_blk, t * HEAD_DIM, S5_SW), BF16),
                   jax.ShapeDtypeStruct((g, 2 * len(pads), 2 * p), F32)] + [car_shape] * len(pads),
        compiler_params=_cparams(("parallel",)),
        name="s5_prep",
    )(lanes(lam_re), lanes(lam_im), log_dt.reshape(g, 1, 1), cols(lam_re), cols(lam_im), bt(b_re), bt(b_im),
      ccat, over_h(twice(crt, 1)), over_h(twice(cit, 1)))
    w_lag, w_inj, dec = outs[:3]
    dec = dec.reshape(n_blk, gb, 2 * len(pads), 2 * p).transpose(2, 0, 1, 3).reshape(2 * len(pads), n_blk, 1, S5_SW)
    return dict(w_lag=w_lag, w_inj=w_inj,
                w_car={pad: outs[3 + n] for n, pad in enumerate(pads)},
                a1={pad: dec[2 * n] for n, pad in enumerate(pads)},
                a2={pad: dec[2 * n + 1] for n, pad in enumerate(pads)})


def _swap_halves(x):
    lane = lax.broadcasted_iota(jnp.int32, (1, x.shape[-1]), 1)
    return jnp.where(lane % (2 * SSM_STATE) < SSM_STATE,
                     pltpu.roll(x, x.shape[-1] - SSM_STATE, x.ndim - 1), pltpu.roll(x, SSM_STATE, x.ndim - 1))


def _s5_block_kernel(x_ref, wlag_ref, winj_ref, wcar_ref, a1_ref, a2_ref, a2s_ref, h0_ref, y_ref, hf_ref,
                     xc_ref, s_ref, ss_ref, hp_ref, *, nbk, nc):
    mc = nbk * nc
    t = S5_CHUNK
    for s in range(t):
        xc_ref[:, s * HEAD_DIM:(s + 1) * HEAD_DIM] = x_ref[pl.ds(s, mc, stride=t), :].astype(BF16)
    inc = _dot(xc_ref[...], winj_ref[0])
    s_ref[...] = inc
    ss_ref[...] = _swap_halves(inc)
    a1, a2, a2s = a1_ref[0], a2_ref[0], a2s_ref[0]
    h0 = tuple((h0_ref[0, 0, b:b + 1, :], _swap_halves(h0_ref[0, 0, b:b + 1, :])) for b in range(nbk))

    def step(c, hs):
        out = []
        for b in range(nbk):
            h, hx = hs[b]
            row = b * nc + c
            hp_ref[pl.ds(row, 1), :] = h
            out.append((a1 * h + a2 * hx + s_ref[pl.ds(row, 1), :],
                        a1 * hx + a2s * h + ss_ref[pl.ds(row, 1), :]))
        return tuple(out)

    hs = lax.fori_loop(0, nc, step, h0)
    for b in range(nbk):
        hf_ref[0, 0, b:b + 1, :] = hs[b][0]
    hp = hp_ref[...].astype(BF16)
    for tt in range(t):
        y = (_dot(xc_ref[:, 0:(tt + 1) * HEAD_DIM], wlag_ref[0, (t - 1 - tt) * HEAD_DIM:, :])
             + _dot(hp, wcar_ref[0, tt]))
        y_ref[pl.ds(tt, mc, stride=t), :] = y


def s5_block(xn, w_lag, w_inj, w_car, a1, a2, a2s, h0, nbk, nc):
    m = xn.shape[0]
    rows = nbk * nc * S5_CHUNK
    n_step = m // rows
    n_blk = N_GROUPS // S5_GB
    kdim = S5_CHUNK * HEAD_DIM
    wspec = lambda a: pl.BlockSpec((1,) + a.shape[1:], lambda g, b: (g,) + (0,) * (a.ndim - 1))
    hspec = pl.BlockSpec((1, 1, nbk, S5_SW), lambda g, b: (g, b, 0, 0))
    xspec = pl.BlockSpec((rows, HEAD_DIM), lambda g, b: (b, g))
    return pl.pallas_call(
        functools.partial(_s5_block_kernel, nbk=nbk, nc=nc),
        grid=(n_blk, n_step),
        in_specs=[xspec, wspec(w_lag), wspec(w_inj), wspec(w_car), wspec(a1), wspec(a2), wspec(a2s), hspec],
        out_specs=[xspec, hspec],
        out_shape=[jax.ShapeDtypeStruct((m, D_MODEL), F32), jax.ShapeDtypeStruct(h0.shape, F32)],
        scratch_shapes=[pltpu.VMEM((nbk * nc, kdim), BF16)] + [pltpu.VMEM((nbk * nc, S5_SW), F32)] * 3,
        compiler_params=_cparams(("parallel", "arbitrary")),
        name="s5_block",
    )(xn, w_lag, w_inj, w_car, a1, a2, a2s, h0)


CMP_PAGES = 16
CMP_ROWS = CMP_PAGES * (PAGE // CMP_STRIDE)
CMP_FLAT = CMP_STRIDE * HEAD_DIM


def _page_specs(n, first):
    def spec(k):
        return pl.BlockSpec((1, PAGE_ROWS, HEAD_DIM), lambda b, p, pt: (pt[b, first(p) + k], 0, 0))
    return [spec(k) for k in range(n)]


def _cmp_proj_kernel(pt_ref, *refs):
    x_refs = refs[:CMP_PAGES]
    wk_ref, wv_ref, o_ref, lhs_ref = refs[CMP_PAGES:]
    per_page = PAGE // CMP_STRIDE
    for pg, x_ref in enumerate(x_refs):
        for kv in range(2):
            for g in range(N_KV):
                c = kv * N_KV + g
                for s in range(CMP_STRIDE):
                    piece = x_ref[0, pl.ds(s * PAGE_CHUNKS + c, per_page, stride=CMP_STRIDE * PAGE_CHUNKS), :]
                    lhs_ref[kv, g, pg * per_page:(pg + 1) * per_page, s * HEAD_DIM:(s + 1) * HEAD_DIM] = piece
    for kv, w_ref in ((0, wk_ref), (1, wv_ref)):
        for g in range(N_KV):
            o_ref[0, kv, g] = _dot(lhs_ref[kv, g].astype(BF16), w_ref[...])


def cmp_proj(pages, page_table, wk_cat, wv_cat):
    nb, npg = page_table.shape
    assert npg % CMP_PAGES == 0, "compression consumes whole groups of pages"
    n_ch = npg * (PAGE // CMP_STRIDE)
    wspec = pl.BlockSpec((CMP_FLAT, 2 * CMP_HIDDEN), lambda b, p, pt: (0, 0))
    grid_spec = pltpu.PrefetchScalarGridSpec(
        num_scalar_prefetch=1,
        grid=(nb, npg // CMP_PAGES),
        in_specs=_page_specs(CMP_PAGES, lambda p: p * CMP_PAGES) + [wspec, wspec],
        out_specs=pl.BlockSpec((1, 2, N_KV, CMP_ROWS, 2 * CMP_HIDDEN), lambda b, p, pt: (b, 0, 0, p, 0)),
        scratch_shapes=[pltpu.VMEM((2, N_KV, CMP_ROWS, CMP_FLAT), F32)],
    )
    return pl.pallas_call(
        _cmp_proj_kernel,
        grid_spec=grid_spec,
        out_shape=jax.ShapeDtypeStruct((nb, 2, N_KV, n_ch, 2 * CMP_HIDDEN), F32),
        compiler_params=_cparams(("parallel", "arbitrary")),
        name="cmp_proj",
    )(page_table, *([pages] * CMP_PAGES), wk_cat, wv_cat)


def _cmp_mlp_kernel(p_ref, w1_ref, pe_ref, w2_ref, o_ref):
    proj = p_ref[0, 0, 0]
    n_ch = proj.shape[0]
    pre0 = jnp.sum(pe_ref[0] * w1_ref[0], axis=0, keepdims=True)
    first = proj[:, 0:CMP_HIDDEN]
    second = pltpu.roll(proj[:, CMP_HIDDEN:2 * CMP_HIDDEN], n_ch - 1, 0)
    pre = (pre0 + first) + second
    o_ref[0, 0, 0] = _dot(_gelu(pre).astype(BF16), w2_ref[0])


def cmp_mlp(proj, w1, pe, w2_bf):
    nb, _, _, n_ch, _ = proj.shape
    flat = CMP_BLOCK * HEAD_DIM
    return pl.pallas_call(
        _cmp_mlp_kernel,
        grid=(nb, 2, N_KV),
        in_specs=[pl.BlockSpec((1, 1, 1, n_ch, 2 * CMP_HIDDEN), lambda b, k, g: (b, k, g, 0, 0)),
                  pl.BlockSpec((1, flat, CMP_HIDDEN), lambda b, k, g: (k, 0, 0)),
                  pl.BlockSpec((1, flat, 1), lambda b, k, g: (k, 0, 0)),
                  pl.BlockSpec((1, CMP_HIDDEN, HEAD_DIM), lambda b, k, g: (k, 0, 0))],
        out_specs=pl.BlockSpec((1, 1, 1, n_ch, HEAD_DIM), lambda b, k, g: (b, k, g, 0, 0)),
        out_shape=jax.ShapeDtypeStruct((nb, 2, N_KV, n_ch, HEAD_DIM), F32),
        compiler_params=_cparams(("parallel", "parallel", "parallel")),
        name="cmp_mlp",
    )(proj, w1, pe, w2_bf)


def _kv_pack_kernel(pt_ref, *refs, per, n_groups):
    x_refs = refs[:per]
    t_ref, k_ref, vt_ref = refs[per:]
    j = pl.program_id(1)

    def emit(head):
        for g in range(N_KV):
            for pg in range(per):
                sl = slice(pg * PAGE, (pg + 1) * PAGE)
                k_ref[0, g, sl, :] = head(pg, g).astype(BF16)
                vt_ref[0, g, 0, :, sl] = head(pg, N_KV + g).T.astype(BF16)

    @pl.when(j < n_groups)
    def _():
        emit(lambda pg, c: x_refs[pg][0, pl.ds(c, PAGE, stride=PAGE_CHUNKS), :])

    @pl.when(j >= n_groups)
    def _():
        emit(lambda pg, c: t_ref[0, pl.ds(pg * PAGE_ROWS + c, PAGE, stride=PAGE_CHUNKS), :])


def kv_pack(pages, page_table, tail, vt_tile):
    nb, n_pages = page_table.shape
    per = vt_tile // PAGE
    assert n_pages % per == 0
    n_groups = n_pages // per
    if tail is None:
        n_tail = 0
        tail = jnp.zeros((nb, per * PAGE_ROWS, HEAD_DIM), F32)
    else:
        n_tail = tail.shape[1] // vt_tile
        tail = tail.reshape(nb, n_tail * per * PAGE_ROWS, HEAD_DIM)
    n_tot = n_groups + n_tail
    grid_spec = pltpu.PrefetchScalarGridSpec(
        num_scalar_prefetch=1,
        grid=(nb, n_tot),
        in_specs=_page_specs(per, lambda p: jnp.minimum(p, n_groups - 1) * per)
        + [pl.BlockSpec((1, per * PAGE_ROWS, HEAD_DIM), lambda b, p, pt: (b, jnp.maximum(p - n_groups, 0), 0))],
        out_specs=[pl.BlockSpec((1, N_KV, vt_tile, HEAD_DIM), lambda b, p, pt: (b, 0, p, 0)),
                   pl.BlockSpec((1, N_KV, 1, HEAD_DIM, vt_tile), lambda b, p, pt: (b, 0, p, 0, 0))],
    )
    return pl.pallas_call(
        functools.partial(_kv_pack_kernel, per=per, n_groups=n_groups),
        grid_spec=grid_spec,
        out_shape=[jax.ShapeDtypeStruct((nb, N_KV, n_tot * vt_tile, HEAD_DIM), BF16),
                   jax.ShapeDtypeStruct((nb, N_KV, n_tot, HEAD_DIM, vt_tile), BF16)],
        compiler_params=_cparams(("parallel", "arbitrary")),
        name="kv_pack",
    )(page_table, *([pages] * per), tail)


def _split3(x):
    hi = x.astype(BF16)
    r1 = x - hi.astype(F32)
    mid = r1.astype(BF16)
    lo = (r1 - mid.astype(F32)).astype(BF16)
    return hi, mid, lo


def _softmax_cols(s, col_ok):
    m = jnp.max(s, axis=0, keepdims=True)
    e = jnp.exp(s - m)
    den = jnp.sum(e, axis=0, keepdims=True)
    return e * jnp.where(col_ok, 1.0 / den, 0.0)


def _nsa_kernel(q_ref, qr_ref, gt_ref, kc_ref, vct_ref, ks_ref, vst_ref, kw_ref, vwt_ref, o_ref,
                bias_ref, s_ref, m_ref, l_ref, acc_ref, *, tq, q_off, swa_base):
    rq = GROUP_Q * tq
    w = max(tq, min(rq, HEAD_DIM))
    span = WINDOW + max(tq, PAGE)
    reps = rq // w
    n_cp = kc_ref.shape[2]
    n_tiles = bias_ref.shape[0]
    per = SLC_TILE // SLC_BLOCK
    n_blk = n_tiles * per
    tw = kw_ref.shape[2]
    i = pl.program_id(2)
    t0 = q_off + i * tq
    t_w = t0 + lax.broadcasted_iota(jnp.int32, (1, w), 1) % tq

    def tile(x):
        return jnp.concatenate([x] * reps, axis=x.ndim - 1) if reps > 1 else x

    def rows(ref):
        x = ref[...]
        return jnp.concatenate([x[:, r * HEAD_DIM:(r + 1) * HEAD_DIM] for r in range(GROUP_Q)], axis=0)

    q2 = rows(q_ref)
    qr2 = rows(qr_ref)
    t_rq = tile(t_w)

    n_io = lax.broadcasted_iota(jnp.int32, (n_cp, 1), 0)
    bias_c = jnp.where((n_io * CMP_STRIDE + (CMP_BLOCK - 1)) <= t_w, 0.0, NEG)
    p_c = _softmax_cols(_dot_nt(kc_ref[0, 0], q2) + tile(bias_c), t_rq >= CMP_BLOCK - 1)
    o_c = _dot(vct_ref[0, 0], p_c.astype(BF16))
    if reps > 1:
        p_sum = p_c[:, 0:w]
        for r in range(1, reps):
            p_sum = p_sum + p_c[:, r * w:(r + 1) * w]
    else:
        p_sum = p_c
        for r in range(1, GROUP_Q):
            p_sum = p_sum + pltpu.roll(p_c, r * tq, 1)
    s_col = lax.broadcasted_iota(jnp.int32, (n_blk, 1), 0)
    n_row = lax.broadcasted_iota(jnp.int32, (1, n_cp), 1)
    ov = jnp.logical_and(n_row * CMP_STRIDE < (s_col + 1) * SLC_BLOCK,
                         n_row * CMP_STRIDE + CMP_BLOCK > s_col * SLC_BLOCK)
    ov = jnp.where(ov, 1.0, 0.0).astype(BF16)
    hi, mid, lo = _split3(p_sum)
    imp = (_dot(ov, hi) + _dot(ov, mid)) + _dot(ov, lo)
    cur = t_w // SLC_BLOCK
    forced = jnp.logical_or(s_col == 0, jnp.logical_or(s_col == cur, s_col == cur - 1))
    elig = s_col * SLC_BLOCK <= t_w
    imp = jnp.where(forced, FORCE_SCORE, imp)
    imp = jnp.where(elig, imp, -jnp.inf)

    s_colf = s_col.astype(F32)

    def pick(_, carry):
        work, sel = carry
        best = jnp.max(work, axis=0, keepdims=True)
        first = jnp.min(jnp.where(work == best, s_colf, float(n_blk)), axis=0, keepdims=True)
        hit = s_colf == first
        return jnp.where(hit, -jnp.inf, work), jnp.where(hit, 1.0, sel)

    _, sel = lax.fori_loop(0, N_SELECT, pick, (imp, jnp.zeros((n_blk, w), F32)))
    bias_s = jnp.where(jnp.logical_and(elig, sel > 0.5), 0.0, NEG).reshape(n_tiles, per, w)
    bias_s = jnp.concatenate([bias_s, jnp.zeros_like(bias_s)], axis=1).astype(BF16)
    bias_ref[...] = tile(bias_s)

    m_ref[...] = jnp.full_like(m_ref, NEG)
    l_ref[...] = jnp.zeros_like(l_ref)
    acc_ref[...] = jnp.zeros_like(acc_ref)
    qr_t = qr2.astype(F32).T.astype(BF16)
    k_io = lax.broadcasted_iota(jnp.int32, (SLC_TILE, 1), 0)
    c_io = lax.broadcasted_iota(jnp.int32, (1, HEAD_DIM), 1)
    onehot = jnp.where(k_io // SLC_BLOCK == c_io, 1.0, 0.0).astype(BF16)
    zpad = jnp.zeros((HEAD_DIM - 2 * per, rq), BF16)

    def scores(kt):
        base = pl.multiple_of(kt * SLC_TILE, SLC_TILE)
        lhs = jnp.concatenate([ks_ref[0, 0, pl.ds(base, SLC_TILE), :], onehot], axis=1)
        rhs = jnp.concatenate([qr_t, bias_ref[kt], zpad], axis=0)
        return _dot(lhs, rhs)

    def update(kt, ss):
        m_old = m_ref[...]
        m_new = m_old
        for s in ss:
            m_new = jnp.maximum(m_new, jnp.max(s, axis=0, keepdims=True))
        alpha = jnp.exp(m_old - m_new)
        l_new = alpha * l_ref[...]
        acc = alpha * acc_ref[...]
        for j, s in enumerate(ss):
            e = jnp.exp(s - m_new)
            l_new = l_new + jnp.sum(e, axis=0, keepdims=True)
            acc = acc + _dot(vst_ref[0, 0, kt + j], e.astype(BF16))
        l_ref[...] = l_new
        acc_ref[...] = acc
        m_ref[...] = m_new

    k_last = t0 // SLC_TILE
    causal = tile(jnp.where((k_last * SLC_TILE + k_io) <= t_w, 0.0, NEG))
    if rq >= SLC_TILE:
        sa_ref, sb_ref = s_ref.at[0], s_ref.at[1]
        sa_ref[...] = scores(0)

        def body(i, carry):
            kt = 2 * i
            sb_ref[...] = scores(kt + 1)
            update(kt, [sa_ref[...]])
            sa_ref[...] = scores(kt + 2)
            update(kt + 1, [sb_ref[...]])
            return carry

        n_pair = k_last // 2
        lax.fori_loop(0, n_pair, body, 0)

        @pl.when(k_last % 2 == 0)
        def _():
            update(k_last, [sa_ref[...] + causal])

        @pl.when(k_last % 2 == 1)
        def _():
            sb_ref[...] = scores(k_last)
            update(k_last - 1, [sa_ref[...]])
            update(k_last, [sb_ref[...] + causal])
    else:
        joint = SLC_TILE // rq
        n_step = k_last // joint
        sa_ref, sb_ref = s_ref.at[0], s_ref.at[1]

        def put(ref, i):
            for j in range(joint):
                ref[j * SLC_TILE:(j + 1) * SLC_TILE, :] = scores(i * joint + j)

        def get(ref):
            return [ref[j * SLC_TILE:(j + 1) * SLC_TILE, :] for j in range(joint)]

        put(sa_ref, 0)

        def body(i, carry):
            put(sb_ref, 2 * i + 1)
            update(2 * i * joint, get(sa_ref))
            put(sa_ref, 2 * i + 2)
            update((2 * i + 1) * joint, get(sb_ref))
            return carry

        n_pair = jnp.maximum(n_step - 1, 0) // 2
        lax.fori_loop(0, n_pair, body, 0)

        @pl.when(n_step % 2 == 1)
        def _():
            update((n_step - 1) * joint, get(sa_ref))

        @pl.when(jnp.logical_and(n_step % 2 == 0, n_step > 0))
        def _():
            put(sb_ref, n_step - 1)
            update((n_step - 2) * joint, get(sa_ref))
            update((n_step - 1) * joint, get(sb_ref))

        def single(kt, carry):
            update(kt, [scores(kt)])
            return carry

        lax.fori_loop(n_step * joint, k_last, single, 0)
        update(k_last, [scores(k_last) + causal])
    o_s = acc_ref[...] * (1.0 / l_ref[...])

    start = jnp.clip(t0 - WINDOW - swa_base, 0, tw - span)
    start = pl.multiple_of(start, PAGE)
    key_pos = swa_base + start + lax.broadcasted_iota(jnp.int32, (span, 1), 0)
    dist = t_w - key_pos
    ok_w = jnp.logical_and(jnp.logical_and(dist >= 0, dist < WINDOW), key_pos >= swa_base)
    sw = _dot_nt(kw_ref[0, 0, pl.ds(start, span), :], qr2) + tile(jnp.where(ok_w, 0.0, NEG))
    p_w = _softmax_cols(sw, True).astype(BF16)
    o_w = jnp.zeros((HEAD_DIM, rq), F32)
    for jt in range(span // PAGE):
        o_w = o_w + _dot(vwt_ref[0, 0, start // PAGE + jt], p_w[jt * PAGE:(jt + 1) * PAGE, :])

    gt = gt_ref[0, 0, 0]
    o_t = o_c * gt[0:1, :] + o_s * gt[1:2, :] + o_w * gt[2:3, :]
    for c in range(rq // HEAD_DIM):
        blk = o_t[:, c * HEAD_DIM:(c + 1) * HEAD_DIM].T.astype(BF16)
        if tq >= HEAD_DIM:
            r, q0 = divmod(c * HEAD_DIM, tq)
            o_ref[q0:q0 + HEAD_DIM, r * HEAD_DIM:(r + 1) * HEAD_DIM] = blk
        else:
            per_blk = HEAD_DIM // tq
            for rr in range(per_blk):
                r = c * per_blk + rr
                o_ref[:, r * HEAD_DIM:(r + 1) * HEAD_DIM] = blk[rr * tq:(rr + 1) * tq, :]


def nsa_attention(q, qr, gates_t, kc, vct, ks, vst, kw, vwt, *, nb, nq, tq, q_off, swa_base):
    rq = GROUP_Q * tq
    tk = ks.shape[2]
    assert SLC_TILE % tq == 0 and q_off % SLC_TILE == 0 and tk % SLC_TILE == 0
    qspec = pl.BlockSpec((tq, GROUP_W), lambda b, g, i: (b * nq + i, g))
    full = lambda a: pl.BlockSpec((1, 1) + a.shape[2:], lambda b, g, i: (b, g) + (0,) * (a.ndim - 2))
    return pl.pallas_call(
        functools.partial(_nsa_kernel, tq=tq, q_off=q_off, swa_base=swa_base),
        grid=(nb, N_KV, nq),
        in_specs=[qspec, qspec,
                  pl.BlockSpec((1, 1, 1, N_BRANCH, rq), lambda b, g, i: (b, g, i, 0, 0)),
                  full(kc), full(vct), full(ks), full(vst), full(kw), full(vwt)],
        out_specs=qspec,
        out_shape=jax.ShapeDtypeStruct((nb * nq * tq, Q_WIDTH), BF16),
        scratch_shapes=[pltpu.VMEM((tk // SLC_TILE, 2 * (SLC_TILE // SLC_BLOCK), rq), BF16),
                        pltpu.VMEM((2, SLC_TILE * max(1, SLC_TILE // rq), rq), F32),
                        pltpu.VMEM((1, rq), F32), pltpu.VMEM((1, rq), F32), pltpu.VMEM((HEAD_DIM, rq), F32)],
        compiler_params=_cparams(("parallel", "parallel", "arbitrary")),
        name="nsa_attention",
    )(q, qr, gates_t, kc, vct, ks, vst, kw, vwt)


def _s5_layer(x, h0_re, h0_im, lp, nb, seq):
    m = x.shape[0]
    nc = -(-seq // S5_CHUNK)
    pad = nc * S5_CHUNK - seq
    n_blk = N_GROUPS // S5_GB
    xn = rms_norm(x, lp['norm_pre'])
    if pad == 0:
        nbk, nbp, xs = 1, nb, xn
    else:
        nbp = -(-nb // S5_CHUNK) * S5_CHUNK
        nbk = nbp
        xs = jnp.pad(xn.reshape(nb, seq, D_MODEL), ((0, nbp - nb), (pad, 0), (0, 0))).reshape(-1, D_MODEL)
    n_step = nbp // nbk
    a1, a2 = lp['a1'][pad], lp['a2'][pad]
    if h0_re is None:
        h0 = jnp.zeros((n_blk, n_step, nbk, S5_SW), F32)
    else:
        h0 = jnp.stack([h0_re, h0_im], axis=2).reshape(nb, n_blk, S5_SW)
        h0 = jnp.pad(h0, ((0, nbp - nb), (0, 0), (0, 0))).transpose(1, 0, 2).reshape(n_blk, n_step, nbk, S5_SW)
    y, hf = s5_block(xs, lp['w_lag'], lp['w_inj'], lp['w_car'][pad], a1, a2, -a2, h0, nbk, nc)
    y = y.reshape(nbp, nc * S5_CHUNK, D_MODEL)[:nb, pad:].reshape(m, D_MODEL)
    hf = hf.reshape(n_blk, nbp, S5_GB, 2, SSM_STATE)[:, :nb].transpose(3, 1, 0, 2, 4).reshape(2, nb, N_GROUPS, SSM_STATE)
    x = glu_tail(x, y, lp['norm_pre'], lp['d'], lp['w_glu'], lp['b_glu'], lp['norm_post'])
    return x, hf[0], hf[1]


def _gates_t(gates, nb, nq, tq):
    g = gates[:, :N_HEADS * N_BRANCH].reshape(nb, nq, tq, N_KV, GROUP_Q, N_BRANCH)
    return g.transpose(0, 3, 1, 5, 4, 2).reshape(nb, N_KV, nq, N_BRANCH, GROUP_Q * tq)


def _pad_rows(a, nb, seq, tq):
    if seq == tq or seq % tq == 0:
        return a
    a = a.reshape(nb, seq, -1)
    return jnp.pad(a, ((0, 0), (0, tq - seq), (0, 0))).reshape(nb * tq, -1)


def _nsa_layer(x, lp, kvs, tables, nb, seq, tq, q_off, swa_base):
    q, qr = q_proj(x, lp['norm_pre'], lp['w_q'], tables)
    gates = gate_proj(x, lp['norm_pre'], lp['w_g'])
    nq = -(-seq // tq)
    o = nsa_attention(_pad_rows(q, nb, seq, tq), _pad_rows(qr, nb, seq, tq),
                      _gates_t(_pad_rows(gates, nb, seq, tq), nb, nq, tq),
                      *kvs, nb=nb, nq=nq, tq=tq, q_off=q_off, swa_base=swa_base)
    if nq * tq != seq:
        o = o.reshape(nb, nq * tq, Q_WIDTH)[:, :seq].reshape(nb * seq, Q_WIDTH)
    return oproj(o, lp['w_o'], lp['norm_post'], x)


def _trunk(x, nb, seq, pos0, h0_re, h0_im, past, prm):
    m = nb * seq
    ssm_re, ssm_im = [], []
    n_a = len(prm['a'])
    for layer in range(n_a):
        x, hr, hi = _s5_layer(x, None if h0_re is None else h0_re[layer],
                              None if h0_im is None else h0_im[layer], prm['a'][layer], nb, seq)
        ssm_re.append(hr)
        ssm_im.append(hi)
        ml = prm['mlp'][layer]
        x = mlp(x, ml['norm_pre'], ml['w_up'], ml['w_down'], ml['norm_post'])

    pos = pos0 + jnp.tile(jnp.arange(seq), nb)
    tables = _rope_tables(pos)
    kv = kv_proj(x, prm['kv_norm'], prm['w_kv'], tables)
    rows_cmp = kv[0].reshape(nb, seq, 2, N_KV, HEAD_DIM)
    rows_slc = kv[1].reshape(nb, seq, 2, N_KV, HEAD_DIM)
    rows_swa = kv[2].reshape(nb, seq, 2, N_KV, HEAD_DIM)
    no_tail = None
    if past is None:
        npg = seq // PAGE
        table = jnp.arange(nb * npg, dtype=jnp.int32).reshape(nb, npg)
        pages = lambda j: kv[j].reshape(nb * npg, PAGE_ROWS, HEAD_DIM)
        cproj = cmp_proj(pages(0), table, prm['cmp_w1k_cat'], prm['cmp_w1v_cat'])
        ks, vst = kv_pack(pages(1), table, no_tail, SLC_TILE)
        kw, vwt = kv_pack(pages(2), table, no_tail, PAGE)
        swa_base = 0
        tq = 256
        swa_buf = rows_swa[:, seq - WINDOW:]
    else:
        cache_cmp, cache_slc, state_swa, table = past
        n_pool = cache_cmp.shape[0]
        npg = table.shape[1]
        cproj = cmp_proj(cache_cmp.reshape(n_pool, PAGE_ROWS, HEAD_DIM), table, prm['cmp_w1k_cat'], prm['cmp_w1v_cat'])
        t_real = npg * PAGE + seq
        t_pad = -(-t_real // SLC_TILE) * SLC_TILE
        tail = jnp.pad(kv[1].reshape(nb, seq, KV_SLAB),
                       ((0, 0), (0, t_pad - npg * PAGE - seq), (0, 0)))
        ks, vst = kv_pack(cache_slc.reshape(n_pool, PAGE_ROWS, HEAD_DIM), table, tail, SLC_TILE)
        w_keep = state_swa.shape[1]
        local = jnp.concatenate([state_swa.reshape(nb, w_keep, KV_SLAB), rows_swa.reshape(nb, seq, KV_SLAB)], axis=1)
        swa_buf = local[:, -w_keep:].reshape(nb, w_keep, 2, N_KV, HEAD_DIM)
        tw = -(-max(local.shape[1], SWA_SPAN) // PAGE) * PAGE
        local = jnp.pad(local, ((0, 0), (0, tw - local.shape[1]), (0, 0)))
        lt = jnp.arange(nb * (tw // PAGE), dtype=jnp.int32).reshape(nb, tw // PAGE)
        kw, vwt = kv_pack(local.reshape(nb * (tw // PAGE), PAGE_ROWS, HEAD_DIM), lt, no_tail, PAGE)
        swa_base = pos0 - w_keep
        tq = 32
    cmp_out = cmp_mlp(cproj, prm['cmp_w1'], prm['cmp_pe'], prm['cmp_w2'])
    kc = cmp_out[:, 0].astype(BF16)
    vct = cmp_out[:, 1].transpose(0, 1, 3, 2).astype(BF16)
    kvs = (kc, vct, ks, vst, kw, vwt)

    for j, lp in enumerate(prm['b']):
        x = _nsa_layer(x, lp, kvs, tables, nb, seq, tq, pos0, swa_base)
        ml = prm['mlp'][n_a + j]
        x = mlp(x, ml['norm_pre'], ml['w_up'], ml['w_down'], ml['norm_post'])
    return x, rows_cmp, rows_slc, swa_buf, jnp.stack(ssm_re), jnp.stack(ssm_im)


def kernel(x_prompt, x_sample, cache_kv_cmp, cache_kv_slc, state_kv_swa, state_ssm_re, state_ssm_im, page_table,
           a_norm_pre, a_lam_re, a_lam_im, a_log_dt, a_b_re, a_b_im, a_c_re, a_c_im, a_d, a_w_glu, a_b_glu,
           a_norm_post, kv_norm, w_kv, cmp_w1_k, cmp_pe_k, cmp_w2_k, cmp_w1_v, cmp_pe_v, cmp_w2_v, b_norm_pre,
           b_w_qg, b_w_o, b_norm_post, mlp_norm_pre, mlp_w_up, mlp_w_down, mlp_norm_post):
    n_a = a_norm_pre.shape[0]
    n_b = b_norm_pre.shape[0]
    prm = {'a': [], 'b': [], 'mlp': []}
    pads = tuple(sorted({-x_prompt.shape[1] % S5_CHUNK, -x_sample.shape[1] % S5_CHUNK}))
    for l in range(n_a):
        ops = s5_prep(a_lam_re[l], a_lam_im[l], a_log_dt[l], a_b_re[l], a_b_im[l], a_c_re[l], a_c_im[l], pads)
        prm['a'].append(dict(norm_pre=a_norm_pre[l], norm_post=a_norm_post[l], d=a_d[l],
                             w_glu=a_w_glu[l].astype(BF16), b_glu=a_b_glu[l], **ops))
    n_gate = N_HEADS * N_BRANCH
    for l in range(n_b):
        w_g = jnp.pad(b_w_qg[l][:, Q_WIDTH:], ((0, 0), (0, HEAD_DIM - n_gate))).astype(BF16)
        prm['b'].append(dict(norm_pre=b_norm_pre[l], norm_post=b_norm_post[l],
                             w_q=b_w_qg[l][:, :Q_WIDTH].astype(BF16), w_g=w_g, w_o=b_w_o[l].astype(BF16)))
    for l in range(n_a + n_b):
        prm['mlp'].append(dict(norm_pre=mlp_norm_pre[l], norm_post=mlp_norm_post[l],
                               w_up=mlp_w_up[l].astype(BF16), w_down=mlp_w_down[l].astype(BF16)))
    prm['kv_norm'] = kv_norm
    prm['w_kv'] = w_kv.astype(BF16)
    r = CMP_BLOCK // CMP_STRIDE
    cat = lambda w: w.reshape(r, CMP_FLAT, CMP_HIDDEN).transpose(1, 0, 2).reshape(CMP_FLAT, r * CMP_HIDDEN).astype(BF16)
    prm['cmp_w1k_cat'] = cat(cmp_w1_k)
    prm['cmp_w1v_cat'] = cat(cmp_w1_v)
    flat = CMP_BLOCK * HEAD_DIM
    prm['cmp_w1'] = jnp.stack([cmp_w1_k.reshape(flat, CMP_HIDDEN), cmp_w1_v.reshape(flat, CMP_HIDDEN)])
    prm['cmp_pe'] = jnp.stack([cmp_pe_k.reshape(flat, 1), cmp_pe_v.reshape(flat, 1)])
    prm['cmp_w2'] = jnp.stack([cmp_w2_k, cmp_w2_v]).astype(BF16)

    bp, sp, _ = x_prompt.shape
    y_p, cmp_p, slc_p, swa_p, re_p, im_p = _trunk(x_prompt.reshape(bp * sp, D_MODEL), bp, sp, 0,
                                                  None, None, None, prm)
    bs, ss, _ = x_sample.shape
    past_len = page_table.shape[1] * PAGE
    y_s, cmp_s, slc_s, swa_s, re_s, im_s = _trunk(x_sample.reshape(bs * ss, D_MODEL), bs, ss, past_len,
                                                  state_ssm_re, state_ssm_im,
                                                  (cache_kv_cmp, cache_kv_slc, state_kv_swa, page_table), prm)
    return (y_p.reshape(bp, sp, D_MODEL), y_s.reshape(bs, ss, D_MODEL), cmp_p, cmp_s, slc_p, slc_s,
            swa_p, swa_s, re_p, im_p, re_s, im_s)
```
